```python
import math
import jax, jax.numpy as jnp
from jax import lax
import numpy as np

D_MODEL = 2048
BATCH = 4
SEQ = 2048
DEPTH = 1
DEC_BATCH = 128
DEC_SEQ = 8
PAST_LEN = 16384
PAGE_SIZE = 128

H_A = 4
DK_A = 256
DV_A = 512
H_B = 16
DK_B = 128
DV_B = 128
CONV_W = 4
D_FF = 5632
CHUNK = 64
EPS = 1e-6
NEG = -1e30

A_QK = H_A * DK_A
A_V = H_A * DV_A
B_K = H_B * DK_B
B_V = H_B * DV_B
B_QKV = 2 * B_K + B_V
SPLIT_SIZES = [A_QK, A_QK, A_V, H_A, H_A, A_V, B_QKV, H_B, H_B, B_V, D_MODEL, D_MODEL]
D_IN = sum(SPLIT_SIZES)

kernel_name = "hybrid_mlstm_gdn_macaron_step"

F32 = jnp.float32


def rmsnorm(x, g):
    xf = x.astype(F32)
    y = xf * lax.rsqrt(jnp.mean(xf * xf, axis=-1, keepdims=True) + EPS)
    return (y * g.astype(F32)).astype(x.dtype)


def l2norm(a):
    return a * lax.rsqrt(jnp.sum(a * a, axis=-1, keepdims=True) + EPS)


def swiglu(x, wg, wu, wd):
    return (jax.nn.silu(x @ wg) * (x @ wu)) @ wd


def heads(a, h):
    b, s, _ = a.shape
    return a.reshape(b, s, h, -1).transpose(0, 2, 1, 3)


def to_chunks(a, L):
    b, h, s = a.shape[:3]
    return jnp.moveaxis(a.reshape(b, h, s // L, L, *a.shape[3:]), 2, 0)


def from_chunks(a):
    a = jnp.moveaxis(a, 0, 2)
    b, h, nc, L = a.shape[:4]
    return a.reshape(b, h, nc * L, *a.shape[4:])


def mlstm_step(carry, inp):
    C, n, m = carry
    q, k, v, ig, lf = inp
    L = q.shape[2]
    causal = jnp.tril(jnp.ones((L, L), bool))
    b = jnp.cumsum(lf, axis=-1)
    logD = jnp.where(causal, b[..., :, None] - b[..., None, :] + ig[..., None, :], NEG)
    m_inter = m[..., None] + b
    m_t = jnp.maximum(m_inter, jnp.max(logD, axis=-1))
    s = jnp.einsum('bhtd,bhsd->bhts', q, k) * jnp.exp(logD - m_t[..., None])
    inter = jnp.exp(m_inter - m_t)
    num = inter[..., None] * jnp.einsum('bhtd,bhde->bhte', q, C) + jnp.einsum('bhts,bhse->bhte', s, v)
    den = inter * jnp.einsum('bhtd,bhd->bht', q, n) + jnp.sum(s, axis=-1)
    h = num / jnp.maximum(jnp.abs(den), jnp.exp(-m_t))[..., None]
    m_new = m_t[..., -1]
    carry_decay = jnp.exp(m + b[..., -1] - m_new)
    w = jnp.exp(b[..., -1:] - b + ig - m_new[..., None])
    kw = k * w[..., None]
    C_new = carry_decay[..., None, None] * C + jnp.einsum('bhsd,bhse->bhde', kw, v)
    n_new = carry_decay[..., None] * n + jnp.sum(kw, axis=2)
    return (C_new, n_new, m_new), h


def gdn_step(S, inp):
    q, k, v, gl, beta = inp
    L = q.shape[2]
    dv = v.shape[-1]
    causal = jnp.tril(jnp.ones((L, L), bool))
    strict = jnp.tril(jnp.ones((L, L), bool), -1)
    g = jnp.cumsum(gl, axis=-1)
    decay = jnp.exp(jnp.where(causal, g[..., :, None] - g[..., None, :], NEG))
    Lm = jnp.where(strict, beta[..., :, None] * jnp.einsum('bhtk,bhsk->bhts', k, k) * decay, 0.0)
    eg = jnp.exp(g)
    rhs = jnp.concatenate([v * beta[..., None], k * (beta * eg)[..., None]], axis=-1)
    sol = lax.linalg.triangular_solve(jnp.eye(L, dtype=F32) + Lm, rhs, left_side=True, lower=True)
    u, w = sol[..., :dv], sol[..., dv:]
    delta = u - jnp.einsum('bhtk,bhkv->bhtv', w, S)
    o = eg[..., None] * jnp.einsum('bhtk,bhkv->bhtv', q, S) + jnp.einsum(
        'bhts,bhsv->bhtv', jnp.einsum('bhtk,bhsk->bhts', q, k) * decay, delta)
    gL = g[..., -1]
    S_new = jnp.exp(gL)[..., None, None] * S + jnp.einsum(
        'bhsk,bhsv->bhkv', k * jnp.exp(gL[..., None] - g)[..., None], delta)
    return S_new, o


def mixer(h, conv_buf, S0, C0, n0, m0, w_in, mlstm_b_i, mlstm_b_f, mlstm_norm, gdn_conv,
          gdn_A_log, gdn_dt_bias, gdn_norm, w_branch_a, w_branch_b, w_out):
    bsz, s, _ = h.shape
    L = math.gcd(s, CHUNK)
    offsets = [int(o) for o in np.cumsum(SPLIT_SIZES)[:-1]]
    aq, ak, av, ai, af, ao, bqkv, ba, bb, bz, ga, gb = jnp.split(h @ w_in, offsets, axis=-1)

    q = heads(aq, H_A).astype(F32)
    k = heads(ak, H_A).astype(F32) * (DK_A ** -0.5)
    v = heads(av, H_A).astype(F32)
    ig = (ai.astype(F32) + mlstm_b_i.astype(F32)).transpose(0, 2, 1)
    lf = jax.nn.log_sigmoid(af.astype(F32) + mlstm_b_f.astype(F32)).transpose(0, 2, 1)
    carry0 = (C0.astype(F32), n0.astype(F32), m0.astype(F32))
    (C1, n1, m1), hA = lax.scan(mlstm_step, carry0, tuple(to_chunks(a, L) for a in (q, k, v, ig, lf)))
    hA = from_chunks(hA).transpose(0, 2, 1, 3)
    hA = rmsnorm(hA, mlstm_norm).reshape(bsz, s, A_V) * jax.nn.sigmoid(ao.astype(F32))
    yA = hA.astype(h.dtype) @ w_branch_a

    xp = jnp.concatenate([conv_buf.astype(bqkv.dtype), bqkv], axis=1)
    conv = lax.conv_general_dilated(xp, gdn_conv[:, None, :].astype(xp.dtype), window_strides=(1,),
                                    padding='VALID', dimension_numbers=('NWC', 'WIO', 'NWC'),
                                    feature_group_count=B_QKV)
    conv = jax.nn.silu(conv.astype(F32))
    new_buf = xp[:, -(CONV_W - 1):].astype(F32)
    cq, ck, cv = jnp.split(conv, [B_K, 2 * B_K], axis=-1)
    qb = l2norm(heads(cq, H_B)) * (DK_B ** -0.5)
    kb = l2norm(heads(ck, H_B))
    vb = heads(cv, H_B)
    gl = (-jnp.exp(gdn_A_log.astype(F32)) * jax.nn.softplus(ba.astype(F32) + gdn_dt_bias.astype(F32))).transpose(0, 2, 1)
    beta = jax.nn.sigmoid(bb.astype(F32)).transpose(0, 2, 1)
    S1, oB = lax.scan(gdn_step, S0.astype(F32), tuple(to_chunks(a, L) for a in (qb, kb, vb, gl, beta)))
    oB = from_chunks(oB).transpose(0, 2, 1, 3)
    z = bz.astype(F32).reshape(bsz, s, H_B, DV_B)
    oB = (rmsnorm(oB, gdn_norm) * jax.nn.silu(z)).reshape(bsz, s, B_V)
    yB = oB.astype(h.dtype) @ w_branch_b

    merged = jax.nn.sigmoid(ga) * yA + jax.nn.sigmoid(gb) * yB
    return merged @ w_out, (new_buf, S1, C1, n1, m1)


def block(x, conv_buf, S0, C0, n0, m0, norm1, ffn1_gate, ffn1_up, ffn1_down, norm2, w_in,
          mlstm_b_i, mlstm_b_f, mlstm_norm, gdn_conv, gdn_A_log, gdn_dt_bias, gdn_norm,
          w_branch_a, w_branch_b, w_out, norm3, ffn2_gate, ffn2_up, ffn2_down):
    x = x + 0.5 * swiglu(rmsnorm(x, norm1), ffn1_gate, ffn1_up, ffn1_down)
    mix, states = mixer(rmsnorm(x, norm2), conv_buf, S0, C0, n0, m0, w_in, mlstm_b_i, mlstm_b_f,
                        mlstm_norm, gdn_conv, gdn_A_log, gdn_dt_bias, gdn_norm,
                        w_branch_a, w_branch_b, w_out)
    x = x + mix
    x = x + 0.5 * swiglu(rmsnorm(x, norm3), ffn2_gate, ffn2_up, ffn2_down)
    return x, states


def setup_inputs(seed: int = 0) -> dict:
    key = jax.random.key(seed)
    ks = iter(jax.random.split(key, 40))
    nrm = lambda shape, scale=1.0: jax.random.normal(next(ks), shape, F32) * scale
    gain = lambda shape: 1.0 + 0.02 * jax.random.normal(next(ks), shape, F32)
    dt = jnp.exp(jax.random.uniform(next(ks), (DEPTH, H_B), F32, math.log(1e-3), math.log(1e-1)))
    return {
        "x_prompt": nrm((BATCH, SEQ, D_MODEL)),
        "x_sample": nrm((DEC_BATCH, DEC_SEQ, D_MODEL)),
        "state_conv": nrm((DEPTH, DEC_BATCH, CONV_W - 1, B_QKV)),
        "state_gdn": nrm((DEPTH, DEC_BATCH, H_B, DK_B, DV_B), 0.1),
        "state_mlstm_C": nrm((DEPTH, DEC_BATCH, H_A, DK_A, DV_A), 0.1),
        "state_mlstm_n": nrm((DEPTH, DEC_BATCH, H_A, DK_A), 0.1),
        "state_mlstm_m": nrm((DEPTH, DEC_BATCH, H_A)),
        "norm1": gain((DEPTH, D_MODEL)),
        "ffn1_gate": nrm((DEPTH, D_MODEL, D_FF), D_MODEL ** -0.5),
        "ffn1_up": nrm((DEPTH, D_MODEL, D_FF), D_MODEL ** -0.5),
        "ffn1_down": nrm((DEPTH, D_FF, D_MODEL), D_FF ** -0.5),
        "norm2": gain((DEPTH, D_MODEL)),
        "w_in": nrm((DEPTH, D_MODEL, D_IN), D_MODEL ** -0.5),
        "mlstm_b_i": nrm((DEPTH, H_A), 0.1),
        "mlstm_b_f": jnp.linspace(3.0, 6.0, H_A, dtype=F32)[None, :] + nrm((DEPTH, H_A), 0.1),
        "mlstm_norm": gain((DEPTH, H_A, DV_A)),
        "gdn_conv": nrm((DEPTH, CONV_W, B_QKV), CONV_W ** -0.5),
        "gdn_A_log": jnp.log(jax.random.uniform(next(ks), (DEPTH, H_B), F32, 1.0, 16.0)),
        "gdn_dt_bias": dt + jnp.log(-jnp.expm1(-dt)),
        "gdn_norm": gain((DEPTH, DV_B)),
        "w_branch_a": nrm((DEPTH, A_V, D_MODEL), A_V ** -0.5),
        "w_branch_b": nrm((DEPTH, B_V, D_MODEL), B_V ** -0.5),
        "w_out": nrm((DEPTH, D_MODEL, D_MODEL), D_MODEL ** -0.5),
        "norm3": gain((DEPTH, D_MODEL)),
        "ffn2_gate": nrm((DEPTH, D_MODEL, D_FF), D_MODEL ** -0.5),
        "ffn2_up": nrm((DEPTH, D_MODEL, D_FF), D_MODEL ** -0.5),
        "ffn2_down": nrm((DEPTH, D_FF, D_MODEL), D_FF ** -0.5),
        "norm_f": gain((D_MODEL,)),
    }


def reference(x_prompt, x_sample, state_conv, state_gdn, state_mlstm_C, state_mlstm_n, state_mlstm_m,
              norm1, ffn1_gate, ffn1_up, ffn1_down, norm2, w_in, mlstm_b_i, mlstm_b_f, mlstm_norm,
              gdn_conv, gdn_A_log, gdn_dt_bias, gdn_norm, w_branch_a, w_branch_b, w_out,
              norm3, ffn2_gate, ffn2_up, ffn2_down, norm_f):
    def run(x, conv, S, C, n, m):
        per_layer = []
        for l in range(DEPTH):
            x, st = block(x, conv[l], S[l], C[l], n[l], m[l], norm1[l], ffn1_gate[l], ffn1_up[l],
                          ffn1_down[l], norm2[l], w_in[l], mlstm_b_i[l], mlstm_b_f[l], mlstm_norm[l],
                          gdn_conv[l], gdn_A_log[l], gdn_dt_bias[l], gdn_norm[l], w_branch_a[l],
                          w_branch_b[l], w_out[l], norm3[l], ffn2_gate[l], ffn2_up[l], ffn2_down[l])
            per_layer.append(st)
        new = [jnp.stack(z) for z in zip(*per_layer)]
        return rmsnorm(x, norm_f), new

    bp = x_prompt.shape[0]
    zeros = lambda *shape: jnp.zeros(shape, F32)
    y_prompt, (p_conv, p_gdn, p_C, p_n, p_m) = run(
        x_prompt, zeros(DEPTH, bp, CONV_W - 1, B_QKV), zeros(DEPTH, bp, H_B, DK_B, DV_B),
        zeros(DEPTH, bp, H_A, DK_A, DV_A), zeros(DEPTH, bp, H_A, DK_A), zeros(DEPTH, bp, H_A))
    y_sample, (s_conv, s_gdn, s_C, s_n, s_m) = run(
        x_sample, state_conv, state_gdn, state_mlstm_C, state_mlstm_n, state_mlstm_m)
    return (y_prompt, y_sample, p_conv, p_gdn, p_C, p_n, p_m, s_conv, s_gdn, s_C, s_n, s_m)
```

```python
import functools
import math

import jax
import jax.numpy as jnp
from jax import lax
from jax.experimental import pallas as pl
from jax.experimental.pallas import tpu as pltpu

F32 = jnp.float32
BF16 = jnp.bfloat16
HIGHEST = lax.Precision.HIGHEST
EPS = 1e-6
NEG = -1e30
CHUNK = 64
GATE_LANES = 128
V7X_VMEM_BYTES = 64 * 1024 * 1024
VMEM_LIMIT = V7X_VMEM_BYTES - 8 * 1024 * 1024

NT_DIMS = (((1,), (1,)), ((), ()))
TN_DIMS = (((0,), (0,)), ((), ()))


def _block(n, preferred):
    return math.gcd(n, preferred)


def _params(*semantics):
    return pltpu.CompilerParams(dimension_semantics=semantics, vmem_limit_bytes=VMEM_LIMIT)


def _sigmoid(x):
    return 1.0 / (1.0 + jnp.exp(-x))


def _softplus(x):
    return jnp.maximum(x, 0.0) + jnp.log1p(jnp.exp(-jnp.abs(x)))


def _rms_rows(x, gain):
    return x * lax.rsqrt(jnp.mean(x * x, axis=-1, keepdims=True) + EPS) * gain


def _row_of(col, eye):
    return jnp.sum(jnp.where(eye, col, 0.0), axis=0, keepdims=True)


def _mm(a, b):
    return jnp.dot(a, b, preferred_element_type=F32)


def _ffn_up_body(x_ref, g_ref, wg_ref, wu_ref, act_ref, h_ref):
    @pl.when(pl.program_id(1) == 0)
    def _():
        h_ref[...] = _rms_rows(x_ref[...], g_ref[...]).astype(BF16)

    h = h_ref[...]
    gate = _mm(h, wg_ref[...])
    up = _mm(h, wu_ref[...])
    act_ref[...] = (gate * _sigmoid(gate) * up).astype(BF16)


def ffn_up(x, gain, wg, wu, *, bm, bf):
    m, d = x.shape
    ff = wg.shape[1]
    return pl.pallas_call(
        _ffn_up_body,
        grid=(m // bm, ff // bf),
        in_specs=[
            pl.BlockSpec((bm, d), lambda i, j: (i, 0)),
            pl.BlockSpec((1, d), lambda i, j: (0, 0)),
            pl.BlockSpec((d, bf), lambda i, j: (0, j)),
            pl.BlockSpec((d, bf), lambda i, j: (0, j)),
        ],
        out_specs=pl.BlockSpec((bm, bf), lambda i, j: (i, j)),
        out_shape=jax.ShapeDtypeStruct((m, ff), BF16),
        scratch_shapes=[pltpu.VMEM((bm, d), BF16)],
        compiler_params=_params("parallel", "arbitrary"),
        name="ffn_up",
    )(x, gain.reshape(1, d), wg, wu)


def _mm_res_body(a_ref, w_ref, x_ref, o_ref, *, scale):
    o_ref[...] = x_ref[...] + scale * _mm(a_ref[...], w_ref[...])


def matmul_residual(a, w, x, *, scale, bm, bn):
    m, k = a.shape
    n = w.shape[1]
    return pl.pallas_call(
        functools.partial(_mm_res_body, scale=scale),
        grid=(m // bm, n // bn),
        in_specs=[
            pl.BlockSpec((bm, k), lambda i, j: (i, 0)),
            pl.BlockSpec((k, bn), lambda i, j: (0, j)),
            pl.BlockSpec((bm, bn), lambda i, j: (i, j)),
        ],
        out_specs=pl.BlockSpec((bm, bn), lambda i, j: (i, j)),
        out_shape=jax.ShapeDtypeStruct((m, n), F32),
        compiler_params=_params("parallel", "arbitrary"),
        name="matmul_residual",
    )(a, w, x)


def _in_proj_body(x_ref, g_ref, w_ref, ws_ref, p_ref, ps_ref, h_ref):
    @pl.when(pl.program_id(1) == 0)
    def _():
        h = _rms_rows(x_ref[...], g_ref[...])
        h_ref[...] = h.astype(BF16)
        ps_ref[...] = jnp.dot(h, ws_ref[...], precision=HIGHEST, preferred_element_type=F32)

    p_ref[...] = _mm(h_ref[...], w_ref[...])


def in_proj(x, gain, w_big, w_small, *, bm, bn):
    m, d = x.shape
    n = w_big.shape[1]
    return pl.pallas_call(
        _in_proj_body,
        grid=(m // bm, n // bn),
        in_specs=[
            pl.BlockSpec((bm, d), lambda i, j: (i, 0)),
            pl.BlockSpec((1, d), lambda i, j: (0, 0)),
            pl.BlockSpec((d, bn), lambda i, j: (0, j)),
            pl.BlockSpec((d, GATE_LANES), lambda i, j: (0, 0)),
        ],
        out_specs=[
            pl.BlockSpec((bm, bn), lambda i, j: (i, j)),
            pl.BlockSpec((bm, GATE_LANES), lambda i, j: (i, 0)),
        ],
        out_shape=[
            jax.ShapeDtypeStruct((m, n), F32),
            jax.ShapeDtypeStruct((m, GATE_LANES), F32),
        ],
        scratch_shapes=[pltpu.VMEM((bm, d), BF16)],
        compiler_params=_params("parallel", "arbitrary"),
        name="in_proj",
    )(x, gain.reshape(1, d), w_big, w_small)


def _merge_body(ha_ref, ob_ref, wa_ref, wb_ref, ga_ref, gb_ref, o_ref):
    ya = _mm(ha_ref[...], wa_ref[...])
    yb = _mm(ob_ref[...], wb_ref[...])
    o_ref[...] = (_sigmoid(ga_ref[...]) * ya + _sigmoid(gb_ref[...]) * yb).astype(BF16)


def merge(ha, ob, wa, wb, p_big, *, ga_off, gb_off, bm, bn):
    m, ka = ha.shape
    kb = ob.shape[1]
    n = wa.shape[1]
    ga_blk, gb_blk = ga_off // bn, gb_off // bn
    return pl.pallas_call(
        _merge_body,
        grid=(m // bm, n // bn),
        in_specs=[
            pl.BlockSpec((bm, ka), lambda i, j: (i, 0)),
            pl.BlockSpec((bm, kb), lambda i, j: (i, 0)),
            pl.BlockSpec((ka, bn), lambda i, j: (0, j)),
            pl.BlockSpec((kb, bn), lambda i, j: (0, j)),
            pl.BlockSpec((bm, bn), lambda i, j: (i, ga_blk + j)),
            pl.BlockSpec((bm, bn), lambda i, j: (i, gb_blk + j)),
        ],
        out_specs=pl.BlockSpec((bm, bn), lambda i, j: (i, j)),
        out_shape=jax.ShapeDtypeStruct((m, n), BF16),
        compiler_params=_params("parallel", "arbitrary"),
        name="merge",
    )(ha, ob, wa, wb, p_big, p_big)


def _rmsnorm_body(x_ref, g_ref, o_ref):
    o_ref[...] = _rms_rows(x_ref[...], g_ref[...])


def rmsnorm_rows(x, gain, *, bm):
    m, d = x.shape
    return pl.pallas_call(
        _rmsnorm_body,
        grid=(m // bm,),
        in_specs=[pl.BlockSpec((bm, d), lambda i: (i, 0)), pl.BlockSpec((1, d), lambda i: (0, 0))],
        out_specs=pl.BlockSpec((bm, d), lambda i: (i, 0)),
        out_shape=jax.ShapeDtypeStruct((m, d), F32),
        compiler_params=_params("parallel"),
        name="rmsnorm_rows",
    )(x, gain.reshape(1, d))


def _mlstm_body(q_ref, k_ref, v_ref, ao_ref, gc_ref, gbias_ref, norm_ref, c0_ref, n0_ref, m0_ref,
                ha_ref, c_ref, n_ref, m_ref, *, heads, dk, dv, chunk):
    L = chunk

    @pl.when(pl.program_id(1) == 0)
    def _():
        c_ref[...] = c0_ref[...]
        n_ref[...] = n0_ref[...]
        m_ref[...] = m0_ref[...]

    row = lax.broadcasted_iota(jnp.int32, (L, L), 0)
    col = lax.broadcasted_iota(jnp.int32, (L, L), 1)
    eye = row == col
    causal = row >= col
    tri = jnp.where(causal, 1.0, 0.0).astype(F32)

    gates = gc_ref[...] + gbias_ref[...]
    log_f = jnp.minimum(gates, 0.0) - jnp.log1p(jnp.exp(-jnp.abs(gates)))
    b_all = jnp.dot(tri, log_f, precision=HIGHEST, preferred_element_type=F32)
    m_row = m_ref[0]
    lane = lax.broadcasted_iota(jnp.int32, m_row.shape, 1)
    m_next = m_row

    for h in range(heads):
        ig_c = gates[:, h:h + 1]
        b_c = b_all[:, heads + h:heads + h + 1]
        ig_r = _row_of(ig_c, eye)
        b_r = _row_of(b_c, eye)
        m_prev = m_row[0:1, h:h + 1]

        log_d = jnp.where(causal, b_c - b_r + ig_r, NEG)
        m_inter = m_prev + b_c
        m_t = jnp.maximum(m_inter, jnp.max(log_d, axis=-1, keepdims=True))
        d_mat = jnp.exp(log_d - m_t)
        inter = jnp.exp(m_inter - m_t)

        q = q_ref[:, h * dk:(h + 1) * dk]
        k = k_ref[:, h * dk:(h + 1) * dk] * (dk ** -0.5)
        v = v_ref[:, h * dv:(h + 1) * dv]
        qb, kb, vb = q.astype(BF16), k.astype(BF16), v.astype(BF16)
        c_old = c_ref[0, h]
        n_old = n_ref[0, h:h + 1, :]

        s = lax.dot_general(qb, kb, NT_DIMS, preferred_element_type=F32) * d_mat
        num = inter * _mm(qb, c_old.astype(BF16)) + _mm(s.astype(BF16), vb)
        den = inter * jnp.sum(q * n_old, axis=-1, keepdims=True) + jnp.sum(s, axis=-1, keepdims=True)
        h_out = num / jnp.maximum(jnp.abs(den), jnp.exp(-m_t))

        m_new = m_t[L - 1:L, :]
        b_last = b_c[L - 1:L, :]
        carry = jnp.exp(m_prev + b_last - m_new)
        kw = k * jnp.exp(b_last - b_c + ig_c - m_new)
        c_ref[0, h] = carry * c_old + lax.dot_general(kw.astype(BF16), vb, TN_DIMS,
                                                      preferred_element_type=F32)
        n_ref[0, h:h + 1, :] = carry * n_old + jnp.sum(kw, axis=0, keepdims=True)
        m_next = jnp.where(lane == h, m_new, m_next)

        hn = _rms_rows(h_out, norm_ref[:, h * dv:(h + 1) * dv])
        ha_ref[:, h * dv:(h + 1) * dv] = (hn * _sigmoid(ao_ref[:, h * dv:(h + 1) * dv])).astype(BF16)

    m_ref[0] = m_next


def mlstm(p_big, p_small, gbias, norm_row, c0, n0, m0, *, row0, seq, chunk, heads, dk, dv,
          q_off, k_off, v_off, ao_off):
    bsz = c0.shape[0]
    nc = seq // chunk
    rb0 = row0 // chunk
    qk_w, v_w = heads * dk, heads * dv
    rows = lambda b, c: rb0 + b * nc + c
    body = functools.partial(_mlstm_body, heads=heads, dk=dk, dv=dv, chunk=chunk)
    return pl.pallas_call(
        body,
        grid=(bsz, nc),
        in_specs=[
            pl.BlockSpec((chunk, qk_w), lambda b, c: (rows(b, c), q_off // qk_w)),
            pl.BlockSpec((chunk, qk_w), lambda b, c: (rows(b, c), k_off // qk_w)),
            pl.BlockSpec((chunk, v_w), lambda b, c: (rows(b, c), v_off // v_w)),
            pl.BlockSpec((chunk, v_w), lambda b, c: (rows(b, c), ao_off // v_w)),
            pl.BlockSpec((chunk, GATE_LANES), lambda b, c: (rows(b, c), 0)),
            pl.BlockSpec((1, GATE_LANES), lambda b, c: (0, 0)),
            pl.BlockSpec((1, v_w), lambda b, c: (0, 0)),
            pl.BlockSpec((1, heads, dk, dv), lambda b, c: (b, 0, 0, 0)),
            pl.BlockSpec((1, heads, dk), lambda b, c: (b, 0, 0)),
            pl.BlockSpec((1, 8, GATE_LANES), lambda b, c: (b, 0, 0)),
        ],
        out_specs=[
            pl.BlockSpec((chunk, v_w), lambda b, c: (b * nc + c, 0)),
            pl.BlockSpec((1, heads, dk, dv), lambda b, c: (b, 0, 0, 0)),
            pl.BlockSpec((1, heads, dk), lambda b, c: (b, 0, 0)),
            pl.BlockSpec((1, 8, GATE_LANES), lambda b, c: (b, 0, 0)),
        ],
        out_shape=[
            jax.ShapeDtypeStruct((bsz * seq, v_w), BF16),
            jax.ShapeDtypeStruct(c0.shape, F32),
            jax.ShapeDtypeStruct(n0.shape, F32),
            jax.ShapeDtypeStruct(m0.shape, F32),
        ],
        compiler_params=_params("parallel", "arbitrary"),
        name="mlstm",
    )(p_big, p_big, p_big, p_big, p_small, gbias, norm_row, c0, n0, m0)


def _shift_rows(x, prev, j):
    xs = pltpu.roll(x, j, axis=0)
    row = lax.broadcasted_iota(jnp.int32, prev.shape, 0)
    head = jnp.where(row < j, pltpu.roll(prev, j, axis=0), xs[0:8])
    if x.shape[0] == 8:
        return head
    return jnp.concatenate([head, xs[8:]], axis=0)


def _conv_silu(x_ref, w_ref, tail_ref, width):
    x = x_ref[...]
    prev = tail_ref[0]
    acc = x * w_ref[width - 1:width, :]
    for j in range(1, width):
        acc = acc + _shift_rows(x, prev, j) * w_ref[width - 1 - j:width - j, :]
    tail_ref[0] = x[x.shape[0] - 8:, :]
    return acc * _sigmoid(acc)


def _gdn_body(xq_ref, xk_ref, xv_ref, z_ref, wq_ref, wk_ref, wv_ref, tq0_ref, tk0_ref, tv0_ref,
              gc_ref, gbias_ref, alog_ref, norm_ref, s0_ref,
              ob_ref, s_ref, tq_ref, tk_ref, tv_ref, *, hb, dk, dv, chunk, width, gl_col, beta_col):
    L = chunk

    @pl.when(pl.program_id(2) == 0)
    def _():
        s_ref[...] = s0_ref[...]
        tq_ref[...] = tq0_ref[...]
        tk_ref[...] = tk0_ref[...]
        tv_ref[...] = tv0_ref[...]

    cq = _conv_silu(xq_ref, wq_ref, tq_ref, width)
    ck = _conv_silu(xk_ref, wk_ref, tk_ref, width)
    cv = _conv_silu(xv_ref, wv_ref, tv_ref, width)

    row = lax.broadcasted_iota(jnp.int32, (L, L), 0)
    col = lax.broadcasted_iota(jnp.int32, (L, L), 1)
    eye = row == col
    causal = row >= col
    strict = row > col
    tri = jnp.where(causal, 1.0, 0.0).astype(F32)
    ident = jnp.where(eye, 1.0, 0.0).astype(F32)

    raw = gc_ref[...] + gbias_ref[...]
    gl_all = -jnp.exp(alog_ref[...]) * _softplus(raw)
    g_all = jnp.dot(tri, gl_all, precision=HIGHEST, preferred_element_type=F32)
    beta_all = _sigmoid(raw)
    lane = lax.broadcasted_iota(jnp.int32, raw.shape, 1)
    head0 = pl.program_id(1) * hb
    n_sq = max(int(math.ceil(math.log2(L))) - 1, 0)

    for i in range(hb):
        g_c = jnp.sum(jnp.where(lane == gl_col + head0 + i, g_all, 0.0), axis=-1, keepdims=True)
        beta_c = jnp.sum(jnp.where(lane == beta_col + head0 + i, beta_all, 0.0), axis=-1, keepdims=True)
        g_r = _row_of(g_c, eye)
        eg_c = jnp.exp(g_c)
        decay = jnp.exp(jnp.where(causal, g_c - g_r, NEG))

        q = cq[:, i * dk:(i + 1) * dk]
        k = ck[:, i * dk:(i + 1) * dk]
        v = cv[:, i * dv:(i + 1) * dv]
        q = q * lax.rsqrt(jnp.sum(q * q, axis=-1, keepdims=True) + EPS) * (dk ** -0.5)
        k = k * lax.rsqrt(jnp.sum(k * k, axis=-1, keepdims=True) + EPS)
        qb, kb = q.astype(BF16), k.astype(BF16)

        kk = lax.dot_general(kb, kb, NT_DIMS, preferred_element_type=F32)
        x_pow = jnp.where(strict, -(beta_c * kk * decay), 0.0)
        t_inv = ident + x_pow
        for _ in range(n_sq):
            x_pow = jnp.dot(x_pow, x_pow, precision=HIGHEST, preferred_element_type=F32)
            t_inv = t_inv + jnp.dot(t_inv, x_pow, precision=HIGHEST, preferred_element_type=F32)
        rhs = jnp.concatenate([v * beta_c, k * (beta_c * eg_c)], axis=-1)
        sol = jnp.dot(t_inv, rhs, precision=HIGHEST, preferred_element_type=F32)
        u, w = sol[:, :dv], sol[:, dv:]

        s_old = s_ref[0, i]
        sb = s_old.astype(BF16)
        delta = u - _mm(w.astype(BF16), sb)
        db = delta.astype(BF16)
        qk = lax.dot_general(qb, kb, NT_DIMS, preferred_element_type=F32) * decay
        o = eg_c * _mm(qb, sb) + _mm(qk.astype(BF16), db)
        g_last = g_c[L - 1:L, :]
        kd = k * jnp.exp(g_last - g_c)
        s_ref[0, i] = jnp.exp(g_last) * s_old + lax.dot_general(kd.astype(BF16), db, TN_DIMS,
                                                                preferred_element_type=F32)

        z = z_ref[:, i * dv:(i + 1) * dv]
        ob_ref[:, i * dv:(i + 1) * dv] = (_rms_rows(o, norm_ref[...]) * (z * _sigmoid(z))).astype(BF16)


def gdn(p_big, p_small, gbias, alog_row, norm_row, conv_w, tails0, s0, *, row0, seq, chunk, hb,
        q_off, k_off, v_off, z_off, gl_col, beta_col):
    bsz, heads, dk, dv = s0.shape
    width = conv_w[0].shape[0]
    nc = seq // chunk
    rb0 = row0 // chunk
    gw = hb * dk
    ng = heads // hb
    rows = lambda b, g, c: rb0 + b * nc + c
    x_spec = lambda off: pl.BlockSpec((chunk, gw), lambda b, g, c: (rows(b, g, c), off // gw + g))
    w_spec = pl.BlockSpec((width, gw), lambda b, g, c: (0, g))
    tail_spec = pl.BlockSpec((1, 8, gw), lambda b, g, c: (b, 0, g))
    s_spec = pl.BlockSpec((1, hb, dk, dv), lambda b, g, c: (b, g, 0, 0))
    vec_spec = lambda n: pl.BlockSpec((1, n), lambda b, g, c: (0, 0))
    body = functools.partial(_gdn_body, hb=hb, dk=dk, dv=dv, chunk=chunk, width=width,
                             gl_col=gl_col, beta_col=beta_col)
    tail_shape = jax.ShapeDtypeStruct(tails0[0].shape, F32)
    return pl.pallas_call(
        body,
        grid=(bsz, ng, nc),
        in_specs=[
            x_spec(q_off), x_spec(k_off), x_spec(v_off), x_spec(z_off),
            w_spec, w_spec, w_spec, tail_spec, tail_spec, tail_spec,
            pl.BlockSpec((chunk, GATE_LANES), lambda b, g, c: (rows(b, g, c), 0)),
            vec_spec(GATE_LANES), vec_spec(GATE_LANES), vec_spec(dv), s_spec,
        ],
        out_specs=[
            pl.BlockSpec((chunk, gw), lambda b, g, c: (b * nc + c, g)),
            s_spec, tail_spec, tail_spec, tail_spec,
        ],
        out_shape=[
            jax.ShapeDtypeStruct((bsz * seq, heads * dv), BF16),
            jax.ShapeDtypeStruct(s0.shape, F32),
            tail_shape, tail_shape, tail_shape,
        ],
        compiler_params=_params("parallel", "parallel", "arbitrary"),
        name="gdn",
    )(p_big, p_big, p_big, p_big, conv_w[0], conv_w[1], conv_w[2], tails0[0], tails0[1], tails0[2],
      p_small, gbias, alog_row, norm_row, s0)


def _pad_lanes(vec, offset):
    return jnp.zeros((1, GATE_LANES), F32).at[0, offset:offset + vec.shape[0]].set(vec.astype(F32))


def _layer(x, streams, lw, dims):
    h_a, dk_a, dv_a, h_b, dk_b, dv_b, width = dims
    a_qk, a_v, b_k, b_v = h_a * dk_a, h_a * dv_a, h_b * dk_b, h_b * dv_b
    m, d = x.shape
    ff = lw["ffn1_gate"].shape[1]
    bm, bf, bd = _block(m, 512), _block(ff, 512), _block(d, 512)

    act = ffn_up(x, lw["norm1"], lw["ffn1_gate"], lw["ffn1_up"], bm=bm, bf=bf)
    x = matmul_residual(act, lw["ffn1_down"], x, scale=0.5, bm=bm, bn=bd)

    offs = dict(q=0, k=a_qk, v=2 * a_qk, ao=2 * a_qk + a_v)
    offs["bq"] = offs["ao"] + a_v
    offs["bk"] = offs["bq"] + b_k
    offs["bv"] = offs["bk"] + b_k
    offs["bz"] = offs["bv"] + b_v
    offs["ga"] = offs["bz"] + b_v
    offs["gb"] = offs["ga"] + d
    p_big, p_small = in_proj(x, lw["norm2"], lw["w_big"], lw["w_small"], bm=bm,
                             bn=_block(lw["w_big"].shape[1], 1024))

    gl_col, beta_col = 2 * h_a, 2 * h_a + h_b
    gbias = (_pad_lanes(lw["mlstm_b_i"], 0) + _pad_lanes(lw["mlstm_b_f"], h_a)
             + _pad_lanes(lw["gdn_dt_bias"], gl_col))
    alog_row = _pad_lanes(lw["gdn_A_log"], gl_col)
    conv_w = [lw["gdn_conv"][:, i * b_k:(i + 1) * b_k] for i in range(3)]

    ha_parts, ob_parts, new_states = [], [], []
    for st in streams:
        bsz = st["bsz"]
        m0 = jnp.zeros((bsz, 8, GATE_LANES), F32).at[:, :, :h_a].set(
            jnp.broadcast_to(st["m"][:, None, :], (bsz, 8, h_a)))
        ha, c1, n1, m1 = mlstm(
            p_big, p_small, gbias, lw["mlstm_norm"].reshape(1, a_v), st["C"], st["n"], m0,
            row0=st["row0"], seq=st["seq"], chunk=st["chunk_a"], heads=h_a, dk=dk_a, dv=dv_a,
            q_off=offs["q"], k_off=offs["k"], v_off=offs["v"], ao_off=offs["ao"])
        tails0 = [jnp.pad(st["conv"][:, :, i * b_k:(i + 1) * b_k], ((0, 0), (8 - (width - 1), 0), (0, 0)))
                  for i in range(3)]
        ob, s1, tq, tk, tv = gdn(
            p_big, p_small, gbias, alog_row, lw["gdn_norm"].reshape(1, dv_b), conv_w, tails0, st["S"],
            row0=st["row0"], seq=st["seq"], chunk=st["chunk_b"], hb=4,
            q_off=offs["bq"], k_off=offs["bk"], v_off=offs["bv"], z_off=offs["bz"],
            gl_col=gl_col, beta_col=beta_col)
        conv1 = jnp.concatenate([tq, tk, tv], axis=-1)[:, 8 - (width - 1):, :]
        ha_parts.append(ha)
        ob_parts.append(ob)
        new_states.append((conv1, s1, c1, n1, m1[:, 0, :h_a]))

    ha = jnp.concatenate(ha_parts, axis=0)
    ob = jnp.concatenate(ob_parts, axis=0)

    merged = merge(ha, ob, lw["w_branch_a"], lw["w_branch_b"], p_big,
                   ga_off=offs["ga"], gb_off=offs["gb"], bm=bm, bn=bd)
    x = matmul_residual(merged, lw["w_out"], x, scale=1.0, bm=bm, bn=bd)

    act = ffn_up(x, lw["norm3"], lw["ffn2_gate"], lw["ffn2_up"], bm=bm, bf=bf)
    x = matmul_residual(act, lw["ffn2_down"], x, scale=0.5, bm=bm, bn=bd)
    return x, new_states


def _prep_layer_weights(l, w, dims):
    h_a, dk_a, dv_a, h_b, dk_b, dv_b, _ = dims
    a_qk, a_v, b_k, b_v = h_a * dk_a, h_a * dv_a, h_b * dk_b, h_b * dv_b
    d = w["w_in"].shape[1]
    sizes = [a_qk, a_qk, a_v, h_a, h_a, a_v, 2 * b_k + b_v, h_b, h_b, b_v, d, d]
    starts = [0]
    for s in sizes:
        starts.append(starts[-1] + s)
    w_in = w["w_in"][l]
    seg = lambda i: w_in[:, starts[i]:starts[i + 1]]
    w_big = jnp.concatenate([seg(0), seg(1), seg(2), seg(5), seg(6), seg(9), seg(10), seg(11)],
                            axis=1).astype(BF16)
    small = jnp.concatenate([seg(3), seg(4), seg(7), seg(8)], axis=1)
    w_small = jnp.pad(small, ((0, 0), (0, GATE_LANES - small.shape[1])))
    lw = {k: w[k][l] for k in ("norm1", "norm2", "norm3", "mlstm_b_i", "mlstm_b_f", "mlstm_norm",
                                "gdn_conv", "gdn_A_log", "gdn_dt_bias", "gdn_norm")}
    for k in ("ffn1_gate", "ffn1_up", "ffn1_down", "ffn2_gate", "ffn2_up", "ffn2_down",
              "w_branch_a", "w_branch_b", "w_out"):
        lw[k] = w[k][l].astype(BF16)
    lw["w_big"], lw["w_small"] = w_big, w_small
    return lw


def kernel(x_prompt, x_sample, state_conv, state_gdn, state_mlstm_C, state_mlstm_n, state_mlstm_m,
           norm1, ffn1_gate, ffn1_up, ffn1_down, norm2, w_in, mlstm_b_i, mlstm_b_f, mlstm_norm,
           gdn_conv, gdn_A_log, gdn_dt_bias, gdn_norm, w_branch_a, w_branch_b, w_out,
           norm3, ffn2_gate, ffn2_up, ffn2_down, norm_f):
    depth = norm1.shape[0]
    bp, sp, d = x_prompt.shape
    bs, ss, _ = x_sample.shape
    _, _, h_a, dk_a, dv_a = state_mlstm_C.shape
    _, _, h_b, dk_b, dv_b = state_gdn.shape
    width = gdn_conv.shape[1]
    dims = (h_a, dk_a, dv_a, h_b, dk_b, dv_b, width)
    w = dict(norm1=norm1, ffn1_gate=ffn1_gate, ffn1_up=ffn1_up, ffn1_down=ffn1_down, norm2=norm2,
             w_in=w_in, mlstm_b_i=mlstm_b_i, mlstm_b_f=mlstm_b_f, mlstm_norm=mlstm_norm,
             gdn_conv=gdn_conv, gdn_A_log=gdn_A_log, gdn_dt_bias=gdn_dt_bias, gdn_norm=gdn_norm,
             w_branch_a=w_branch_a, w_branch_b=w_branch_b, w_out=w_out, norm3=norm3,
             ffn2_gate=ffn2_gate, ffn2_up=ffn2_up, ffn2_down=ffn2_down)

    x = jnp.concatenate([x_prompt.reshape(bp * sp, d), x_sample.reshape(bs * ss, d)], axis=0)
    zeros = lambda *shape: jnp.zeros(shape, F32)
    prompt = dict(row0=0, bsz=bp, seq=sp, chunk_a=math.gcd(sp, 256), chunk_b=math.gcd(sp, CHUNK))
    sample = dict(row0=bp * sp, bsz=bs, seq=ss, chunk_a=math.gcd(ss, 256), chunk_b=math.gcd(ss, CHUNK))

    per_layer = [[], []]
    for l in range(depth):
        lw = _prep_layer_weights(l, w, dims)
        prompt.update(conv=zeros(bp, width - 1, (2 * dk_b + dv_b) * h_b), S=zeros(bp, h_b, dk_b, dv_b),
                      C=zeros(bp, h_a, dk_a, dv_a), n=zeros(bp, h_a, dk_a), m=zeros(bp, h_a))
        sample.update(conv=state_conv[l], S=state_gdn[l], C=state_mlstm_C[l], n=state_mlstm_n[l],
                      m=state_mlstm_m[l])
        x, states = _layer(x, [prompt, sample], lw, dims)
        per_layer[0].append(states[0])
        per_layer[1].append(states[1])

    y = rmsnorm_rows(x, norm_f, bm=_block(x.shape[0], 512))
    y_prompt = y[:bp * sp].reshape(bp, sp, d)
    y_sample = y[bp * sp:].reshape(bs, ss, d)
    stack = lambda sts: tuple(jnp.stack(z) for z in zip(*sts))
    return (y_prompt, y_sample) + stack(per_layer[0]) + stack(per_layer[1])
```

```python
import functools
import math

import jax
import jax.numpy as jnp
from jax import lax
from jax.experimental import pallas as pl
from jax.experimental.pallas import tpu as pltpu

F32 = jnp.float32
BF16 = jnp.bfloat16
HIGHEST = lax.Precision.HIGHEST
EPS = 1e-6
NEG = -1e30
CHUNK = 64
GATE_LANES = 128
V7X_VMEM_BYTES = 64 * 1024 * 1024
VMEM_LIMIT = V7X_VMEM_BYTES - 8 * 1024 * 1024

NT_DIMS = (((1,), (1,)), ((), ()))
TN_DIMS = (((0,), (0,)), ((), ()))


def _block(n, preferred):
    return math.gcd(n, preferred)


def _params(*semantics):
    return pltpu.CompilerParams(dimension_semantics=semantics, vmem_limit_bytes=VMEM_LIMIT)


def _sigmoid(x):
    return 1.0 / (1.0 + jnp.exp(-x))


def _softplus(x):
    return jnp.maximum(x, 0.0) + jnp.log1p(jnp.exp(-jnp.abs(x)))


def _rms_rows(x, gain):
    return x * lax.rsqrt(jnp.mean(x * x, axis=-1, keepdims=True) + EPS) * gain


def _row_of(col, eye):
    return jnp.sum(jnp.where(eye, col, 0.0), axis=0, keepdims=True)


def _mm(a, b):
    return jnp.dot(a, b, preferred_element_type=F32)


def _ffn_up_body(x_ref, g_ref, wg_ref, wu_ref, act_ref, h_ref):
    @pl.when(pl.program_id(1) == 0)
    def _():
        h_ref[...] = _rms_rows(x_ref[...], g_ref[...]).astype(BF16)

    h = h_ref[...]
    gate = _mm(h, wg_ref[...])
    up = _mm(h, wu_ref[...])
    act_ref[...] = (gate * _sigmoid(gate) * up).astype(BF16)


def ffn_up(x, gain, wg, wu, *, bm, bf):
    m, d = x.shape
    ff = wg.shape[1]
    return pl.pallas_call(
        _ffn_up_body,
        grid=(m // bm, ff // bf),
        in_specs=[
            pl.BlockSpec((bm, d), lambda i, j: (i, 0)),
            pl.BlockSpec((1, d), lambda i, j: (0, 0)),
            pl.BlockSpec((d, bf), lambda i, j: (0, j)),
            pl.BlockSpec((d, bf), lambda i, j: (0, j)),
        ],
        out_specs=pl.BlockSpec((bm, bf), lambda i, j: (i, j)),
        out_shape=jax.ShapeDtypeStruct((m, ff), BF16),
        scratch_shapes=[pltpu.VMEM((bm, d), BF16)],
        compiler_params=_params("parallel", "arbitrary"),
        name="ffn_up",
    )(x, gain.reshape(1, d), wg, wu)


def _mm_res_body(a_ref, w_ref, x_ref, o_ref, *, scale):
    o_ref[...] = x_ref[...] + scale * _mm(a_ref[...], w_ref[...])


def matmul_residual(a, w, x, *, scale, bm, bn):
    m, k = a.shape
    n = w.shape[1]
    return pl.pallas_call(
        functools.partial(_mm_res_body, scale=scale),
        grid=(m // bm, n // bn),
        in_specs=[
            pl.BlockSpec((bm, k), lambda i, j: (i, 0)),
            pl.BlockSpec((k, bn), lambda i, j: (0, j)),
            pl.BlockSpec((bm, bn), lambda i, j: (i, j)),
        ],
        out_specs=pl.BlockSpec((bm, bn), lambda i, j: (i, j)),
        out_shape=jax.ShapeDtypeStruct((m, n), F32),
        compiler_params=_params("parallel", "arbitrary"),
        name="matmul_residual",
    )(a, w, x)


def _in_proj_body(x_ref, g_ref, w_ref, ws_ref, p_ref, ps_ref, h_ref):
    @pl.when(pl.program_id(1) == 0)
    def _():
        h = _rms_rows(x_ref[...], g_ref[...])
        h_ref[...] = h.astype(BF16)
        ps_ref[...] = jnp.dot(h, ws_ref[...], precision=HIGHEST, preferred_element_type=F32)

    p_ref[...] = _mm(h_ref[...], w_ref[...])


def in_proj(x, gain, w_big, w_small, *, bm, bn):
    m, d = x.shape
    n = w_big.shape[1]
    return pl.pallas_call(
        _in_proj_body,
        grid=(m // bm, n // bn),
        in_specs=[
            pl.BlockSpec((bm, d), lambda i, j: (i, 0)),
            pl.BlockSpec((1, d), lambda i, j: (0, 0)),
            pl.BlockSpec((d, bn), lambda i, j: (0, j)),
            pl.BlockSpec((d, GATE_LANES), lambda i, j: (0, 0)),
        ],
        out_specs=[
            pl.BlockSpec((bm, bn), lambda i, j: (i, j)),
            pl.BlockSpec((bm, GATE_LANES), lambda i, j: (i, 0)),
        ],
        out_shape=[
            jax.ShapeDtypeStruct((m, n), F32),
            jax.ShapeDtypeStruct((m, GATE_LANES), F32),
        ],
        scratch_shapes=[pltpu.VMEM((bm, d), BF16)],
        compiler_params=_params("parallel", "arbitrary"),
        name="in_proj",
    )(x, gain.reshape(1, d), w_big, w_small)


def _merge_body(ha_ref, ob_ref, wa_ref, wb_ref, ga_ref, gb_ref, o_ref):
    ya = _mm(ha_ref[...], wa_ref[...])
    yb = _mm(ob_ref[...], wb_ref[...])
    o_ref[...] = (_sigmoid(ga_ref[...]) * ya + _sigmoid(gb_ref[...]) * yb).astype(BF16)


def merge(ha, ob, wa, wb, p_big, *, ga_off, gb_off, bm, bn):
    m, ka = ha.shape
    kb = ob.shape[1]
    n = wa.shape[1]
    assert ga_off % bn == 0 and gb_off % bn == 0
    ga_blk, gb_blk = ga_off // bn, gb_off // bn
    return pl.pallas_call(
        _merge_body,
        grid=(m // bm, n // bn),
        in_specs=[
            pl.BlockSpec((bm, ka), lambda i, j: (i, 0)),
            pl.BlockSpec((bm, kb), lambda i, j: (i, 0)),
            pl.BlockSpec((ka, bn), lambda i, j: (0, j)),
            pl.BlockSpec((kb, bn), lambda i, j: (0, j)),
            pl.BlockSpec((bm, bn), lambda i, j: (i, ga_blk + j)),
            pl.BlockSpec((bm, bn), lambda i, j: (i, gb_blk + j)),
        ],
        out_specs=pl.BlockSpec((bm, bn), lambda i, j: (i, j)),
        out_shape=jax.ShapeDtypeStruct((m, n), BF16),
        compiler_params=_params("parallel", "arbitrary"),
        name="merge",
    )(ha, ob, wa, wb, p_big, p_big)


def _rmsnorm_body(x_ref, g_ref, o_ref):
    o_ref[...] = _rms_rows(x_ref[...], g_ref[...])


def rmsnorm_rows(x, gain, *, bm):
    m, d = x.shape
    return pl.pallas_call(
        _rmsnorm_body,
        grid=(m // bm,),
        in_specs=[pl.BlockSpec((bm, d), lambda i: (i, 0)), pl.BlockSpec((1, d), lambda i: (0, 0))],
        out_specs=pl.BlockSpec((bm, d), lambda i: (i, 0)),
        out_shape=jax.ShapeDtypeStruct((m, d), F32),
        compiler_params=_params("parallel"),
        name="rmsnorm_rows",
    )(x, gain.reshape(1, d))


def _mlstm_body(q_ref, k_ref, v_ref, ao_ref, gc_ref, gbias_ref, norm_ref, c0_ref, n0_ref, m0_ref,
                ha_ref, c_ref, n_ref, m_ref, *, heads, dk, dv, chunk):
    L = chunk

    @pl.when(pl.program_id(1) == 0)
    def _():
        c_ref[...] = c0_ref[...]
        n_ref[...] = n0_ref[...]
        m_ref[...] = m0_ref[...]

    row = lax.broadcasted_iota(jnp.int32, (L, L), 0)
    col = lax.broadcasted_iota(jnp.int32, (L, L), 1)
    eye = row == col
    causal = row >= col
    tri = jnp.where(causal, 1.0, 0.0).astype(F32)

    gates = gc_ref[...] + gbias_ref[...]
    log_f = jnp.minimum(gates, 0.0) - jnp.log1p(jnp.exp(-jnp.abs(gates)))
    b_all = jnp.dot(tri, log_f, precision=HIGHEST, preferred_element_type=F32)
    m_row = m_ref[0]
    lane = lax.broadcasted_iota(jnp.int32, m_row.shape, 1)
    m_next = m_row

    for h in range(heads):
        ig_c = gates[:, h:h + 1]
        b_c = b_all[:, heads + h:heads + h + 1]
        ig_r = _row_of(ig_c, eye)
        b_r = _row_of(b_c, eye)
        m_prev = m_row[0:1, h:h + 1]

        log_d = jnp.where(causal, b_c - b_r + ig_r, NEG)
        m_inter = m_prev + b_c
        m_t = jnp.maximum(m_inter, jnp.max(log_d, axis=-1, keepdims=True))
        d_mat = jnp.exp(log_d - m_t)
        inter = jnp.exp(m_inter - m_t)

        q = q_ref[:, h * dk:(h + 1) * dk]
        k = k_ref[:, h * dk:(h + 1) * dk] * (dk ** -0.5)
        v = v_ref[:, h * dv:(h + 1) * dv]
        qb, kb, vb = q.astype(BF16), k.astype(BF16), v.astype(BF16)
        c_old = c_ref[0, h]
        n_old = n_ref[0, h:h + 1, :]

        s = lax.dot_general(qb, kb, NT_DIMS, preferred_element_type=F32) * d_mat
        num = inter * _mm(qb, c_old.astype(BF16)) + _mm(s.astype(BF16), vb)
        den = inter * jnp.sum(q * n_old, axis=-1, keepdims=True) + jnp.sum(s, axis=-1, keepdims=True)
        h_out = num / jnp.maximum(jnp.abs(den), jnp.exp(-m_t))

        m_new = m_t[L - 1:L, :]
        b_last = b_c[L - 1:L, :]
        carry = jnp.exp(m_prev + b_last - m_new)
        kw = k * jnp.exp(b_last - b_c + ig_c - m_new)
        c_ref[0, h] = carry * c_old + lax.dot_general(kw.astype(BF16), vb, TN_DIMS,
                                                      preferred_element_type=F32)
        n_ref[0, h:h + 1, :] = carry * n_old + jnp.sum(kw, axis=0, keepdims=True)
        m_next = jnp.where(lane == h, m_new, m_next)

        hn = _rms_rows(h_out, norm_ref[:, h * dv:(h + 1) * dv])
        ha_ref[:, h * dv:(h + 1) * dv] = (hn * _sigmoid(ao_ref[:, h * dv:(h + 1) * dv])).astype(BF16)

    m_ref[0] = m_next


def mlstm(p_big, p_small, gbias, norm_row, c0, n0, m0, *, row0, seq, chunk, heads, dk, dv,
          q_off, k_off, v_off, ao_off):
    bsz = c0.shape[0]
    nc = seq // chunk
    rb0 = row0 // chunk
    qk_w, v_w = heads * dk, heads * dv
    assert q_off % qk_w == 0 and k_off % qk_w == 0 and v_off % v_w == 0 and ao_off % v_w == 0
    assert row0 % chunk == 0
    rows = lambda b, c: rb0 + b * nc + c
    body = functools.partial(_mlstm_body, heads=heads, dk=dk, dv=dv, chunk=chunk)
    return pl.pallas_call(
        body,
        grid=(bsz, nc),
        in_specs=[
            pl.BlockSpec((chunk, qk_w), lambda b, c: (rows(b, c), q_off // qk_w)),
            pl.BlockSpec((chunk, qk_w), lambda b, c: (rows(b, c), k_off // qk_w)),
            pl.BlockSpec((chunk, v_w), lambda b, c: (rows(b, c), v_off // v_w)),
            pl.BlockSpec((chunk, v_w), lambda b, c: (rows(b, c), ao_off // v_w)),
            pl.BlockSpec((chunk, GATE_LANES), lambda b, c: (rows(b, c), 0)),
            pl.BlockSpec((1, GATE_LANES), lambda b, c: (0, 0)),
            pl.BlockSpec((1, v_w), lambda b, c: (0, 0)),
            pl.BlockSpec((1, heads, dk, dv), lambda b, c: (b, 0, 0, 0)),
            pl.BlockSpec((1, heads, dk), lambda b, c: (b, 0, 0)),
            pl.BlockSpec((1, 8, GATE_LANES), lambda b, c: (b, 0, 0)),
        ],
        out_specs=[
            pl.BlockSpec((chunk, v_w), lambda b, c: (b * nc + c, 0)),
            pl.BlockSpec((1, heads, dk, dv), lambda b, c: (b, 0, 0, 0)),
            pl.BlockSpec((1, heads, dk), lambda b, c: (b, 0, 0)),
            pl.BlockSpec((1, 8, GATE_LANES), lambda b, c: (b, 0, 0)),
        ],
        out_shape=[
            jax.ShapeDtypeStruct((bsz * seq, v_w), BF16),
            jax.ShapeDtypeStruct(c0.shape, F32),
            jax.ShapeDtypeStruct(n0.shape, F32),
            jax.ShapeDtypeStruct(m0.shape, F32),
        ],
        compiler_params=_params("parallel", "arbitrary"),
        name="mlstm",
    )(p_big, p_big, p_big, p_big, p_small, gbias, norm_row, c0, n0, m0)


def _shift_rows(x, prev, j):
    xs = pltpu.roll(x, j, axis=0)
    row = lax.broadcasted_iota(jnp.int32, prev.shape, 0)
    head = jnp.where(row < j, pltpu.roll(prev, j, axis=0), xs[0:8])
    if x.shape[0] == 8:
        return head
    return jnp.concatenate([head, xs[8:]], axis=0)


def _conv_silu(x_ref, w_ref, tail_ref, width):
    x = x_ref[...]
    prev = tail_ref[0]
    acc = x * w_ref[width - 1:width, :]
    for j in range(1, width):
        acc = acc + _shift_rows(x, prev, j) * w_ref[width - 1 - j:width - j, :]
    tail_ref[0] = x[x.shape[0] - 8:, :]
    return acc * _sigmoid(acc)


def _split_heads(a, heads, width):
    return jnp.stack([a[:, i * width:(i + 1) * width] for i in range(heads)])


def _bmm(a, b):
    return jnp.einsum('hij,hjk->hik', a, b, preferred_element_type=F32)


def _bmm_nt(a, b):
    return jnp.einsum('hik,hjk->hij', a, b, preferred_element_type=F32)


def _bmm_tn(a, b):
    return jnp.einsum('hsk,hsv->hkv', a, b, preferred_element_type=F32)


def _gdn_body(xq_ref, xk_ref, xv_ref, z_ref, wq_ref, wk_ref, wv_ref, tq0_ref, tk0_ref, tv0_ref,
              gc_ref, gbias_ref, alog_ref, norm_ref, s0_ref,
              ob_ref, s_ref, tq_ref, tk_ref, tv_ref, *, heads, dk, dv, chunk, width, gl_col, beta_col):
    L = chunk

    @pl.when(pl.program_id(1) == 0)
    def _():
        s_ref[...] = s0_ref[...]
        tq_ref[...] = tq0_ref[...]
        tk_ref[...] = tk0_ref[...]
        tv_ref[...] = tv0_ref[...]

    q = _split_heads(_conv_silu(xq_ref, wq_ref, tq_ref, width), heads, dk)
    k = _split_heads(_conv_silu(xk_ref, wk_ref, tk_ref, width), heads, dk)
    v = _split_heads(_conv_silu(xv_ref, wv_ref, tv_ref, width), heads, dv)
    q = q * lax.rsqrt(jnp.sum(q * q, axis=-1, keepdims=True) + EPS) * (dk ** -0.5)
    k = k * lax.rsqrt(jnp.sum(k * k, axis=-1, keepdims=True) + EPS)

    row = lax.broadcasted_iota(jnp.int32, (L, L), 0)
    col = lax.broadcasted_iota(jnp.int32, (L, L), 1)
    causal = (row >= col)[None]
    strict = (row > col)[None]
    tri = jnp.where(row >= col, 1.0, 0.0).astype(F32)

    raw = gc_ref[...] + gbias_ref[...]
    gl_all = -jnp.exp(alog_ref[...]) * _softplus(raw)
    g_all = jnp.dot(tri, gl_all, precision=HIGHEST, preferred_element_type=F32)
    beta_all = _sigmoid(raw)
    g_rows = g_all.T
    g_c = jnp.stack([g_all[:, gl_col + i:gl_col + i + 1] for i in range(heads)])
    g_r = jnp.stack([g_rows[gl_col + i:gl_col + i + 1, :] for i in range(heads)])
    beta_c = jnp.stack([beta_all[:, beta_col + i:beta_col + i + 1] for i in range(heads)])
    eg_c = jnp.exp(g_c)
    decay = jnp.exp(jnp.where(causal, g_c - g_r, NEG))

    kb = k.astype(BF16)
    both = _bmm_nt(jnp.concatenate([k, q], axis=1).astype(BF16), kb)
    kk, qk = both[:, :L], both[:, L:]

    x_pow = jnp.where(strict, -(beta_c * kk * decay), 0.0)
    n_inv = x_pow
    for _ in range(max(int(math.ceil(math.log2(L))) - 1, 0)):
        xb = x_pow.astype(BF16)
        x_pow = _bmm(xb, xb)
        n_inv = n_inv + x_pow + _bmm(n_inv.astype(BF16), x_pow.astype(BF16))
    rhs = jnp.concatenate([v * beta_c, k * (beta_c * eg_c)], axis=-1)
    sol = rhs + _bmm(n_inv.astype(BF16), rhs.astype(BF16))
    u, w = sol[..., :dv], sol[..., dv:]

    s_old = s_ref[0]
    sb = s_old.astype(BF16)
    ws_qs = _bmm(jnp.concatenate([w, q], axis=1).astype(BF16), sb)
    delta = u - ws_qs[:, :L]
    db = delta.astype(BF16)
    o = eg_c * ws_qs[:, L:] + _bmm((qk * decay).astype(BF16), db)
    g_last = g_c[:, L - 1:L, :]
    kd = k * jnp.exp(g_last - g_c)
    s_ref[0] = jnp.exp(g_last) * s_old + _bmm_tn(kd.astype(BF16), db)

    o = _rms_rows(o, norm_ref[...])
    for i in range(heads):
        z = z_ref[:, i * dv:(i + 1) * dv]
        ob_ref[:, i * dv:(i + 1) * dv] = (o[i] * (z * _sigmoid(z))).astype(BF16)


def gdn(p_big, p_small, gbias, alog_row, norm_row, conv_w, tails0, s0, *, row0, seq, chunk,
        q_off, k_off, v_off, z_off, gl_col, beta_col):
    bsz, heads, dk, dv = s0.shape
    assert dk == dv
    width = conv_w[0].shape[0]
    nc = seq // chunk
    rb0 = row0 // chunk
    hw = heads * dk
    assert all(off % hw == 0 for off in (q_off, k_off, v_off, z_off)) and row0 % chunk == 0
    rows = lambda b, c: rb0 + b * nc + c
    x_spec = lambda off: pl.BlockSpec((chunk, hw), lambda b, c: (rows(b, c), off // hw))
    w_spec = pl.BlockSpec((width, hw), lambda b, c: (0, 0))
    tail_spec = pl.BlockSpec((1, 8, hw), lambda b, c: (b, 0, 0))
    s_spec = pl.BlockSpec((1, heads, dk, dv), lambda b, c: (b, 0, 0, 0))
    vec_spec = lambda n: pl.BlockSpec((1, n), lambda b, c: (0, 0))
    body = functools.partial(_gdn_body, heads=heads, dk=dk, dv=dv, chunk=chunk, width=width,
                             gl_col=gl_col, beta_col=beta_col)
    tail_shape = jax.ShapeDtypeStruct(tails0[0].shape, F32)
    return pl.pallas_call(
        body,
        grid=(bsz, nc),
        in_specs=[
            x_spec(q_off), x_spec(k_off), x_spec(v_off), x_spec(z_off),
            w_spec, w_spec, w_spec, tail_spec, tail_spec, tail_spec,
            pl.BlockSpec((chunk, GATE_LANES), lambda b, c: (rows(b, c), 0)),
            vec_spec(GATE_LANES), vec_spec(GATE_LANES), vec_spec(dv), s_spec,
        ],
        out_specs=[
            pl.BlockSpec((chunk, hw), lambda b, c: (b * nc + c, 0)),
            s_spec, tail_spec, tail_spec, tail_spec,
        ],
        out_shape=[
            jax.ShapeDtypeStruct((bsz * seq, heads * dv), BF16),
            jax.ShapeDtypeStruct(s0.shape, F32),
            tail_shape, tail_shape, tail_shape,
        ],
        compiler_params=_params("parallel", "arbitrary"),
        name="gdn",
    )(p_big, p_big, p_big, p_big, conv_w[0], conv_w[1], conv_w[2], tails0[0], tails0[1], tails0[2],
      p_small, gbias, alog_row, norm_row, s0)


def _pad_lanes(vec, offset):
    return jnp.zeros((1, GATE_LANES), F32).at[0, offset:offset + vec.shape[0]].set(vec.astype(F32))


def _layer(x, streams, lw, dims):
    h_a, dk_a, dv_a, h_b, dk_b, dv_b, width = dims
    a_qk, a_v, b_k, b_v = h_a * dk_a, h_a * dv_a, h_b * dk_b, h_b * dv_b
    m, d = x.shape
    ff = lw["ffn1_gate"].shape[1]
    bm, bf, bd = _block(m, 512), _block(ff, 512), _block(d, 512)

    act = ffn_up(x, lw["norm1"], lw["ffn1_gate"], lw["ffn1_up"], bm=bm, bf=bf)
    x = matmul_residual(act, lw["ffn1_down"], x, scale=0.5, bm=bm, bn=bd)

    offs = dict(q=0, k=a_qk, v=2 * a_qk, ao=2 * a_qk + a_v)
    offs["bq"] = offs["ao"] + a_v
    offs["bk"] = offs["bq"] + b_k
    offs["bv"] = offs["bk"] + b_k
    offs["bz"] = offs["bv"] + b_v
    offs["ga"] = offs["bz"] + b_v
    offs["gb"] = offs["ga"] + d
    p_big, p_small = in_proj(x, lw["norm2"], lw["w_big"], lw["w_small"], bm=bm,
                             bn=_block(lw["w_big"].shape[1], 1024))

    gl_col, beta_col = 2 * h_a, 2 * h_a + h_b
    gbias = (_pad_lanes(lw["mlstm_b_i"], 0) + _pad_lanes(lw["mlstm_b_f"], h_a)
             + _pad_lanes(lw["gdn_dt_bias"], gl_col))
    alog_row = _pad_lanes(lw["gdn_A_log"], gl_col)
    conv_w = [lw["gdn_conv"][:, i * b_k:(i + 1) * b_k] for i in range(3)]

    ha_parts, ob_parts, new_states = [], [], []
    for st in streams:
        bsz = st["bsz"]
        m0 = jnp.zeros((bsz, 8, GATE_LANES), F32).at[:, :, :h_a].set(
            jnp.broadcast_to(st["m"][:, None, :], (bsz, 8, h_a)))
        ha, c1, n1, m1 = mlstm(
            p_big, p_small, gbias, lw["mlstm_norm"].reshape(1, a_v), st["C"], st["n"], m0,
            row0=st["row0"], seq=st["seq"], chunk=st["chunk_a"], heads=h_a, dk=dk_a, dv=dv_a,
            q_off=offs["q"], k_off=offs["k"], v_off=offs["v"], ao_off=offs["ao"])
        tails0 = [jnp.pad(st["conv"][:, :, i * b_k:(i + 1) * b_k], ((0, 0), (8 - (width - 1), 0), (0, 0)))
                  for i in range(3)]
        ob, s1, tq, tk, tv = gdn(
            p_big, p_small, gbias, alog_row, lw["gdn_norm"].reshape(1, dv_b), conv_w, tails0, st["S"],
            row0=st["row0"], seq=st["seq"], chunk=st["chunk_b"],
            q_off=offs["bq"], k_off=offs["bk"], v_off=offs["bv"], z_off=offs["bz"],
            gl_col=gl_col, beta_col=beta_col)
        conv1 = jnp.concatenate([tq, tk, tv], axis=-1)[:, 8 - (width - 1):, :]
        ha_parts.append(ha)
        ob_parts.append(ob)
        new_states.append((conv1, s1, c1, n1, m1[:, 0, :h_a]))

    ha = jnp.concatenate(ha_parts, axis=0)
    ob = jnp.concatenate(ob_parts, axis=0)

    merged = merge(ha, ob, lw["w_branch_a"], lw["w_branch_b"], p_big,
                   ga_off=offs["ga"], gb_off=offs["gb"], bm=bm, bn=bd)
    x = matmul_residual(merged, lw["w_out"], x, scale=1.0, bm=bm, bn=bd)

    act = ffn_up(x, lw["norm3"], lw["ffn2_gate"], lw["ffn2_up"], bm=bm, bf=bf)
    x = matmul_residual(act, lw["ffn2_down"], x, scale=0.5, bm=bm, bn=bd)
    return x, new_states


def _prep_layer_weights(l, w, dims):
    h_a, dk_a, dv_a, h_b, dk_b, dv_b, _ = dims
    a_qk, a_v, b_k, b_v = h_a * dk_a, h_a * dv_a, h_b * dk_b, h_b * dv_b
    d = w["w_in"].shape[1]
    sizes = [a_qk, a_qk, a_v, h_a, h_a, a_v, 2 * b_k + b_v, h_b, h_b, b_v, d, d]
    starts = [0]
    for s in sizes:
        starts.append(starts[-1] + s)
    w_in = w["w_in"][l]
    seg = lambda i: w_in[:, starts[i]:starts[i + 1]]
    w_big = jnp.concatenate([seg(0), seg(1), seg(2), seg(5), seg(6), seg(9), seg(10), seg(11)],
                            axis=1).astype(BF16)
    small = jnp.concatenate([seg(3), seg(4), seg(7), seg(8)], axis=1)
    w_small = jnp.pad(small, ((0, 0), (0, GATE_LANES - small.shape[1])))
    lw = {k: w[k][l] for k in ("norm1", "norm2", "norm3", "mlstm_b_i", "mlstm_b_f", "mlstm_norm",
                                "gdn_conv", "gdn_A_log", "gdn_dt_bias", "gdn_norm")}
    for k in ("ffn1_gate", "ffn1_up", "ffn1_down", "ffn2_gate", "ffn2_up", "ffn2_down",
              "w_branch_a", "w_branch_b", "w_out"):
        lw[k] = w[k][l].astype(BF16)
    lw["w_big"], lw["w_small"] = w_big, w_small
    return lw


def kernel(x_prompt, x_sample, state_conv, state_gdn, state_mlstm_C, state_mlstm_n, state_mlstm_m,
           norm1, ffn1_gate, ffn1_up, ffn1_down, norm2, w_in, mlstm_b_i, mlstm_b_f, mlstm_norm,
           gdn_conv, gdn_A_log, gdn_dt_bias, gdn_norm, w_branch_a, w_branch_b, w_out,
           norm3, ffn2_gate, ffn2_up, ffn2_down, norm_f):
    depth = norm1.shape[0]
    bp, sp, d = x_prompt.shape
    bs, ss, _ = x_sample.shape
    _, _, h_a, dk_a, dv_a = state_mlstm_C.shape
    _, _, h_b, dk_b, dv_b = state_gdn.shape
    width = gdn_conv.shape[1]
    dims = (h_a, dk_a, dv_a, h_b, dk_b, dv_b, width)
    w = dict(norm1=norm1, ffn1_gate=ffn1_gate, ffn1_up=ffn1_up, ffn1_down=ffn1_down, norm2=norm2,
             w_in=w_in, mlstm_b_i=mlstm_b_i, mlstm_b_f=mlstm_b_f, mlstm_norm=mlstm_norm,
             gdn_conv=gdn_conv, gdn_A_log=gdn_A_log, gdn_dt_bias=gdn_dt_bias, gdn_norm=gdn_norm,
             w_branch_a=w_branch_a, w_branch_b=w_branch_b, w_out=w_out, norm3=norm3,
             ffn2_gate=ffn2_gate, ffn2_up=ffn2_up, ffn2_down=ffn2_down)

    x = jnp.concatenate([x_prompt.reshape(bp * sp, d), x_sample.reshape(bs * ss, d)], axis=0)
    zeros = lambda *shape: jnp.zeros(shape, F32)
    prompt = dict(row0=0, bsz=bp, seq=sp, chunk_a=math.gcd(sp, 256), chunk_b=math.gcd(sp, CHUNK))
    sample = dict(row0=bp * sp, bsz=bs, seq=ss, chunk_a=math.gcd(ss, 256), chunk_b=math.gcd(ss, CHUNK))

    per_layer = [[], []]
    for l in range(depth):
        lw = _prep_layer_weights(l, w, dims)
        prompt.update(conv=zeros(bp, width - 1, (2 * dk_b + dv_b) * h_b), S=zeros(bp, h_b, dk_b, dv_b),
                      C=zeros(bp, h_a, dk_a, dv_a), n=zeros(bp, h_a, dk_a), m=zeros(bp, h_a))
        sample.update(conv=state_conv[l], S=state_gdn[l], C=state_mlstm_C[l], n=state_mlstm_n[l],
                      m=state_mlstm_m[l])
        x, states = _layer(x, [prompt, sample], lw, dims)
        per_layer[0].append(states[0])
        per_layer[1].append(states[1])

    y = rmsnorm_rows(x, norm_f, bm=_block(x.shape[0], 512))
    y_prompt = y[:bp * sp].reshape(bp, sp, d)
    y_sample = y[bp * sp:].reshape(bs, ss, d)
    stack = lambda sts: tuple(jnp.stack(z) for z in zip(*sts))
    return (y_prompt, y_sample) + stack(per_layer[0]) + stack(per_layer[1])
```

```python
import functools
import math

import jax
import jax.numpy as jnp
from jax import lax
from jax.experimental import pallas as pl
from jax.experimental.pallas import tpu as pltpu

F32 = jnp.float32
BF16 = jnp.bfloat16
HIGHEST = lax.Precision.HIGHEST
EPS = 1e-6
NEG = -1e30
CHUNK = 64
GATE_LANES = 128
V7X_VMEM_BYTES = 64 * 1024 * 1024
VMEM_LIMIT = V7X_VMEM_BYTES - 8 * 1024 * 1024

NT_DIMS = (((1,), (1,)), ((), ()))
TN_DIMS = (((0,), (0,)), ((), ()))


def _block(n, preferred):
    return math.gcd(n, preferred)


def _params(*semantics):
    return pltpu.CompilerParams(dimension_semantics=semantics, vmem_limit_bytes=VMEM_LIMIT)


def _sigmoid(x):
    return 1.0 / (1.0 + jnp.exp(-x))


def _softplus(x):
    return jnp.maximum(x, 0.0) + jnp.log1p(jnp.exp(-jnp.abs(x)))


def _rms_rows(x, gain):
    return x * lax.rsqrt(jnp.mean(x * x, axis=-1, keepdims=True) + EPS) * gain


def _row_of(col, eye):
    return jnp.sum(jnp.where(eye, col, 0.0), axis=0, keepdims=True)


def _mm(a, b):
    return jnp.dot(a, b, preferred_element_type=F32)


def _part_starts(parts, bm):
    starts, total = [], 0
    for p in parts:
        assert p.shape[0] % bm == 0
        starts.append(total)
        total += p.shape[0] // bm
    return tuple(starts), total


def _part_specs(parts, bm, cols, col_of, **spec_kwargs):
    starts, _ = _part_starts(parts, bm)
    specs = []
    for p, s in zip(parts, starts):
        nb = p.shape[0] // bm
        specs.append(pl.BlockSpec(
            (bm, cols), lambda i, *js, s=s, nb=nb: (jnp.clip(i - s, 0, nb - 1), col_of(*js)),
            **spec_kwargs))
    return specs


def _read_parts(refs, starts):
    val = refs[0][...]
    for r, s in zip(refs[1:], starts[1:]):
        val = jnp.where(pl.program_id(0) >= s, r[...], val)
    return val


def _ffn_up_body(*refs, starts):
    x_refs = refs[:len(starts)]
    g_ref, wg_ref, wu_ref, act_ref, h_ref = refs[len(starts):]

    @pl.when(pl.program_id(1) == 0)
    def _():
        h_ref[...] = _rms_rows(_read_parts(x_refs, starts), g_ref[...]).astype(BF16)

    h = h_ref[...]
    gate = _mm(h, wg_ref[...])
    up = _mm(h, wu_ref[...])
    act_ref[...] = (gate * _sigmoid(gate) * up).astype(BF16)


def ffn_up(x_parts, gain, wg, wu, *, bm, bf):
    d, ff = wg.shape
    starts, nrow = _part_starts(x_parts, bm)
    return pl.pallas_call(
        functools.partial(_ffn_up_body, starts=starts),
        grid=(nrow, ff // bf),
        in_specs=_part_specs(x_parts, bm, d, lambda j: 0, pipeline_mode=pl.Buffered(1)) + [
            pl.BlockSpec((1, d), lambda i, j: (0, 0)),
            pl.BlockSpec((d, bf), lambda i, j: (0, j)),
            pl.BlockSpec((d, bf), lambda i, j: (0, j)),
        ],
        out_specs=pl.BlockSpec((bm, bf), lambda i, j: (i, j)),
        out_shape=jax.ShapeDtypeStruct((nrow * bm, ff), BF16),
        scratch_shapes=[pltpu.VMEM((bm, d), BF16)],
        compiler_params=_params("parallel", "arbitrary"),
        name="ffn_up",
    )(*x_parts, gain.reshape(1, d), wg, wu)


def _mm_res_body(a_ref, w_ref, *refs, scale, starts):
    x_refs, o_ref = refs[:-1], refs[-1]
    o_ref[...] = _read_parts(x_refs, starts) + scale * _mm(a_ref[...], w_ref[...])


def matmul_residual(a, w, x_parts, *, scale, bm, bn):
    m, k = a.shape
    n = w.shape[1]
    starts, nrow = _part_starts(x_parts, bm)
    assert nrow * bm == m
    return pl.pallas_call(
        functools.partial(_mm_res_body, scale=scale, starts=starts),
        grid=(nrow, n // bn),
        in_specs=[
            pl.BlockSpec((bm, k), lambda i, j: (i, 0)),
            pl.BlockSpec((k, bn), lambda i, j: (0, j)),
        ] + _part_specs(x_parts, bm, bn, lambda j: j),
        out_specs=pl.BlockSpec((bm, bn), lambda i, j: (i, j)),
        out_shape=jax.ShapeDtypeStruct((m, n), F32),
        compiler_params=_params("parallel", "arbitrary"),
        name="matmul_residual",
    )(a, w, *x_parts)


def _in_proj_body(x_ref, g_ref, w_ref, ws_ref, p_ref, ps_ref, h_ref):
    @pl.when(pl.program_id(1) == 0)
    def _():
        h = _rms_rows(x_ref[...], g_ref[...])
        h_ref[...] = h.astype(BF16)
        ps_ref[...] = jnp.dot(h, ws_ref[...], precision=HIGHEST, preferred_element_type=F32)

    p_ref[...] = _mm(h_ref[...], w_ref[...])


def in_proj(x, gain, w_big, w_small, *, bm, bn):
    m, d = x.shape
    n = w_big.shape[1]
    return pl.pallas_call(
        _in_proj_body,
        grid=(m // bm, n // bn),
        in_specs=[
            pl.BlockSpec((bm, d), lambda i, j: (i, 0)),
            pl.BlockSpec((1, d), lambda i, j: (0, 0)),
            pl.BlockSpec((d, bn), lambda i, j: (0, j)),
            pl.BlockSpec((d, GATE_LANES), lambda i, j: (0, 0)),
        ],
        out_specs=[
            pl.BlockSpec((bm, bn), lambda i, j: (i, j)),
            pl.BlockSpec((bm, GATE_LANES), lambda i, j: (i, 0)),
        ],
        out_shape=[
            jax.ShapeDtypeStruct((m, n), F32),
            jax.ShapeDtypeStruct((m, GATE_LANES), F32),
        ],
        scratch_shapes=[pltpu.VMEM((bm, d), BF16)],
        compiler_params=_params("parallel", "arbitrary"),
        name="in_proj",
    )(x, gain.reshape(1, d), w_big, w_small)


def _merge_body(*refs, starts):
    n = len(starts)
    ha_refs, ob_refs = refs[:n], refs[n:2 * n]
    wa_ref, wb_ref, ga_ref, gb_ref, o_ref = refs[2 * n:]
    ya = _mm(_read_parts(ha_refs, starts), wa_ref[...])
    yb = _mm(_read_parts(ob_refs, starts), wb_ref[...])
    o_ref[...] = (_sigmoid(ga_ref[...]) * ya + _sigmoid(gb_ref[...]) * yb).astype(BF16)


def merge(ha_parts, ob_parts, wa, wb, p_big, *, ga_off, gb_off, bm, bn):
    ka, n = wa.shape
    kb = wb.shape[0]
    assert ga_off % bn == 0 and gb_off % bn == 0
    ga_blk, gb_blk = ga_off // bn, gb_off // bn
    starts, nrow = _part_starts(ha_parts, bm)
    assert _part_starts(ob_parts, bm) == (starts, nrow) and nrow * bm == p_big.shape[0]
    return pl.pallas_call(
        functools.partial(_merge_body, starts=starts),
        grid=(nrow, n // bn),
        in_specs=_part_specs(ha_parts, bm, ka, lambda j: 0) + _part_specs(ob_parts, bm, kb, lambda j: 0) + [
            pl.BlockSpec((ka, bn), lambda i, j: (0, j)),
            pl.BlockSpec((kb, bn), lambda i, j: (0, j)),
            pl.BlockSpec((bm, bn), lambda i, j: (i, ga_blk + j)),
            pl.BlockSpec((bm, bn), lambda i, j: (i, gb_blk + j)),
        ],
        out_specs=pl.BlockSpec((bm, bn), lambda i, j: (i, j)),
        out_shape=jax.ShapeDtypeStruct((nrow * bm, n), BF16),
        compiler_params=_params("parallel", "arbitrary"),
        name="merge",
    )(*ha_parts, *ob_parts, wa, wb, p_big, p_big)


def _rmsnorm_body(x_ref, g_ref, *o_refs, starts, nrow):
    y = _rms_rows(x_ref[...], g_ref[...])
    i = pl.program_id(0)
    ends = starts[1:] + (nrow,)
    for o_ref, lo, hi in zip(o_refs, starts, ends):
        @pl.when((i >= lo) & (i < hi))
        def _(o_ref=o_ref):
            o_ref[...] = y


def rmsnorm_rows(x, gain, part_rows, *, bm):
    m, d = x.shape
    parts = [jax.ShapeDtypeStruct((r, d), F32) for r in part_rows]
    starts, nrow = _part_starts(parts, bm)
    assert nrow * bm == m
    return pl.pallas_call(
        functools.partial(_rmsnorm_body, starts=starts, nrow=nrow),
        grid=(nrow,),
        in_specs=[pl.BlockSpec((bm, d), lambda i: (i, 0)), pl.BlockSpec((1, d), lambda i: (0, 0))],
        out_specs=_part_specs(parts, bm, d, lambda: 0),
        out_shape=parts,
        compiler_params=_params("arbitrary"),
        name="rmsnorm_rows",
    )(x, gain.reshape(1, d))


def _mlstm_body(q_ref, k_ref, v_ref, ao_ref, gc_ref, gbias_ref, norm_ref, c0_ref, n0_ref, m0_ref,
                ha_ref, c_ref, n_ref, m_ref, *, heads, dk, dv, chunk, bb):
    L = chunk

    @pl.when(pl.program_id(1) == 0)
    def _():
        c_ref[...] = c0_ref[...]
        n_ref[...] = n0_ref[...]
        m_ref[...] = m0_ref[...]

    row = lax.broadcasted_iota(jnp.int32, (L, L), 0)
    col = lax.broadcasted_iota(jnp.int32, (L, L), 1)
    eye = row == col
    causal = row >= col
    tri = jnp.where(causal, 1.0, 0.0).astype(F32)

    lane = lax.broadcasted_iota(jnp.int32, (8, GATE_LANES), 1)

    for bi in range(bb):
        r0 = bi * L
        gates = gc_ref[r0:r0 + L, :] + gbias_ref[...]
        log_f = jnp.minimum(gates, 0.0) - jnp.log1p(jnp.exp(-jnp.abs(gates)))
        b_all = jnp.dot(tri, log_f, precision=HIGHEST, preferred_element_type=F32)
        m_row = m_ref[bi]
        m_next = m_row

        for h in range(heads):
            ig_c = gates[:, h:h + 1]
            b_c = b_all[:, heads + h:heads + h + 1]
            ig_r = _row_of(ig_c, eye)
            b_r = _row_of(b_c, eye)
            m_prev = m_row[0:1, h:h + 1]

            log_d = jnp.where(causal, b_c - b_r + ig_r, NEG)
            m_inter = m_prev + b_c
            m_t = jnp.maximum(m_inter, jnp.max(log_d, axis=-1, keepdims=True))
            d_mat = jnp.exp(log_d - m_t)
            inter = jnp.exp(m_inter - m_t)

            q = q_ref[r0:r0 + L, h * dk:(h + 1) * dk]
            k = k_ref[r0:r0 + L, h * dk:(h + 1) * dk] * (dk ** -0.5)
            v = v_ref[r0:r0 + L, h * dv:(h + 1) * dv]
            qb, kb, vb = q.astype(BF16), k.astype(BF16), v.astype(BF16)
            c_old = c_ref[bi, h]
            n_old = n_ref[bi, h:h + 1, :]

            s = lax.dot_general(qb, kb, NT_DIMS, preferred_element_type=F32) * d_mat
            num = inter * _mm(qb, c_old.astype(BF16)) + _mm(s.astype(BF16), vb)
            den = (inter * jnp.sum(q * n_old, axis=-1, keepdims=True)
                   + jnp.sum(s, axis=-1, keepdims=True))
            h_out = num / jnp.maximum(jnp.abs(den), jnp.exp(-m_t))

            m_new = m_t[L - 1:L, :]
            b_last = b_c[L - 1:L, :]
            carry = jnp.exp(m_prev + b_last - m_new)
            kw = k * jnp.exp(b_last - b_c + ig_c - m_new)
            c_ref[bi, h] = carry * c_old + lax.dot_general(kw.astype(BF16), vb, TN_DIMS,
                                                           preferred_element_type=F32)
            n_ref[bi, h:h + 1, :] = carry * n_old + jnp.sum(kw, axis=0, keepdims=True)
            m_next = jnp.where(lane == h, m_new, m_next)

            hn = _rms_rows(h_out, norm_ref[:, h * dv:(h + 1) * dv])
            gate_o = _sigmoid(ao_ref[r0:r0 + L, h * dv:(h + 1) * dv])
            ha_ref[r0:r0 + L, h * dv:(h + 1) * dv] = (hn * gate_o).astype(BF16)

        m_ref[bi] = m_next


def mlstm(p_big, p_small, gbias, norm_row, c0, n0, m0, *, row0, seq, chunk, heads, dk, dv,
          q_off, k_off, v_off, ao_off):
    bsz = c0.shape[0]
    nc = seq // chunk
    bb = 2 if (nc == 1 and bsz % 2 == 0) else 1
    blk = bb * chunk
    qk_w, v_w = heads * dk, heads * dv
    assert q_off % qk_w == 0 and k_off % qk_w == 0 and v_off % v_w == 0 and ao_off % v_w == 0
    assert row0 % blk == 0
    rows = lambda b, c: row0 // blk + b * nc + c
    body = functools.partial(_mlstm_body, heads=heads, dk=dk, dv=dv, chunk=chunk, bb=bb)
    return pl.pallas_call(
        body,
        grid=(bsz // bb, nc),
        in_specs=[
            pl.BlockSpec((blk, qk_w), lambda b, c: (rows(b, c), q_off // qk_w)),
            pl.BlockSpec((blk, qk_w), lambda b, c: (rows(b, c), k_off // qk_w)),
            pl.BlockSpec((blk, v_w), lambda b, c: (rows(b, c), v_off // v_w)),
            pl.BlockSpec((blk, v_w), lambda b, c: (rows(b, c), ao_off // v_w)),
            pl.BlockSpec((blk, GATE_LANES), lambda b, c: (rows(b, c), 0)),
            pl.BlockSpec((1, GATE_LANES), lambda b, c: (0, 0)),
            pl.BlockSpec((1, v_w), lambda b, c: (0, 0)),
            pl.BlockSpec((bb, heads, dk, dv), lambda b, c: (b, 0, 0, 0)),
            pl.BlockSpec((bb, heads, dk), lambda b, c: (b, 0, 0)),
            pl.BlockSpec((bb, 8, GATE_LANES), lambda b, c: (b, 0, 0)),
        ],
        out_specs=[
            pl.BlockSpec((blk, v_w), lambda b, c: (b * nc + c, 0)),
            pl.BlockSpec((bb, heads, dk, dv), lambda b, c: (b, 0, 0, 0)),
            pl.BlockSpec((bb, heads, dk), lambda b, c: (b, 0, 0)),
            pl.BlockSpec((bb, 8, GATE_LANES), lambda b, c: (b, 0, 0)),
        ],
        out_shape=[
            jax.ShapeDtypeStruct((bsz * seq, v_w), BF16),
            jax.ShapeDtypeStruct(c0.shape, F32),
            jax.ShapeDtypeStruct(n0.shape, F32),
            jax.ShapeDtypeStruct(m0.shape, F32),
        ],
        compiler_params=_params("parallel", "arbitrary"),
        name="mlstm",
    )(p_big, p_big, p_big, p_big, p_small, gbias, norm_row, c0, n0, m0)


def _shift_rows(x, prev, j):
    xs = pltpu.roll(x, j, axis=0)
    row = lax.broadcasted_iota(jnp.int32, prev.shape, 0)
    head = jnp.where(row < j, pltpu.roll(prev, j, axis=0), xs[0:8])
    if x.shape[0] == 8:
        return head
    return jnp.concatenate([head, xs[8:]], axis=0)


def _conv_silu(x_ref, w_ref, tail_ref, width):
    outs = []
    for bi in range(tail_ref.shape[0]):
        L = x_ref.shape[0] // tail_ref.shape[0]
        x = x_ref[bi * L:(bi + 1) * L, :]
        prev = tail_ref[bi]
        acc = x * w_ref[width - 1:width, :]
        for j in range(1, width):
            acc = acc + _shift_rows(x, prev, j) * w_ref[width - 1 - j:width - j, :]
        tail_ref[bi] = x[L - 8:, :]
        outs.append(acc * _sigmoid(acc))
    return outs


def _split_heads(seqs, heads, width):
    return jnp.stack([a[:, i * width:(i + 1) * width] for a in seqs for i in range(heads)])


def _bmm(a, b):
    return jnp.einsum('hij,hjk->hik', a, b, preferred_element_type=F32)


def _bmm_nt(a, b):
    return jnp.einsum('hik,hjk->hij', a, b, preferred_element_type=F32)


def _bmm_tn(a, b):
    return jnp.einsum('hsk,hsv->hkv', a, b, preferred_element_type=F32)


def _gdn_body(xq_ref, xk_ref, xv_ref, z_ref, wq_ref, wk_ref, wv_ref, tq0_ref, tk0_ref, tv0_ref,
              gc_ref, gbias_ref, alog_ref, norm_ref, s0_ref,
              ob_ref, s_ref, tq_ref, tk_ref, tv_ref, *, heads, dk, dv, chunk, width, gl_col, beta_col):
    L = chunk
    bb = s_ref.shape[0]

    @pl.when(pl.program_id(1) == 0)
    def _():
        s_ref[...] = s0_ref[...]
        tq_ref[...] = tq0_ref[...]
        tk_ref[...] = tk0_ref[...]
        tv_ref[...] = tv0_ref[...]

    q = _split_heads(_conv_silu(xq_ref, wq_ref, tq_ref, width), heads, dk)
    k = _split_heads(_conv_silu(xk_ref, wk_ref, tk_ref, width), heads, dk)
    v = _split_heads(_conv_silu(xv_ref, wv_ref, tv_ref, width), heads, dv)
    q = q * lax.rsqrt(jnp.sum(q * q, axis=-1, keepdims=True) + EPS) * (dk ** -0.5)
    k = k * lax.rsqrt(jnp.sum(k * k, axis=-1, keepdims=True) + EPS)

    row = lax.broadcasted_iota(jnp.int32, (L, L), 0)
    col = lax.broadcasted_iota(jnp.int32, (L, L), 1)
    causal = (row >= col)[None]
    strict = (row > col)[None]
    tri = jnp.where(row >= col, 1.0, 0.0).astype(F32)

    g_cols, g_rows, beta_cols = [], [], []
    for bi in range(bb):
        raw = gc_ref[bi * L:(bi + 1) * L, :] + gbias_ref[...]
        gl_all = -jnp.exp(alog_ref[...]) * _softplus(raw)
        g_all = jnp.dot(tri, gl_all, precision=HIGHEST, preferred_element_type=F32)
        g_all_t = g_all.T
        beta_all = _sigmoid(raw)
        g_cols += [g_all[:, gl_col + i:gl_col + i + 1] for i in range(heads)]
        g_rows += [g_all_t[gl_col + i:gl_col + i + 1, :] for i in range(heads)]
        beta_cols += [beta_all[:, beta_col + i:beta_col + i + 1] for i in range(heads)]
    g_c, g_r, beta_c = jnp.stack(g_cols), jnp.stack(g_rows), jnp.stack(beta_cols)
    eg_c = jnp.exp(g_c)
    decay = jnp.exp(jnp.where(causal, g_c - g_r, NEG))

    kb = k.astype(BF16)
    both = _bmm_nt(jnp.concatenate([k, q], axis=1).astype(BF16), kb)
    kk, qk = both[:, :L], both[:, L:]

    x_pow = jnp.where(strict, -(beta_c * kk * decay), 0.0)
    n_inv = x_pow
    for _ in range(max(int(math.ceil(math.log2(L))) - 1, 0)):
        xb = x_pow.astype(BF16)
        x_pow = _bmm(xb, xb)
        n_inv = n_inv + x_pow + _bmm(n_inv.astype(BF16), x_pow.astype(BF16))
    rhs = jnp.concatenate([v * beta_c, k * (beta_c * eg_c)], axis=-1)
    sol = rhs + _bmm(n_inv.astype(BF16), rhs.astype(BF16))
    u, w = sol[..., :dv], sol[..., dv:]

    s_old = s_ref[...].reshape(bb * heads, dk, dv)
    sb = s_old.astype(BF16)
    ws_qs = _bmm(jnp.concatenate([w, q], axis=1).astype(BF16), sb)
    delta = u - ws_qs[:, :L]
    db = delta.astype(BF16)
    o = eg_c * ws_qs[:, L:] + _bmm((qk * decay).astype(BF16), db)
    g_last = g_c[:, L - 1:L, :]
    kd = k * jnp.exp(g_last - g_c)
    s_new = jnp.exp(g_last) * s_old + _bmm_tn(kd.astype(BF16), db)
    s_ref[...] = s_new.reshape(bb, heads, dk, dv)

    o = _rms_rows(o, norm_ref[...])
    for bi in range(bb):
        for i in range(heads):
            z = z_ref[bi * L:(bi + 1) * L, i * dv:(i + 1) * dv]
            ob_ref[bi * L:(bi + 1) * L, i * dv:(i + 1) * dv] = (
                o[bi * heads + i] * (z * _sigmoid(z))).astype(BF16)


def gdn(p_big, p_small, gbias, alog_row, norm_row, conv_w, tails0, s0, *, row0, seq, chunk,
        q_off, k_off, v_off, z_off, gl_col, beta_col):
    bsz, heads, dk, dv = s0.shape
    assert dk == dv
    width = conv_w[0].shape[0]
    nc = seq // chunk
    bb = 2 if (nc == 1 and bsz % 2 == 0) else 1
    blk = bb * chunk
    hw = heads * dk
    assert all(off % hw == 0 for off in (q_off, k_off, v_off, z_off)) and row0 % blk == 0
    rows = lambda b, c: row0 // blk + b * nc + c
    x_spec = lambda off: pl.BlockSpec((blk, hw), lambda b, c: (rows(b, c), off // hw))
    w_spec = pl.BlockSpec((width, hw), lambda b, c: (0, 0))
    tail_spec = pl.BlockSpec((bb, 8, hw), lambda b, c: (b, 0, 0))
    s_spec = pl.BlockSpec((bb, heads, dk, dv), lambda b, c: (b, 0, 0, 0))
    vec_spec = lambda n: pl.BlockSpec((1, n), lambda b, c: (0, 0))
    body = functools.partial(_gdn_body, heads=heads, dk=dk, dv=dv, chunk=chunk, width=width,
                             gl_col=gl_col, beta_col=beta_col)
    tail_shape = jax.ShapeDtypeStruct(tails0[0].shape, F32)
    return pl.pallas_call(
        body,
        grid=(bsz // bb, nc),
        in_specs=[
            x_spec(q_off), x_spec(k_off), x_spec(v_off), x_spec(z_off),
            w_spec, w_spec, w_spec, tail_spec, tail_spec, tail_spec,
            pl.BlockSpec((blk, GATE_LANES), lambda b, c: (rows(b, c), 0)),
            vec_spec(GATE_LANES), vec_spec(GATE_LANES), vec_spec(dv), s_spec,
        ],
        out_specs=[
            pl.BlockSpec((blk, hw), lambda b, c: (b * nc + c, 0)),
            s_spec, tail_spec, tail_spec, tail_spec,
        ],
        out_shape=[
            jax.ShapeDtypeStruct((bsz * seq, heads * dv), BF16),
            jax.ShapeDtypeStruct(s0.shape, F32),
            tail_shape, tail_shape, tail_shape,
        ],
        compiler_params=_params("parallel", "arbitrary"),
        name="gdn",
    )(p_big, p_big, p_big, p_big, conv_w[0], conv_w[1], conv_w[2], tails0[0], tails0[1], tails0[2],
      p_small, gbias, alog_row, norm_row, s0)


def _pad_lanes(vec, offset):
    return jnp.zeros((1, GATE_LANES), F32).at[0, offset:offset + vec.shape[0]].set(vec.astype(F32))


def _layer(x_parts, streams, lw, dims):
    h_a, dk_a, dv_a, h_b, dk_b, dv_b, width = dims
    a_qk, a_v, b_k, b_v = h_a * dk_a, h_a * dv_a, h_b * dk_b, h_b * dv_b
    d = x_parts[0].shape[1]
    ff = lw["ffn1_gate"].shape[1]
    bm = 1024
    for p in x_parts:
        bm = _block(p.shape[0], bm)
    bf, bd = _block(ff, 512), _block(d, 512)

    act = ffn_up(x_parts, lw["norm1"], lw["ffn1_gate"], lw["ffn1_up"], bm=bm, bf=bf)
    x = matmul_residual(act, lw["ffn1_down"], x_parts, scale=0.5, bm=bm, bn=bd)

    offs = dict(q=0, k=a_qk, v=2 * a_qk, ao=2 * a_qk + a_v)
    offs["bq"] = offs["ao"] + a_v
    offs["bk"] = offs["bq"] + b_k
    offs["bv"] = offs["bk"] + b_k
    offs["bz"] = offs["bv"] + b_v
    offs["ga"] = offs["bz"] + b_v
    offs["gb"] = offs["ga"] + d
    p_big, p_small = in_proj(x, lw["norm2"], lw["w_big"], lw["w_small"], bm=bm,
                             bn=_block(lw["w_big"].shape[1], 1024))

    gl_col, beta_col = 2 * h_a, 2 * h_a + h_b
    gbias = (_pad_lanes(lw["mlstm_b_i"], 0) + _pad_lanes(lw["mlstm_b_f"], h_a)
             + _pad_lanes(lw["gdn_dt_bias"], gl_col))
    alog_row = _pad_lanes(lw["gdn_A_log"], gl_col)
    conv_w = [lw["gdn_conv"][:, i * b_k:(i + 1) * b_k] for i in range(3)]

    ha_parts, ob_parts, new_states = [], [], []
    for st in streams:
        bsz = st["bsz"]
        m0 = jnp.zeros((bsz, 8, GATE_LANES), F32).at[:, :, :h_a].set(
            jnp.broadcast_to(st["m"][:, None, :], (bsz, 8, h_a)))
        ha, c1, n1, m1 = mlstm(
            p_big, p_small, gbias, lw["mlstm_norm"].reshape(1, a_v), st["C"], st["n"], m0,
            row0=st["row0"], seq=st["seq"], chunk=st["chunk_a"], heads=h_a, dk=dk_a, dv=dv_a,
            q_off=offs["q"], k_off=offs["k"], v_off=offs["v"], ao_off=offs["ao"])
        tails0 = [jnp.pad(st["conv"][:, :, i * b_k:(i + 1) * b_k], ((0, 0), (8 - (width - 1), 0), (0, 0)))
                  for i in range(3)]
        ob, s1, tq, tk, tv = gdn(
            p_big, p_small, gbias, alog_row, lw["gdn_norm"].reshape(1, dv_b), conv_w, tails0, st["S"],
            row0=st["row0"], seq=st["seq"], chunk=st["chunk_b"],
            q_off=offs["bq"], k_off=offs["bk"], v_off=offs["bv"], z_off=offs["bz"],
            gl_col=gl_col, beta_col=beta_col)
        conv1 = jnp.concatenate([tq, tk, tv], axis=-1)[:, 8 - (width - 1):, :]
        ha_parts.append(ha)
        ob_parts.append(ob)
        new_states.append((conv1, s1, c1, n1, m1[:, 0, :h_a]))

    merged = merge(ha_parts, ob_parts, lw["w_branch_a"], lw["w_branch_b"], p_big,
                   ga_off=offs["ga"], gb_off=offs["gb"], bm=bm, bn=bd)
    x = matmul_residual(merged, lw["w_out"], [x], scale=1.0, bm=bm, bn=_block(d, 1024))

    act = ffn_up([x], lw["norm3"], lw["ffn2_gate"], lw["ffn2_up"], bm=bm, bf=bf)
    x = matmul_residual(act, lw["ffn2_down"], [x], scale=0.5, bm=bm, bn=bd)
    return x, new_states


def _split_w_in_body(w_ref, big_ref, small_ref, *, big_segs, small_segs):
    for src, width, dst in big_segs:
        big_ref[:, dst:dst + width] = w_ref[:, src:src + width].astype(BF16)
    small_ref[...] = jnp.zeros(small_ref.shape, F32)
    for src, width, dst in small_segs:
        small_ref[:, dst:dst + width] = w_ref[:, src:src + width]


def split_w_in(w_in, dims):
    h_a, dk_a, dv_a, h_b, dk_b, dv_b, _ = dims
    a_qk, a_v, b_k, b_v = h_a * dk_a, h_a * dv_a, h_b * dk_b, h_b * dv_b
    d, d_in = w_in.shape
    runs = [(2 * a_qk + a_v, True), (2 * h_a, False), (a_v + 2 * b_k + b_v, True), (2 * h_b, False),
            (b_v + 2 * d, True)]
    assert sum(r for r, _ in runs) == d_in
    big_segs, small_segs, src, big_dst, small_dst = [], [], 0, 0, 0
    for width, is_big in runs:
        if is_big:
            big_segs.append((src, width, big_dst))
            big_dst += width
        else:
            small_segs.append((src, width, small_dst))
            small_dst += width
        src += width
    assert big_dst % GATE_LANES == 0 and small_dst <= GATE_LANES
    rows = _block(d, 128)
    return pl.pallas_call(
        functools.partial(_split_w_in_body, big_segs=tuple(big_segs), small_segs=tuple(small_segs)),
        grid=(d // rows,),
        in_specs=[pl.BlockSpec((rows, d_in), lambda i: (i, 0))],
        out_specs=[pl.BlockSpec((rows, big_dst), lambda i: (i, 0)),
                   pl.BlockSpec((rows, GATE_LANES), lambda i: (i, 0))],
        out_shape=[jax.ShapeDtypeStruct((d, big_dst), BF16), jax.ShapeDtypeStruct((d, GATE_LANES), F32)],
        compiler_params=_params("parallel"),
        name="split_w_in",
    )(w_in)


def _prep_layer_weights(l, w, dims):
    lw = {k: w[k][l] for k in ("norm1", "norm2", "norm3", "mlstm_b_i", "mlstm_b_f", "mlstm_norm",
                                "gdn_conv", "gdn_A_log", "gdn_dt_bias", "gdn_norm")}
    for k in ("ffn1_gate", "ffn1_up", "ffn1_down", "ffn2_gate", "ffn2_up", "ffn2_down",
              "w_branch_a", "w_branch_b", "w_out"):
        lw[k] = w[k][l].astype(BF16)
    lw["w_big"], lw["w_small"] = split_w_in(w["w_in"][l], dims)
    return lw


def kernel(x_prompt, x_sample, state_conv, state_gdn, state_mlstm_C, state_mlstm_n, state_mlstm_m,
           norm1, ffn1_gate, ffn1_up, ffn1_down, norm2, w_in, mlstm_b_i, mlstm_b_f, mlstm_norm,
           gdn_conv, gdn_A_log, gdn_dt_bias, gdn_norm, w_branch_a, w_branch_b, w_out,
           norm3, ffn2_gate, ffn2_up, ffn2_down, norm_f):
    depth = norm1.shape[0]
    bp, sp, d = x_prompt.shape
    bs, ss, _ = x_sample.shape
    _, _, h_a, dk_a, dv_a = state_mlstm_C.shape
    _, _, h_b, dk_b, dv_b = state_gdn.shape
    width = gdn_conv.shape[1]
    dims = (h_a, dk_a, dv_a, h_b, dk_b, dv_b, width)
    w = dict(norm1=norm1, ffn1_gate=ffn1_gate, ffn1_up=ffn1_up, ffn1_down=ffn1_down, norm2=norm2,
             w_in=w_in, mlstm_b_i=mlstm_b_i, mlstm_b_f=mlstm_b_f, mlstm_norm=mlstm_norm,
             gdn_conv=gdn_conv, gdn_A_log=gdn_A_log, gdn_dt_bias=gdn_dt_bias, gdn_norm=gdn_norm,
             w_branch_a=w_branch_a, w_branch_b=w_branch_b, w_out=w_out, norm3=norm3,
             ffn2_gate=ffn2_gate, ffn2_up=ffn2_up, ffn2_down=ffn2_down)

    x_parts = [x_prompt.reshape(bp * sp, d), x_sample.reshape(bs * ss, d)]
    zeros = lambda *shape: jnp.zeros(shape, F32)
    prompt = dict(row0=0, bsz=bp, seq=sp, chunk_a=math.gcd(sp, 256), chunk_b=math.gcd(sp, CHUNK))
    sample = dict(row0=bp * sp, bsz=bs, seq=ss, chunk_a=math.gcd(ss, 256), chunk_b=math.gcd(ss, CHUNK))

    per_layer = [[], []]
    for l in range(depth):
        lw = _prep_layer_weights(l, w, dims)
        prompt.update(conv=zeros(bp, width - 1, (2 * dk_b + dv_b) * h_b), S=zeros(bp, h_b, dk_b, dv_b),
                      C=zeros(bp, h_a, dk_a, dv_a), n=zeros(bp, h_a, dk_a), m=zeros(bp, h_a))
        sample.update(conv=state_conv[l], S=state_gdn[l], C=state_mlstm_C[l], n=state_mlstm_n[l],
                      m=state_mlstm_m[l])
        x, states = _layer(x_parts, [prompt, sample], lw, dims)
        x_parts = [x]
        per_layer[0].append(states[0])
        per_layer[1].append(states[1])

    part_rows = (bp * sp, bs * ss)
    y_prompt, y_sample = rmsnorm_rows(x, norm_f, part_rows,
                                      bm=_block(math.gcd(*part_rows), 512))
    stack = lambda sts: tuple(jnp.stack(z) for z in zip(*sts))
    return ((y_prompt.reshape(bp, sp, d), y_sample.reshape(bs, ss, d))
            + stack(per_layer[0]) + stack(per_layer[1]))
```

```python
import functools
import math

import jax
import jax.numpy as jnp
from jax import lax
from jax.experimental import pallas as pl
from jax.experimental.pallas import tpu as pltpu

F32 = jnp.float32
BF16 = jnp.bfloat16
HIGHEST = lax.Precision.HIGHEST
EPS = 1e-6
NEG = -1e30
CHUNK = 64
GATE_LANES = 128
V7X_VMEM_BYTES = 64 * 1024 * 1024
VMEM_LIMIT = V7X_VMEM_BYTES - 8 * 1024 * 1024

NT_DIMS = (((1,), (1,)), ((), ()))
TN_DIMS = (((0,), (0,)), ((), ()))


def _block(n, preferred):
    return math.gcd(n, preferred)


def _params(*semantics):
    return pltpu.CompilerParams(dimension_semantics=semantics, vmem_limit_bytes=VMEM_LIMIT)


def _sigmoid(x):
    return 1.0 / (1.0 + jnp.exp(-x))


def _softplus(x):
    return jnp.maximum(x, 0.0) + jnp.log1p(jnp.exp(-jnp.abs(x)))


def _rms_rows(x, gain):
    return x * lax.rsqrt(jnp.mean(x * x, axis=-1, keepdims=True) + EPS) * gain


def _row_of(col, eye):
    return jnp.sum(jnp.where(eye, col, 0.0), axis=0, keepdims=True)


def _mm(a, b):
    return jnp.dot(a, b, preferred_element_type=F32)


def _part_starts(parts, bm):
    starts, total = [], 0
    for p in parts:
        assert p.shape[0] % bm == 0
        starts.append(total)
        total += p.shape[0] // bm
    return tuple(starts), total


def _part_specs(parts, bm, cols, col_of, **spec_kwargs):
    starts, _ = _part_starts(parts, bm)
    specs = []
    for p, s in zip(parts, starts):
        nb = p.shape[0] // bm
        specs.append(pl.BlockSpec(
            (bm, cols), lambda i, *js, s=s, nb=nb: (jnp.clip(i - s, 0, nb - 1), col_of(*js)),
            **spec_kwargs))
    return specs


def _read_parts(refs, starts):
    val = refs[0][...]
    for r, s in zip(refs[1:], starts[1:]):
        val = jnp.where(pl.program_id(0) >= s, r[...], val)
    return val


def _ffn_up_body(*refs, starts):
    x_refs = refs[:len(starts)]
    g_ref, wg_ref, wu_ref, act_ref, h_ref = refs[len(starts):]

    @pl.when(pl.program_id(1) == 0)
    def _():
        h_ref[...] = _rms_rows(_read_parts(x_refs, starts), g_ref[...]).astype(BF16)

    h = h_ref[...]
    gate = _mm(h, wg_ref[...].astype(BF16))
    up = _mm(h, wu_ref[...].astype(BF16))
    act_ref[...] = (gate * _sigmoid(gate) * up).astype(BF16)


def ffn_up(x_parts, gain, wg, wu, *, bm, bf):
    d, ff = wg.shape
    starts, nrow = _part_starts(x_parts, bm)
    return pl.pallas_call(
        functools.partial(_ffn_up_body, starts=starts),
        grid=(nrow, ff // bf),
        in_specs=_part_specs(x_parts, bm, d, lambda j: 0, pipeline_mode=pl.Buffered(1)) + [
            pl.BlockSpec((1, d), lambda i, j: (0, 0)),
            pl.BlockSpec((d, bf), lambda i, j: (0, j)),
            pl.BlockSpec((d, bf), lambda i, j: (0, j)),
        ],
        out_specs=pl.BlockSpec((bm, bf), lambda i, j: (i, j)),
        out_shape=jax.ShapeDtypeStruct((nrow * bm, ff), BF16),
        scratch_shapes=[pltpu.VMEM((bm, d), BF16)],
        compiler_params=_params("parallel", "arbitrary"),
        name="ffn_up",
    )(*x_parts, gain.reshape(1, d), wg, wu)


def _mm_res_body(a_ref, w_ref, *refs, scale, starts):
    x_refs, o_ref = refs[:-1], refs[-1]
    o_ref[...] = _read_parts(x_refs, starts) + scale * _mm(a_ref[...], w_ref[...].astype(BF16))


def matmul_residual(a, w, x_parts, *, scale, bm, bn):
    m, k = a.shape
    n = w.shape[1]
    starts, nrow = _part_starts(x_parts, bm)
    assert nrow * bm == m
    return pl.pallas_call(
        functools.partial(_mm_res_body, scale=scale, starts=starts),
        grid=(nrow, n // bn),
        in_specs=[
            pl.BlockSpec((bm, k), lambda i, j: (i, 0)),
            pl.BlockSpec((k, bn), lambda i, j: (0, j)),
        ] + _part_specs(x_parts, bm, bn, lambda j: j),
        out_specs=pl.BlockSpec((bm, bn), lambda i, j: (i, j)),
        out_shape=jax.ShapeDtypeStruct((m, n), F32),
        compiler_params=_params("parallel", "arbitrary"),
        name="matmul_residual",
    )(a, w, *x_parts)


def _in_proj_body(x_ref, g_ref, w_ref, ws_ref, p_ref, ps_ref, h_ref):
    @pl.when(pl.program_id(1) == 0)
    def _():
        h = _rms_rows(x_ref[...], g_ref[...])
        h_ref[...] = h.astype(BF16)
        ps_ref[...] = jnp.dot(h, ws_ref[...], precision=HIGHEST, preferred_element_type=F32)

    p_ref[...] = _mm(h_ref[...], w_ref[...])


def in_proj(x, gain, w_big, w_small, *, bm, bn):
    m, d = x.shape
    n = w_big.shape[1]
    return pl.pallas_call(
        _in_proj_body,
        grid=(m // bm, n // bn),
        in_specs=[
            pl.BlockSpec((bm, d), lambda i, j: (i, 0)),
            pl.BlockSpec((1, d), lambda i, j: (0, 0)),
            pl.BlockSpec((d, bn), lambda i, j: (0, j)),
            pl.BlockSpec((d, GATE_LANES), lambda i, j: (0, 0)),
        ],
        out_specs=[
            pl.BlockSpec((bm, bn), lambda i, j: (i, j)),
            pl.BlockSpec((bm, GATE_LANES), lambda i, j: (i, 0)),
        ],
        out_shape=[
            jax.ShapeDtypeStruct((m, n), F32),
            jax.ShapeDtypeStruct((m, GATE_LANES), F32),
        ],
        scratch_shapes=[pltpu.VMEM((bm, d), BF16)],
        compiler_params=_params("parallel", "arbitrary"),
        name="in_proj",
    )(x, gain.reshape(1, d), w_big, w_small)


def _merge_body(*refs, starts):
    n = len(starts)
    ha_refs, ob_refs = refs[:n], refs[n:2 * n]
    wa_ref, wb_ref, ga_ref, gb_ref, o_ref = refs[2 * n:]
    ya = _mm(_read_parts(ha_refs, starts), wa_ref[...].astype(BF16))
    yb = _mm(_read_parts(ob_refs, starts), wb_ref[...].astype(BF16))
    o_ref[...] = (_sigmoid(ga_ref[...]) * ya + _sigmoid(gb_ref[...]) * yb).astype(BF16)


def merge(ha_parts, ob_parts, wa, wb, p_big, *, ga_off, gb_off, bm, bn):
    ka, n = wa.shape
    kb = wb.shape[0]
    assert ga_off % bn == 0 and gb_off % bn == 0
    ga_blk, gb_blk = ga_off // bn, gb_off // bn
    starts, nrow = _part_starts(ha_parts, bm)
    assert _part_starts(ob_parts, bm) == (starts, nrow) and nrow * bm == p_big.shape[0]
    return pl.pallas_call(
        functools.partial(_merge_body, starts=starts),
        grid=(nrow, n // bn),
        in_specs=_part_specs(ha_parts, bm, ka, lambda j: 0, pipeline_mode=pl.Buffered(1))
        + _part_specs(ob_parts, bm, kb, lambda j: 0, pipeline_mode=pl.Buffered(1)) + [
            pl.BlockSpec((ka, bn), lambda i, j: (0, j)),
            pl.BlockSpec((kb, bn), lambda i, j: (0, j)),
            pl.BlockSpec((bm, bn), lambda i, j: (i, ga_blk + j)),
            pl.BlockSpec((bm, bn), lambda i, j: (i, gb_blk + j)),
        ],
        out_specs=pl.BlockSpec((bm, bn), lambda i, j: (i, j)),
        out_shape=jax.ShapeDtypeStruct((nrow * bm, n), BF16),
        compiler_params=_params("parallel", "arbitrary"),
        name="merge",
    )(*ha_parts, *ob_parts, wa, wb, p_big, p_big)


def _rmsnorm_body(x_ref, g_ref, *o_refs, starts, nrow):
    y = _rms_rows(x_ref[...], g_ref[...])
    i = pl.program_id(0)
    ends = starts[1:] + (nrow,)
    for o_ref, lo, hi in zip(o_refs, starts, ends):
        @pl.when((i >= lo) & (i < hi))
        def _(o_ref=o_ref):
            o_ref[...] = y


def rmsnorm_rows(x, gain, part_rows, *, bm):
    m, d = x.shape
    parts = [jax.ShapeDtypeStruct((r, d), F32) for r in part_rows]
    starts, nrow = _part_starts(parts, bm)
    assert nrow * bm == m
    return pl.pallas_call(
        functools.partial(_rmsnorm_body, starts=starts, nrow=nrow),
        grid=(nrow,),
        in_specs=[pl.BlockSpec((bm, d), lambda i: (i, 0)), pl.BlockSpec((1, d), lambda i: (0, 0))],
        out_specs=_part_specs(parts, bm, d, lambda: 0),
        out_shape=parts,
        compiler_params=_params("arbitrary"),
        name="rmsnorm_rows",
    )(x, gain.reshape(1, d))


def _mlstm_body(q_ref, k_ref, v_ref, ao_ref, gc_ref, gbias_ref, norm_ref, c0_ref, n0_ref, m0_ref,
                ha_ref, c_ref, n_ref, m_ref, *, heads, dk, dv, chunk, bb):
    L = chunk

    @pl.when(pl.program_id(1) == 0)
    def _():
        c_ref[...] = c0_ref[...]
        n_ref[...] = n0_ref[...]
        m_ref[...] = m0_ref[...]

    row = lax.broadcasted_iota(jnp.int32, (L, L), 0)
    col = lax.broadcasted_iota(jnp.int32, (L, L), 1)
    eye = row == col
    causal = row >= col
    tri = jnp.where(causal, 1.0, 0.0).astype(F32)

    lane = lax.broadcasted_iota(jnp.int32, (8, GATE_LANES), 1)

    for bi in range(bb):
        r0 = bi * L
        gates = gc_ref[r0:r0 + L, :] + gbias_ref[...]
        log_f = jnp.minimum(gates, 0.0) - jnp.log1p(jnp.exp(-jnp.abs(gates)))
        b_all = jnp.dot(tri, log_f, precision=HIGHEST, preferred_element_type=F32)
        m_row = m_ref[bi]
        m_next = m_row

        for h in range(heads):
            ig_c = gates[:, h:h + 1]
            b_c = b_all[:, heads + h:heads + h + 1]
            ig_r = _row_of(ig_c, eye)
            b_r = _row_of(b_c, eye)
            m_prev = m_row[0:1, h:h + 1]

            log_d = jnp.where(causal, b_c - b_r + ig_r, NEG)
            m_inter = m_prev + b_c
            m_t = jnp.maximum(m_inter, jnp.max(log_d, axis=-1, keepdims=True))
            d_mat = jnp.exp(log_d - m_t)
            inter = jnp.exp(m_inter - m_t)

            q = q_ref[r0:r0 + L, h * dk:(h + 1) * dk]
            k = k_ref[r0:r0 + L, h * dk:(h + 1) * dk] * (dk ** -0.5)
            v = v_ref[r0:r0 + L, h * dv:(h + 1) * dv]
            qb, kb, vb = q.astype(BF16), k.astype(BF16), v.astype(BF16)
            c_old = c_ref[bi, h]
            n_old = n_ref[bi, h:h + 1, :]

            s = lax.dot_general(qb, kb, NT_DIMS, preferred_element_type=F32) * d_mat
            num = inter * _mm(qb, c_old.astype(BF16)) + _mm(s.astype(BF16), vb)
            den = (inter * jnp.sum(q * n_old, axis=-1, keepdims=True)
                   + jnp.sum(s, axis=-1, keepdims=True))
            h_out = num / jnp.maximum(jnp.abs(den), jnp.exp(-m_t))

            m_new = m_t[L - 1:L, :]
            b_last = b_c[L - 1:L, :]
            carry = jnp.exp(m_prev + b_last - m_new)
            kw = k * jnp.exp(b_last - b_c + ig_c - m_new)
            c_ref[bi, h] = carry * c_old + lax.dot_general(kw.astype(BF16), vb, TN_DIMS,
                                                           preferred_element_type=F32)
            n_ref[bi, h:h + 1, :] = carry * n_old + jnp.sum(kw, axis=0, keepdims=True)
            m_next = jnp.where(lane == h, m_new, m_next)

            hn = _rms_rows(h_out, norm_ref[:, h * dv:(h + 1) * dv])
            gate_o = _sigmoid(ao_ref[r0:r0 + L, h * dv:(h + 1) * dv])
            ha_ref[r0:r0 + L, h * dv:(h + 1) * dv] = (hn * gate_o).astype(BF16)

        m_ref[bi] = m_next


def mlstm(p_big, p_small, gbias, norm_row, c0, n0, m0, *, row0, seq, chunk, heads, dk, dv,
          q_off, k_off, v_off, ao_off):
    bsz = c0.shape[0]
    nc = seq // chunk
    bb = 2 if (nc == 1 and bsz % 2 == 0) else 1
    blk = bb * chunk
    qk_w, v_w = heads * dk, heads * dv
    assert q_off % qk_w == 0 and k_off % qk_w == 0 and v_off % v_w == 0 and ao_off % v_w == 0
    assert row0 % blk == 0
    rows = lambda b, c: row0 // blk + b * nc + c
    body = functools.partial(_mlstm_body, heads=heads, dk=dk, dv=dv, chunk=chunk, bb=bb)
    return pl.pallas_call(
        body,
        grid=(bsz // bb, nc),
        in_specs=[
            pl.BlockSpec((blk, qk_w), lambda b, c: (rows(b, c), q_off // qk_w)),
            pl.BlockSpec((blk, qk_w), lambda b, c: (rows(b, c), k_off // qk_w)),
            pl.BlockSpec((blk, v_w), lambda b, c: (rows(b, c), v_off // v_w)),
            pl.BlockSpec((blk, v_w), lambda b, c: (rows(b, c), ao_off // v_w)),
            pl.BlockSpec((blk, GATE_LANES), lambda b, c: (rows(b, c), 0)),
            pl.BlockSpec((1, GATE_LANES), lambda b, c: (0, 0)),
            pl.BlockSpec((1, v_w), lambda b, c: (0, 0)),
            pl.BlockSpec((bb, heads, dk, dv), lambda b, c: (b, 0, 0, 0)),
            pl.BlockSpec((bb, heads, dk), lambda b, c: (b, 0, 0)),
            pl.BlockSpec((bb, 8, GATE_LANES), lambda b, c: (b, 0, 0)),
        ],
        out_specs=[
            pl.BlockSpec((blk, v_w), lambda b, c: (b * nc + c, 0)),
            pl.BlockSpec((bb, heads, dk, dv), lambda b, c: (b, 0, 0, 0)),
            pl.BlockSpec((bb, heads, dk), lambda b, c: (b, 0, 0)),
            pl.BlockSpec((bb, 8, GATE_LANES), lambda b, c: (b, 0, 0)),
        ],
        out_shape=[
            jax.ShapeDtypeStruct((bsz * seq, v_w), BF16),
            jax.ShapeDtypeStruct(c0.shape, F32),
            jax.ShapeDtypeStruct(n0.shape, F32),
            jax.ShapeDtypeStruct(m0.shape, F32),
        ],
        compiler_params=_params("parallel", "arbitrary"),
        name="mlstm",
    )(p_big, p_big, p_big, p_big, p_small, gbias, norm_row, c0, n0, m0)


def _shift_rows(x, prev, j):
    xs = pltpu.roll(x, j, axis=0)
    row = lax.broadcasted_iota(jnp.int32, prev.shape, 0)
    head = jnp.where(row < j, pltpu.roll(prev, j, axis=0), xs[0:8])
    if x.shape[0] == 8:
        return head
    return jnp.concatenate([head, xs[8:]], axis=0)


def _conv_silu(x_ref, w_ref, tail_ref, width):
    outs = []
    for bi in range(tail_ref.shape[0]):
        L = x_ref.shape[0] // tail_ref.shape[0]
        x = x_ref[bi * L:(bi + 1) * L, :]
        prev = tail_ref[bi]
        acc = x * w_ref[width - 1:width, :]
        for j in range(1, width):
            acc = acc + _shift_rows(x, prev, j) * w_ref[width - 1 - j:width - j, :]
        tail_ref[bi] = x[L - 8:, :]
        outs.append(acc * _sigmoid(acc))
    return outs


def _split_heads(seqs, heads, width):
    return jnp.stack([a[:, i * width:(i + 1) * width] for a in seqs for i in range(heads)])


def _bmm(a, b):
    return jnp.einsum('hij,hjk->hik', a, b, preferred_element_type=F32)


def _bmm_nt(a, b):
    return jnp.einsum('hik,hjk->hij', a, b, preferred_element_type=F32)


def _bmm_tn(a, b):
    return jnp.einsum('hsk,hsv->hkv', a, b, preferred_element_type=F32)


def _gdn_body(xq_ref, xk_ref, xv_ref, z_ref, wq_ref, wk_ref, wv_ref, tq0_ref, tk0_ref, tv0_ref,
              gc_ref, gbias_ref, alog_ref, norm_ref, s0_ref,
              ob_ref, s_ref, tq_ref, tk_ref, tv_ref, *, heads, dk, dv, chunk, width, gl_col, beta_col):
    L = chunk
    bb = s_ref.shape[0]

    @pl.when(pl.program_id(1) == 0)
    def _():
        s_ref[...] = s0_ref[...]
        tq_ref[...] = tq0_ref[...]
        tk_ref[...] = tk0_ref[...]
        tv_ref[...] = tv0_ref[...]

    q = _split_heads(_conv_silu(xq_ref, wq_ref, tq_ref, width), heads, dk)
    k = _split_heads(_conv_silu(xk_ref, wk_ref, tk_ref, width), heads, dk)
    v = _split_heads(_conv_silu(xv_ref, wv_ref, tv_ref, width), heads, dv)
    q = q * lax.rsqrt(jnp.sum(q * q, axis=-1, keepdims=True) + EPS) * (dk ** -0.5)
    k = k * lax.rsqrt(jnp.sum(k * k, axis=-1, keepdims=True) + EPS)

    row = lax.broadcasted_iota(jnp.int32, (L, L), 0)
    col = lax.broadcasted_iota(jnp.int32, (L, L), 1)
    causal = (row >= col)[None]
    strict = (row > col)[None]
    tri = jnp.where(row >= col, 1.0, 0.0).astype(F32)

    g_cols, g_rows, beta_cols = [], [], []
    for bi in range(bb):
        raw = gc_ref[bi * L:(bi + 1) * L, :] + gbias_ref[...]
        gl_all = -jnp.exp(alog_ref[...]) * _softplus(raw)
        g_all = jnp.dot(tri, gl_all, precision=HIGHEST, preferred_element_type=F32)
        g_all_t = g_all.T
        beta_all = _sigmoid(raw)
        g_cols += [g_all[:, gl_col + i:gl_col + i + 1] for i in range(heads)]
        g_rows += [g_all_t[gl_col + i:gl_col + i + 1, :] for i in range(heads)]
        beta_cols += [beta_all[:, beta_col + i:beta_col + i + 1] for i in range(heads)]
    g_c, g_r, beta_c = jnp.stack(g_cols), jnp.stack(g_rows), jnp.stack(beta_cols)
    eg_c = jnp.exp(g_c)
    decay = jnp.exp(jnp.where(causal, g_c - g_r, NEG))

    kb = k.astype(BF16)
    both = _bmm_nt(jnp.concatenate([k, q], axis=1).astype(BF16), kb)
    kk, qk = both[:, :L], both[:, L:]

    x_pow = jnp.where(strict, -(beta_c * kk * decay), 0.0)
    n_inv = x_pow
    for _ in range(max(int(math.ceil(math.log2(L))) - 1, 0)):
        xb = x_pow.astype(BF16)
        x_pow = _bmm(xb, xb)
        n_inv = n_inv + x_pow + _bmm(n_inv.astype(BF16), x_pow.astype(BF16))
    rhs = jnp.concatenate([v * beta_c, k * (beta_c * eg_c)], axis=-1)
    sol = rhs + _bmm(n_inv.astype(BF16), rhs.astype(BF16))
    u, w = sol[..., :dv], sol[..., dv:]

    s_old = s_ref[...].reshape(bb * heads, dk, dv)
    sb = s_old.astype(BF16)
    ws_qs = _bmm(jnp.concatenate([w, q], axis=1).astype(BF16), sb)
    delta = u - ws_qs[:, :L]
    db = delta.astype(BF16)
    o = eg_c * ws_qs[:, L:] + _bmm((qk * decay).astype(BF16), db)
    g_last = g_c[:, L - 1:L, :]
    kd = k * jnp.exp(g_last - g_c)
    s_new = jnp.exp(g_last) * s_old + _bmm_tn(kd.astype(BF16), db)
    s_ref[...] = s_new.reshape(bb, heads, dk, dv)

    o = _rms_rows(o, norm_ref[...])
    for bi in range(bb):
        for i in range(heads):
            z = z_ref[bi * L:(bi + 1) * L, i * dv:(i + 1) * dv]
            ob_ref[bi * L:(bi + 1) * L, i * dv:(i + 1) * dv] = (
                o[bi * heads + i] * (z * _sigmoid(z))).astype(BF16)


def gdn(p_big, p_small, gbias, alog_row, norm_row, conv_w, tails0, s0, *, row0, seq, chunk,
        q_off, k_off, v_off, z_off, gl_col, beta_col):
    bsz, heads, dk, dv = s0.shape
    assert dk == dv
    width = conv_w[0].shape[0]
    nc = seq // chunk
    bb = 2 if (nc == 1 and bsz % 2 == 0) else 1
    blk = bb * chunk
    hw = heads * dk
    assert all(off % hw == 0 for off in (q_off, k_off, v_off, z_off)) and row0 % blk == 0
    rows = lambda b, c: row0 // blk + b * nc + c
    x_spec = lambda off: pl.BlockSpec((blk, hw), lambda b, c: (rows(b, c), off // hw))
    w_spec = pl.BlockSpec((width, hw), lambda b, c: (0, 0))
    tail_spec = pl.BlockSpec((bb, 8, hw), lambda b, c: (b, 0, 0))
    s_spec = pl.BlockSpec((bb, heads, dk, dv), lambda b, c: (b, 0, 0, 0))
    vec_spec = lambda n: pl.BlockSpec((1, n), lambda b, c: (0, 0))
    body = functools.partial(_gdn_body, heads=heads, dk=dk, dv=dv, chunk=chunk, width=width,
                             gl_col=gl_col, beta_col=beta_col)
    tail_shape = jax.ShapeDtypeStruct(tails0[0].shape, F32)
    return pl.pallas_call(
        body,
        grid=(bsz // bb, nc),
        in_specs=[
            x_spec(q_off), x_spec(k_off), x_spec(v_off), x_spec(z_off),
            w_spec, w_spec, w_spec, tail_spec, tail_spec, tail_spec,
            pl.BlockSpec((blk, GATE_LANES), lambda b, c: (rows(b, c), 0)),
            vec_spec(GATE_LANES), vec_spec(GATE_LANES), vec_spec(dv), s_spec,
        ],
        out_specs=[
            pl.BlockSpec((blk, hw), lambda b, c: (b * nc + c, 0)),
            s_spec, tail_spec, tail_spec, tail_spec,
        ],
        out_shape=[
            jax.ShapeDtypeStruct((bsz * seq, heads * dv), BF16),
            jax.ShapeDtypeStruct(s0.shape, F32),
            tail_shape, tail_shape, tail_shape,
        ],
        compiler_params=_params("parallel", "arbitrary"),
        name="gdn",
    )(p_big, p_big, p_big, p_big, conv_w[0], conv_w[1], conv_w[2], tails0[0], tails0[1], tails0[2],
      p_small, gbias, alog_row, norm_row, s0)


def _pad_lanes(vec, offset):
    return jnp.zeros((1, GATE_LANES), F32).at[0, offset:offset + vec.shape[0]].set(vec.astype(F32))


def _layer(x_parts, streams, lw, dims):
    h_a, dk_a, dv_a, h_b, dk_b, dv_b, width = dims
    a_qk, a_v, b_k, b_v = h_a * dk_a, h_a * dv_a, h_b * dk_b, h_b * dv_b
    d = x_parts[0].shape[1]
    ff = lw["ffn1_gate"].shape[1]
    bm = 1024
    for p in x_parts:
        bm = _block(p.shape[0], bm)
    bf, bd, bd_down, bd_out = _block(ff, 512), _block(d, 512), _block(d, 256), _block(d, 1024)

    act = ffn_up(x_parts, lw["norm1"], lw["ffn1_gate"], lw["ffn1_up"], bm=bm, bf=bf)
    x = matmul_residual(act, lw["ffn1_down"], x_parts, scale=0.5, bm=bm, bn=bd_down)

    offs = dict(q=0, k=a_qk, v=2 * a_qk, ao=2 * a_qk + a_v)
    offs["bq"] = offs["ao"] + a_v
    offs["bk"] = offs["bq"] + b_k
    offs["bv"] = offs["bk"] + b_k
    offs["bz"] = offs["bv"] + b_v
    offs["ga"] = offs["bz"] + b_v
    offs["gb"] = offs["ga"] + d
    p_big, p_small = in_proj(x, lw["norm2"], lw["w_big"], lw["w_small"], bm=bm,
                             bn=_block(lw["w_big"].shape[1], 1024))

    gl_col, beta_col = 2 * h_a, 2 * h_a + h_b
    gbias = (_pad_lanes(lw["mlstm_b_i"], 0) + _pad_lanes(lw["mlstm_b_f"], h_a)
             + _pad_lanes(lw["gdn_dt_bias"], gl_col))
    alog_row = _pad_lanes(lw["gdn_A_log"], gl_col)
    conv_w = [lw["gdn_conv"][:, i * b_k:(i + 1) * b_k] for i in range(3)]

    ha_parts, ob_parts, new_states = [], [], []
    for st in streams:
        bsz = st["bsz"]
        m0 = jnp.zeros((bsz, 8, GATE_LANES), F32).at[:, :, :h_a].set(
            jnp.broadcast_to(st["m"][:, None, :], (bsz, 8, h_a)))
        ha, c1, n1, m1 = mlstm(
            p_big, p_small, gbias, lw["mlstm_norm"].reshape(1, a_v), st["C"], st["n"], m0,
            row0=st["row0"], seq=st["seq"], chunk=st["chunk_a"], heads=h_a, dk=dk_a, dv=dv_a,
            q_off=offs["q"], k_off=offs["k"], v_off=offs["v"], ao_off=offs["ao"])
        tails0 = [jnp.pad(st["conv"][:, :, i * b_k:(i + 1) * b_k], ((0, 0), (8 - (width - 1), 0), (0, 0)))
                  for i in range(3)]
        ob, s1, tq, tk, tv = gdn(
            p_big, p_small, gbias, alog_row, lw["gdn_norm"].reshape(1, dv_b), conv_w, tails0, st["S"],
            row0=st["row0"], seq=st["seq"], chunk=st["chunk_b"],
            q_off=offs["bq"], k_off=offs["bk"], v_off=offs["bv"], z_off=offs["bz"],
            gl_col=gl_col, beta_col=beta_col)
        conv1 = jnp.concatenate([tq, tk, tv], axis=-1)[:, 8 - (width - 1):, :]
        ha_parts.append(ha)
        ob_parts.append(ob)
        new_states.append((conv1, s1, c1, n1, m1[:, 0, :h_a]))

    merged = merge(ha_parts, ob_parts, lw["w_branch_a"], lw["w_branch_b"], p_big,
                   ga_off=offs["ga"], gb_off=offs["gb"], bm=bm, bn=bd)
    x = matmul_residual(merged, lw["w_out"], [x], scale=1.0, bm=bm, bn=bd_out)

    act = ffn_up([x], lw["norm3"], lw["ffn2_gate"], lw["ffn2_up"], bm=bm, bf=bf)
    x = matmul_residual(act, lw["ffn2_down"], [x], scale=0.5, bm=bm, bn=bd_down)
    return x, new_states


def _split_w_in_body(wt_ref, big_ref):
    big_ref[...] = wt_ref[...].T.astype(BF16)


def split_w_in(w_in, dims):
    h_a, dk_a, dv_a, h_b, dk_b, dv_b, _ = dims
    a_qk, a_v, b_k, b_v = h_a * dk_a, h_a * dv_a, h_b * dk_b, h_b * dv_b
    d, d_in = w_in.shape
    runs = [(2 * a_qk + a_v, True), (2 * h_a, False), (a_v + 2 * b_k + b_v, True), (2 * h_b, False),
            (b_v + 2 * d, True)]
    assert sum(r for r, _ in runs) == d_in
    big_segs, small_segs, src, big_dst = [], [], 0, 0
    for width, is_big in runs:
        if is_big:
            big_segs.append((src, width, big_dst))
            big_dst += width
        else:
            small_segs.append((src, width))
        src += width
    rb = 1024
    for s, width, _ in big_segs:
        assert s % 8 == 0
        rb = _block(width, rb)
    w_t = jnp.swapaxes(w_in, 0, 1)

    def src_row(i):
        row = i * rb
        for s, _, dst in big_segs[1:]:
            row = jnp.where(i * rb >= dst, i * rb + (s - dst), row)
        return pl.multiple_of(row, 8)

    w_big = pl.pallas_call(
        _split_w_in_body,
        grid=(big_dst // rb,),
        in_specs=[pl.BlockSpec((pl.Element(rb), pl.Element(d)), lambda i: (src_row(i), 0))],
        out_specs=pl.BlockSpec((d, rb), lambda i: (0, i)),
        out_shape=jax.ShapeDtypeStruct((d, big_dst), BF16),
        compiler_params=_params("parallel"),
        name="split_w_in",
    )(w_t)
    small = jnp.concatenate([w_t[s:s + width] for s, width in small_segs], axis=0)
    assert small.shape[0] <= GATE_LANES
    w_small = jnp.pad(small, ((0, GATE_LANES - small.shape[0]), (0, 0))).T
    return w_big, w_small


def _prep_layer_weights(l, w, dims):
    lw = {k: w[k][l] for k in w if k != "w_in"}
    lw["w_big"], lw["w_small"] = split_w_in(w["w_in"][l], dims)
    return lw


def kernel(x_prompt, x_sample, state_conv, state_gdn, state_mlstm_C, state_mlstm_n, state_mlstm_m,
           norm1, ffn1_gate, ffn1_up, ffn1_down, norm2, w_in, mlstm_b_i, mlstm_b_f, mlstm_norm,
           gdn_conv, gdn_A_log, gdn_dt_bias, gdn_norm, w_branch_a, w_branch_b, w_out,
           norm3, ffn2_gate, ffn2_up, ffn2_down, norm_f):
    depth = norm1.shape[0]
    bp, sp, d = x_prompt.shape
    bs, ss, _ = x_sample.shape
    _, _, h_a, dk_a, dv_a = state_mlstm_C.shape
    _, _, h_b, dk_b, dv_b = state_gdn.shape
    width = gdn_conv.shape[1]
    dims = (h_a, dk_a, dv_a, h_b, dk_b, dv_b, width)
    w = dict(norm1=norm1, ffn1_gate=ffn1_gate, ffn1_up=ffn1_up, ffn1_down=ffn1_down, norm2=norm2,
             w_in=w_in, mlstm_b_i=mlstm_b_i, mlstm_b_f=mlstm_b_f, mlstm_norm=mlstm_norm,
             gdn_conv=gdn_conv, gdn_A_log=gdn_A_log, gdn_dt_bias=gdn_dt_bias, gdn_norm=gdn_norm,
             w_branch_a=w_branch_a, w_branch_b=w_branch_b, w_out=w_out, norm3=norm3,
             ffn2_gate=ffn2_gate, ffn2_up=ffn2_up, ffn2_down=ffn2_down)

    x_parts = [x_prompt.reshape(bp * sp, d), x_sample.reshape(bs * ss, d)]
    zeros = lambda *shape: jnp.zeros(shape, F32)
    prompt = dict(row0=0, bsz=bp, seq=sp, chunk_a=math.gcd(sp, 256), chunk_b=math.gcd(sp, CHUNK))
    sample = dict(row0=bp * sp, bsz=bs, seq=ss, chunk_a=math.gcd(ss, 256), chunk_b=math.gcd(ss, CHUNK))

    per_layer = [[], []]
    for l in range(depth):
        lw = _prep_layer_weights(l, w, dims)
        prompt.update(conv=zeros(bp, width - 1, (2 * dk_b + dv_b) * h_b), S=zeros(bp, h_b, dk_b, dv_b),
                      C=zeros(bp, h_a, dk_a, dv_a), n=zeros(bp, h_a, dk_a), m=zeros(bp, h_a))
        sample.update(conv=state_conv[l], S=state_gdn[l], C=state_mlstm_C[l], n=state_mlstm_n[l],
                      m=state_mlstm_m[l])
        x, states = _layer(x_parts, [prompt, sample], lw, dims)
        x_parts = [x]
        per_layer[0].append(states[0])
        per_layer[1].append(states[1])

    part_rows = (bp * sp, bs * ss)
    y_prompt, y_sample = rmsnorm_rows(x, norm_f, part_rows,
                                      bm=_block(math.gcd(*part_rows), 512))
    stack = lambda sts: tuple(jnp.stack(z) for z in zip(*sts))
    return ((y_prompt.reshape(bp, sp, d), y_sample.reshape(bs, ss, d))
            + stack(per_layer[0]) + stack(per_layer[1]))
```

```python
import functools
import math

import jax
import jax.numpy as jnp
from jax import lax
from jax.experimental import pallas as pl
from jax.experimental.pallas import tpu as pltpu

F32 = jnp.float32
BF16 = jnp.bfloat16
HIGHEST = lax.Precision.HIGHEST
EPS = 1e-6
NEG = -1e30
CHUNK = 64
GATE_LANES = 128
V7X_VMEM_BYTES = 64 * 1024 * 1024
VMEM_LIMIT = V7X_VMEM_BYTES - 8 * 1024 * 1024

NT_DIMS = (((1,), (1,)), ((), ()))
TN_DIMS = (((0,), (0,)), ((), ()))


def _block(n, preferred):
    return math.gcd(n, preferred)


def _params(*semantics):
    return pltpu.CompilerParams(dimension_semantics=semantics, vmem_limit_bytes=VMEM_LIMIT)


def _sigmoid(x):
    return 1.0 / (1.0 + jnp.exp(-x))


def _softplus(x):
    return jnp.maximum(x, 0.0) + jnp.log1p(jnp.exp(-jnp.abs(x)))


def _rms_rows(x, gain):
    return x * lax.rsqrt(jnp.mean(x * x, axis=-1, keepdims=True) + EPS) * gain


def _row_of(col, eye):
    return jnp.sum(jnp.where(eye, col, 0.0), axis=0, keepdims=True)


def _mm(a, b):
    return jnp.dot(a, b, preferred_element_type=F32)


def _part_starts(parts, bm):
    starts, total = [], 0
    for p in parts:
        assert p.shape[0] % bm == 0
        starts.append(total)
        total += p.shape[0] // bm
    return tuple(starts), total


def _part_specs(parts, bm, cols, col_of, **spec_kwargs):
    starts, _ = _part_starts(parts, bm)
    specs = []
    for p, s in zip(parts, starts):
        nb = p.shape[0] // bm
        specs.append(pl.BlockSpec(
            (bm, cols), lambda i, *js, s=s, nb=nb: (jnp.clip(i - s, 0, nb - 1), col_of(*js)),
            **spec_kwargs))
    return specs


def _read_parts(refs, starts):
    val = refs[0][...]
    for r, s in zip(refs[1:], starts[1:]):
        val = jnp.where(pl.program_id(0) >= s, r[...], val)
    return val


def _ffn_up_body(*refs, starts):
    x_refs = refs[:len(starts)]
    g_ref, wg_ref, wu_ref, act_ref, h_ref = refs[len(starts):]

    @pl.when(pl.program_id(1) == 0)
    def _():
        h_ref[...] = _rms_rows(_read_parts(x_refs, starts), g_ref[...]).astype(BF16)

    h = h_ref[...]
    gate = _mm(h, wg_ref[...].astype(BF16))
    up = _mm(h, wu_ref[...].astype(BF16))
    act_ref[...] = (gate * _sigmoid(gate) * up).astype(BF16)


def ffn_up(x_parts, gain, wg, wu, *, bm, bf):
    d, ff = wg.shape
    starts, nrow = _part_starts(x_parts, bm)
    return pl.pallas_call(
        functools.partial(_ffn_up_body, starts=starts),
        grid=(nrow, ff // bf),
        in_specs=_part_specs(x_parts, bm, d, lambda j: 0, pipeline_mode=pl.Buffered(1)) + [
            pl.BlockSpec((1, d), lambda i, j: (0, 0)),
            pl.BlockSpec((d, bf), lambda i, j: (0, j)),
            pl.BlockSpec((d, bf), lambda i, j: (0, j)),
        ],
        out_specs=pl.BlockSpec((bm, bf), lambda i, j: (i, j)),
        out_shape=jax.ShapeDtypeStruct((nrow * bm, ff), BF16),
        scratch_shapes=[pltpu.VMEM((bm, d), BF16)],
        compiler_params=_params("parallel", "arbitrary"),
        name="ffn_up",
    )(*x_parts, gain.reshape(1, d), wg, wu)


def _mm_res_body(a_ref, w_ref, *refs, scale, starts):
    x_refs, o_ref = refs[:-1], refs[-1]
    o_ref[...] = _read_parts(x_refs, starts) + scale * _mm(a_ref[...], w_ref[...])


def matmul_residual(a, w, x_parts, *, scale, bm, bn):
    m, k = a.shape
    n = w.shape[1]
    starts, nrow = _part_starts(x_parts, bm)
    assert nrow * bm == m
    return pl.pallas_call(
        functools.partial(_mm_res_body, scale=scale, starts=starts),
        grid=(nrow, n // bn),
        in_specs=[
            pl.BlockSpec((bm, k), lambda i, j: (i, 0)),
            pl.BlockSpec((k, bn), lambda i, j: (0, j)),
        ] + _part_specs(x_parts, bm, bn, lambda j: j),
        out_specs=pl.BlockSpec((bm, bn), lambda i, j: (i, j)),
        out_shape=jax.ShapeDtypeStruct((m, n), F32),
        compiler_params=_params("parallel", "arbitrary"),
        name="matmul_residual",
    )(a, w, *x_parts)


def _in_proj_body(x_ref, g_ref, w_ref, ws_ref, p_ref, ps_ref, h_ref):
    @pl.when(pl.program_id(1) == 0)
    def _():
        h = _rms_rows(x_ref[...], g_ref[...])
        h_ref[...] = h.astype(BF16)
        ps_ref[...] = jnp.dot(h, ws_ref[...], precision=HIGHEST, preferred_element_type=F32)

    p_ref[...] = _mm(h_ref[...], w_ref[...])


def in_proj(x, gain, w_big, w_small, *, bm, bn):
    m, d = x.shape
    n = w_big.shape[1]
    return pl.pallas_call(
        _in_proj_body,
        grid=(m // bm, n // bn),
        in_specs=[
            pl.BlockSpec((bm, d), lambda i, j: (i, 0)),
            pl.BlockSpec((1, d), lambda i, j: (0, 0)),
            pl.BlockSpec((d, bn), lambda i, j: (0, j)),
            pl.BlockSpec((d, GATE_LANES), lambda i, j: (0, 0)),
        ],
        out_specs=[
            pl.BlockSpec((bm, bn), lambda i, j: (i, j)),
            pl.BlockSpec((bm, GATE_LANES), lambda i, j: (i, 0)),
        ],
        out_shape=[
            jax.ShapeDtypeStruct((m, n), F32),
            jax.ShapeDtypeStruct((m, GATE_LANES), F32),
        ],
        scratch_shapes=[pltpu.VMEM((bm, d), BF16)],
        compiler_params=_params("parallel", "arbitrary"),
        name="in_proj",
    )(x, gain.reshape(1, d), w_big, w_small)


def _merge_body(*refs, starts):
    n = len(starts)
    ha_refs, ob_refs = refs[:n], refs[n:2 * n]
    wa_ref, wb_ref, ga_ref, gb_ref, o_ref = refs[2 * n:]
    ya = _mm(_read_parts(ha_refs, starts), wa_ref[...])
    yb = _mm(_read_parts(ob_refs, starts), wb_ref[...])
    o_ref[...] = (_sigmoid(ga_ref[...]) * ya + _sigmoid(gb_ref[...]) * yb).astype(BF16)


def merge(ha_parts, ob_parts, wa, wb, p_big, *, ga_off, gb_off, bm, bn):
    ka, n = wa.shape
    kb = wb.shape[0]
    assert ga_off % bn == 0 and gb_off % bn == 0
    ga_blk, gb_blk = ga_off // bn, gb_off // bn
    starts, nrow = _part_starts(ha_parts, bm)
    assert _part_starts(ob_parts, bm) == (starts, nrow) and nrow * bm == p_big.shape[0]
    return pl.pallas_call(
        functools.partial(_merge_body, starts=starts),
        grid=(nrow, n // bn),
        in_specs=_part_specs(ha_parts, bm, ka, lambda j: 0) + _part_specs(ob_parts, bm, kb, lambda j: 0) + [
            pl.BlockSpec((ka, bn), lambda i, j: (0, j)),
            pl.BlockSpec((kb, bn), lambda i, j: (0, j)),
            pl.BlockSpec((bm, bn), lambda i, j: (i, ga_blk + j)),
            pl.BlockSpec((bm, bn), lambda i, j: (i, gb_blk + j)),
        ],
        out_specs=pl.BlockSpec((bm, bn), lambda i, j: (i, j)),
        out_shape=jax.ShapeDtypeStruct((nrow * bm, n), BF16),
        compiler_params=_params("parallel", "arbitrary"),
        name="merge",
    )(*ha_parts, *ob_parts, wa, wb, p_big, p_big)


def _rmsnorm_body(x_ref, g_ref, *o_refs, starts, nrow):
    y = _rms_rows(x_ref[...], g_ref[...])
    i = pl.program_id(0)
    ends = starts[1:] + (nrow,)
    for o_ref, lo, hi in zip(o_refs, starts, ends):
        @pl.when((i >= lo) & (i < hi))
        def _(o_ref=o_ref):
            o_ref[...] = y


def rmsnorm_rows(x, gain, part_rows, *, bm):
    m, d = x.shape
    parts = [jax.ShapeDtypeStruct((r, d), F32) for r in part_rows]
    starts, nrow = _part_starts(parts, bm)
    assert nrow * bm == m
    return pl.pallas_call(
        functools.partial(_rmsnorm_body, starts=starts, nrow=nrow),
        grid=(nrow,),
        in_specs=[pl.BlockSpec((bm, d), lambda i: (i, 0)), pl.BlockSpec((1, d), lambda i: (0, 0))],
        out_specs=_part_specs(parts, bm, d, lambda: 0),
        out_shape=parts,
        compiler_params=_params("arbitrary"),
        name="rmsnorm_rows",
    )(x, gain.reshape(1, d))


def _mlstm_body(q_ref, k_ref, v_ref, ao_ref, gc_ref, gbias_ref, norm_ref, c0_ref, n0_ref, m0_ref,
                ha_ref, c_ref, n_ref, m_ref, *, heads, dk, dv, chunk, bb, single_chunk):
    L = chunk
    if single_chunk:
        c_src, n_src, m_src = c0_ref, n0_ref, m0_ref
    else:
        c_src, n_src, m_src = c_ref, n_ref, m_ref

        @pl.when(pl.program_id(1) == 0)
        def _():
            c_ref[...] = c0_ref[...]
            n_ref[...] = n0_ref[...]
            m_ref[...] = m0_ref[...]

    row = lax.broadcasted_iota(jnp.int32, (L, L), 0)
    col = lax.broadcasted_iota(jnp.int32, (L, L), 1)
    causal = (row >= col)[None]
    tri = jnp.where(row >= col, 1.0, 0.0).astype(F32)
    pairs = [(bi, h) for bi in range(bb) for h in range(heads)]

    ig_cols, ig_rows, b_cols, b_rows = [], [], [], []
    for bi in range(bb):
        gates = gc_ref[bi * L:(bi + 1) * L, :] + gbias_ref[...]
        log_f = jnp.minimum(gates, 0.0) - jnp.log1p(jnp.exp(-jnp.abs(gates)))
        b_all = jnp.dot(tri, log_f, precision=HIGHEST, preferred_element_type=F32)
        gates_t, b_all_t = gates.T, b_all.T
        ig_cols += [gates[:, h:h + 1] for h in range(heads)]
        ig_rows += [gates_t[h:h + 1, :] for h in range(heads)]
        b_cols += [b_all[:, heads + h:heads + h + 1] for h in range(heads)]
        b_rows += [b_all_t[heads + h:heads + h + 1, :] for h in range(heads)]
    ig_c, ig_r = jnp.stack(ig_cols), jnp.stack(ig_rows)
    b_c, b_r = jnp.stack(b_cols), jnp.stack(b_rows)
    m_prev = jnp.stack([m_src[bi][0:1, h:h + 1] for bi, h in pairs])

    log_d = jnp.where(causal, b_c - b_r + ig_r, NEG)
    m_inter = m_prev + b_c
    m_t = jnp.maximum(m_inter, jnp.max(log_d, axis=-1, keepdims=True))
    d_mat = jnp.exp(log_d - m_t)
    inter = jnp.exp(m_inter - m_t)

    seq_rows = lambda ref: [ref[bi * L:(bi + 1) * L, :] for bi in range(bb)]
    q = _split_heads(seq_rows(q_ref), heads, dk)
    k = _split_heads(seq_rows(k_ref), heads, dk) * (dk ** -0.5)
    v = _split_heads(seq_rows(v_ref), heads, dv)
    qb, kb, vb = q.astype(BF16), k.astype(BF16), v.astype(BF16)
    c_old = c_src[...].reshape(bb * heads, dk, dv)
    n_old = jnp.stack([n_src[bi, h:h + 1, :] for bi, h in pairs])

    s = _bmm_nt(qb, kb) * d_mat
    num = inter * _bmm(qb, c_old.astype(BF16)) + _bmm(s.astype(BF16), vb)
    den = inter * jnp.sum(q * n_old, axis=-1, keepdims=True) + jnp.sum(s, axis=-1, keepdims=True)
    h_out = num / jnp.maximum(jnp.abs(den), jnp.exp(-m_t))

    m_new = m_t[:, L - 1:L, :]
    b_last = b_c[:, L - 1:L, :]
    carry = jnp.exp(m_prev + b_last - m_new)
    kw = k * jnp.exp(b_last - b_c + ig_c - m_new)
    c_new = carry * c_old + _bmm_tn(kw.astype(BF16), vb)
    c_ref[...] = c_new.reshape(bb, heads, dk, dv)
    n_new = carry * n_old + jnp.sum(kw, axis=1, keepdims=True)

    gain = jnp.stack([norm_ref[:, h * dv:(h + 1) * dv] for _, h in pairs])
    gate_o = _sigmoid(_split_heads(seq_rows(ao_ref), heads, dv))
    ha = (_rms_rows(h_out, gain) * gate_o).astype(BF16)

    lane = lax.broadcasted_iota(jnp.int32, (8, GATE_LANES), 1)
    for bi in range(bb):
        m_next = m_src[bi]
        for h in range(heads):
            i = bi * heads + h
            n_ref[bi, h:h + 1, :] = n_new[i]
            m_next = jnp.where(lane == h, m_new[i], m_next)
            ha_ref[bi * L:(bi + 1) * L, h * dv:(h + 1) * dv] = ha[i]
        m_ref[bi] = m_next


def mlstm(p_big, p_small, gbias, norm_row, c0, n0, m0, *, row0, seq, chunk, heads, dk, dv,
          q_off, k_off, v_off, ao_off):
    bsz = c0.shape[0]
    nc = seq // chunk
    bb = 2 if (nc == 1 and bsz % 2 == 0) else 1
    blk = bb * chunk
    qk_w, v_w = heads * dk, heads * dv
    assert q_off % qk_w == 0 and k_off % qk_w == 0 and v_off % v_w == 0 and ao_off % v_w == 0
    assert row0 % blk == 0
    rows = lambda b, c: row0 // blk + b * nc + c
    body = functools.partial(_mlstm_body, heads=heads, dk=dk, dv=dv, chunk=chunk, bb=bb,
                             single_chunk=(nc == 1))
    return pl.pallas_call(
        body,
        grid=(bsz // bb, nc),
        in_specs=[
            pl.BlockSpec((blk, qk_w), lambda b, c: (rows(b, c), q_off // qk_w)),
            pl.BlockSpec((blk, qk_w), lambda b, c: (rows(b, c), k_off // qk_w)),
            pl.BlockSpec((blk, v_w), lambda b, c: (rows(b, c), v_off // v_w)),
            pl.BlockSpec((blk, v_w), lambda b, c: (rows(b, c), ao_off // v_w)),
            pl.BlockSpec((blk, GATE_LANES), lambda b, c: (rows(b, c), 0)),
            pl.BlockSpec((1, GATE_LANES), lambda b, c: (0, 0)),
            pl.BlockSpec((1, v_w), lambda b, c: (0, 0)),
            pl.BlockSpec((bb, heads, dk, dv), lambda b, c: (b, 0, 0, 0)),
            pl.BlockSpec((bb, heads, dk), lambda b, c: (b, 0, 0)),
            pl.BlockSpec((bb, 8, GATE_LANES), lambda b, c: (b, 0, 0)),
        ],
        out_specs=[
            pl.BlockSpec((blk, v_w), lambda b, c: (b * nc + c, 0)),
            pl.BlockSpec((bb, heads, dk, dv), lambda b, c: (b, 0, 0, 0)),
            pl.BlockSpec((bb, heads, dk), lambda b, c: (b, 0, 0)),
            pl.BlockSpec((bb, 8, GATE_LANES), lambda b, c: (b, 0, 0)),
        ],
        out_shape=[
            jax.ShapeDtypeStruct((bsz * seq, v_w), BF16),
            jax.ShapeDtypeStruct(c0.shape, F32),
            jax.ShapeDtypeStruct(n0.shape, F32),
            jax.ShapeDtypeStruct(m0.shape, F32),
        ],
        compiler_params=_params("parallel", "arbitrary"),
        name="mlstm",
    )(p_big, p_big, p_big, p_big, p_small, gbias, norm_row, c0, n0, m0)


def _shift_rows(x, prev, j):
    xs = pltpu.roll(x, j, axis=0)
    row = lax.broadcasted_iota(jnp.int32, prev.shape, 0)
    head = jnp.where(row < j, pltpu.roll(prev, j, axis=0), xs[0:8])
    if x.shape[0] == 8:
        return head
    return jnp.concatenate([head, xs[8:]], axis=0)


def _conv_silu(x_ref, w_ref, tail_src, tail_ref, width):
    outs = []
    for bi in range(tail_ref.shape[0]):
        L = x_ref.shape[0] // tail_ref.shape[0]
        x = x_ref[bi * L:(bi + 1) * L, :]
        prev = tail_src[bi]
        acc = x * w_ref[width - 1:width, :]
        for j in range(1, width):
            acc = acc + _shift_rows(x, prev, j) * w_ref[width - 1 - j:width - j, :]
        tail_ref[bi] = x[L - 8:, :]
        outs.append(acc * _sigmoid(acc))
    return outs


def _split_heads(seqs, heads, width):
    return jnp.stack([a[:, i * width:(i + 1) * width] for a in seqs for i in range(heads)])


def _bmm(a, b):
    return jnp.einsum('hij,hjk->hik', a, b, preferred_element_type=F32)


def _bmm_nt(a, b):
    return jnp.einsum('hik,hjk->hij', a, b, preferred_element_type=F32)


def _bmm_tn(a, b):
    return jnp.einsum('hsk,hsv->hkv', a, b, preferred_element_type=F32)


def _gdn_body(xq_ref, xk_ref, xv_ref, z_ref, wq_ref, wk_ref, wv_ref, tq0_ref, tk0_ref, tv0_ref,
              gc_ref, gbias_ref, alog_ref, norm_ref, s0_ref,
              ob_ref, s_ref, tq_ref, tk_ref, tv_ref, *, heads, dk, dv, chunk, width, gl_col, beta_col,
              single_chunk):
    L = chunk
    bb = s_ref.shape[0]
    if single_chunk:
        s_src, tq_src, tk_src, tv_src = s0_ref, tq0_ref, tk0_ref, tv0_ref
    else:
        s_src, tq_src, tk_src, tv_src = s_ref, tq_ref, tk_ref, tv_ref

        @pl.when(pl.program_id(1) == 0)
        def _():
            s_ref[...] = s0_ref[...]
            tq_ref[...] = tq0_ref[...]
            tk_ref[...] = tk0_ref[...]
            tv_ref[...] = tv0_ref[...]

    q = _split_heads(_conv_silu(xq_ref, wq_ref, tq_src, tq_ref, width), heads, dk)
    k = _split_heads(_conv_silu(xk_ref, wk_ref, tk_src, tk_ref, width), heads, dk)
    v = _split_heads(_conv_silu(xv_ref, wv_ref, tv_src, tv_ref, width), heads, dv)
    q = q * lax.rsqrt(jnp.sum(q * q, axis=-1, keepdims=True) + EPS) * (dk ** -0.5)
    k = k * lax.rsqrt(jnp.sum(k * k, axis=-1, keepdims=True) + EPS)

    row = lax.broadcasted_iota(jnp.int32, (L, L), 0)
    col = lax.broadcasted_iota(jnp.int32, (L, L), 1)
    causal = (row >= col)[None]
    strict = (row > col)[None]
    tri = jnp.where(row >= col, 1.0, 0.0).astype(F32)

    g_cols, g_rows, beta_cols = [], [], []
    for bi in range(bb):
        raw = gc_ref[bi * L:(bi + 1) * L, :] + gbias_ref[...]
        gl_all = -jnp.exp(alog_ref[...]) * _softplus(raw)
        g_all = jnp.dot(tri, gl_all, precision=HIGHEST, preferred_element_type=F32)
        g_all_t = g_all.T
        beta_all = _sigmoid(raw)
        g_cols += [g_all[:, gl_col + i:gl_col + i + 1] for i in range(heads)]
        g_rows += [g_all_t[gl_col + i:gl_col + i + 1, :] for i in range(heads)]
        beta_cols += [beta_all[:, beta_col + i:beta_col + i + 1] for i in range(heads)]
    g_c, g_r, beta_c = jnp.stack(g_cols), jnp.stack(g_rows), jnp.stack(beta_cols)
    eg_c = jnp.exp(g_c)
    decay = jnp.exp(jnp.where(causal, g_c - g_r, NEG))

    kb = k.astype(BF16)
    both = _bmm_nt(jnp.concatenate([k, q], axis=1).astype(BF16), kb)
    kk, qk = both[:, :L], both[:, L:]

    x_pow = jnp.where(strict, -(beta_c * kk * decay), 0.0)
    n_inv = x_pow
    for _ in range(max(int(math.ceil(math.log2(L))) - 1, 0)):
        xb = x_pow.astype(BF16)
        x_pow = _bmm(xb, xb)
        n_inv = n_inv + x_pow + _bmm(n_inv.astype(BF16), x_pow.astype(BF16))
    rhs = jnp.concatenate([v * beta_c, k * (beta_c * eg_c)], axis=-1)
    sol = rhs + _bmm(n_inv.astype(BF16), rhs.astype(BF16))
    u, w = sol[..., :dv], sol[..., dv:]

    s_old = s_src[...].reshape(bb * heads, dk, dv)
    sb = s_old.astype(BF16)
    ws_qs = _bmm(jnp.concatenate([w, q], axis=1).astype(BF16), sb)
    delta = u - ws_qs[:, :L]
    db = delta.astype(BF16)
    o = eg_c * ws_qs[:, L:] + _bmm((qk * decay).astype(BF16), db)
    g_last = g_c[:, L - 1:L, :]
    kd = k * jnp.exp(g_last - g_c)
    s_new = jnp.exp(g_last) * s_old + _bmm_tn(kd.astype(BF16), db)
    s_ref[...] = s_new.reshape(bb, heads, dk, dv)

    o = _rms_rows(o, norm_ref[...])
    for bi in range(bb):
        for i in range(heads):
            z = z_ref[bi * L:(bi + 1) * L, i * dv:(i + 1) * dv]
            ob_ref[bi * L:(bi + 1) * L, i * dv:(i + 1) * dv] = (
                o[bi * heads + i] * (z * _sigmoid(z))).astype(BF16)


def gdn(p_big, p_small, gbias, alog_row, norm_row, conv_w, tails0, s0, *, row0, seq, chunk,
        q_off, k_off, v_off, z_off, gl_col, beta_col):
    bsz, heads, dk, dv = s0.shape
    assert dk == dv
    width = conv_w[0].shape[0]
    nc = seq // chunk
    bb = math.gcd(bsz, 4) if nc == 1 else 1
    blk = bb * chunk
    hw = heads * dk
    assert all(off % hw == 0 for off in (q_off, k_off, v_off, z_off)) and row0 % blk == 0
    rows = lambda b, c: row0 // blk + b * nc + c
    x_spec = lambda off: pl.BlockSpec((blk, hw), lambda b, c: (rows(b, c), off // hw))
    w_spec = pl.BlockSpec((width, hw), lambda b, c: (0, 0))
    tail_spec = pl.BlockSpec((bb, 8, hw), lambda b, c: (b, 0, 0))
    s_spec = pl.BlockSpec((bb, heads, dk, dv), lambda b, c: (b, 0, 0, 0))
    vec_spec = lambda n: pl.BlockSpec((1, n), lambda b, c: (0, 0))
    body = functools.partial(_gdn_body, heads=heads, dk=dk, dv=dv, chunk=chunk, width=width,
                             gl_col=gl_col, beta_col=beta_col, single_chunk=(nc == 1))
    tail_shape = jax.ShapeDtypeStruct(tails0[0].shape, F32)
    return pl.pallas_call(
        body,
        grid=(bsz // bb, nc),
        in_specs=[
            x_spec(q_off), x_spec(k_off), x_spec(v_off), x_spec(z_off),
            w_spec, w_spec, w_spec, tail_spec, tail_spec, tail_spec,
            pl.BlockSpec((blk, GATE_LANES), lambda b, c: (rows(b, c), 0)),
            vec_spec(GATE_LANES), vec_spec(GATE_LANES), vec_spec(dv), s_spec,
        ],
        out_specs=[
            pl.BlockSpec((blk, hw), lambda b, c: (b * nc + c, 0)),
            s_spec, tail_spec, tail_spec, tail_spec,
        ],
        out_shape=[
            jax.ShapeDtypeStruct((bsz * seq, heads * dv), BF16),
            jax.ShapeDtypeStruct(s0.shape, F32),
            tail_shape, tail_shape, tail_shape,
        ],
        compiler_params=_params("parallel", "arbitrary"),
        name="gdn",
    )(p_big, p_big, p_big, p_big, conv_w[0], conv_w[1], conv_w[2], tails0[0], tails0[1], tails0[2],
      p_small, gbias, alog_row, norm_row, s0)


def _pad_lanes(vec, offset):
    return jnp.zeros((1, GATE_LANES), F32).at[0, offset:offset + vec.shape[0]].set(vec.astype(F32))


def _layer(x_parts, streams, lw, dims):
    h_a, dk_a, dv_a, h_b, dk_b, dv_b, width = dims
    a_qk, a_v, b_k, b_v = h_a * dk_a, h_a * dv_a, h_b * dk_b, h_b * dv_b
    d = x_parts[0].shape[1]
    ff = lw["ffn1_gate"].shape[1]
    bm = 1024
    for p in x_parts:
        bm = _block(p.shape[0], bm)
    bf, bd, bd_down, bd_out = _block(ff, 512), _block(d, 512), _block(d, 512), _block(d, 1024)

    act = ffn_up(x_parts, lw["norm1"], lw["ffn1_gate"], lw["ffn1_up"], bm=bm, bf=bf)
    x = matmul_residual(act, lw["ffn1_down"], x_parts, scale=0.5, bm=bm, bn=bd_down)

    offs = dict(q=0, k=a_qk, v=2 * a_qk, ao=2 * a_qk + a_v)
    offs["bq"] = offs["ao"] + a_v
    offs["bk"] = offs["bq"] + b_k
    offs["bv"] = offs["bk"] + b_k
    offs["bz"] = offs["bv"] + b_v
    offs["ga"] = offs["bz"] + b_v
    offs["gb"] = offs["ga"] + d
    p_big, p_small = in_proj(x, lw["norm2"], lw["w_big"], lw["w_small"], bm=bm,
                             bn=_block(lw["w_big"].shape[1], 1024))

    gl_col, beta_col = 2 * h_a, 2 * h_a + h_b
    gbias = (_pad_lanes(lw["mlstm_b_i"], 0) + _pad_lanes(lw["mlstm_b_f"], h_a)
             + _pad_lanes(lw["gdn_dt_bias"], gl_col))
    alog_row = _pad_lanes(lw["gdn_A_log"], gl_col)
    conv_w = [lw["gdn_conv"][:, i * b_k:(i + 1) * b_k] for i in range(3)]

    ha_parts, ob_parts, new_states = [], [], []
    for st in streams:
        bsz = st["bsz"]
        m0 = jnp.zeros((bsz, 8, GATE_LANES), F32).at[:, :, :h_a].set(
            jnp.broadcast_to(st["m"][:, None, :], (bsz, 8, h_a)))
        ha, c1, n1, m1 = mlstm(
            p_big, p_small, gbias, lw["mlstm_norm"].reshape(1, a_v), st["C"], st["n"], m0,
            row0=st["row0"], seq=st["seq"], chunk=st["chunk_a"], heads=h_a, dk=dk_a, dv=dv_a,
            q_off=offs["q"], k_off=offs["k"], v_off=offs["v"], ao_off=offs["ao"])
        tails0 = [jnp.pad(st["conv"][:, :, i * b_k:(i + 1) * b_k], ((0, 0), (8 - (width - 1), 0), (0, 0)))
                  for i in range(3)]
        ob, s1, tq, tk, tv = gdn(
            p_big, p_small, gbias, alog_row, lw["gdn_norm"].reshape(1, dv_b), conv_w, tails0, st["S"],
            row0=st["row0"], seq=st["seq"], chunk=st["chunk_b"],
            q_off=offs["bq"], k_off=offs["bk"], v_off=offs["bv"], z_off=offs["bz"],
            gl_col=gl_col, beta_col=beta_col)
        conv1 = jnp.concatenate([tq, tk, tv], axis=-1)[:, 8 - (width - 1):, :]
        ha_parts.append(ha)
        ob_parts.append(ob)
        new_states.append((conv1, s1, c1, n1, m1[:, 0, :h_a]))

    merged = merge(ha_parts, ob_parts, lw["w_branch_a"], lw["w_branch_b"], p_big,
                   ga_off=offs["ga"], gb_off=offs["gb"], bm=bm, bn=bd)
    x = matmul_residual(merged, lw["w_out"], [x], scale=1.0, bm=bm, bn=bd_out)

    act = ffn_up([x], lw["norm3"], lw["ffn2_gate"], lw["ffn2_up"], bm=bm, bf=bf)
    x = matmul_residual(act, lw["ffn2_down"], [x], scale=0.5, bm=bm, bn=bd_down)
    return x, new_states


def _split_w_in_body(wt_ref, big_ref):
    big_ref[...] = wt_ref[...].T.astype(BF16)


def split_w_in(w_in, dims):
    h_a, dk_a, dv_a, h_b, dk_b, dv_b, _ = dims
    a_qk, a_v, b_k, b_v = h_a * dk_a, h_a * dv_a, h_b * dk_b, h_b * dv_b
    d, d_in = w_in.shape
    runs = [(2 * a_qk + a_v, True), (2 * h_a, False), (a_v + 2 * b_k + b_v, True), (2 * h_b, False),
            (b_v + 2 * d, True)]
    assert sum(r for r, _ in runs) == d_in
    big_segs, small_segs, src, big_dst = [], [], 0, 0
    for width, is_big in runs:
        if is_big:
            big_segs.append((src, width, big_dst))
            big_dst += width
        else:
            small_segs.append((src, width))
        src += width
    rb = 1024
    for s, width, _ in big_segs:
        assert s % 8 == 0
        rb = _block(width, rb)
    w_t = jnp.swapaxes(w_in, 0, 1)

    def src_row(i):
        row = i * rb
        for s, _, dst in big_segs[1:]:
            row = jnp.where(i * rb >= dst, i * rb + (s - dst), row)
        return pl.multiple_of(row, 8)

    w_big = pl.pallas_call(
        _split_w_in_body,
        grid=(big_dst // rb,),
        in_specs=[pl.BlockSpec((pl.Element(rb), pl.Element(d)), lambda i: (src_row(i), 0))],
        out_specs=pl.BlockSpec((d, rb), lambda i: (0, i)),
        out_shape=jax.ShapeDtypeStruct((d, big_dst), BF16),
        compiler_params=_params("parallel"),
        name="split_w_in",
    )(w_t)
    small = jnp.concatenate([w_t[s:s + width] for s, width in small_segs], axis=0)
    assert small.shape[0] <= GATE_LANES
    w_small = jnp.pad(small, ((0, GATE_LANES - small.shape[0]), (0, 0))).T
    return w_big, w_small


def _prep_layer_weights(l, w, dims):
    lw = {k: w[k][l] for k in w if k != "w_in"}
    for k in ("ffn1_down", "ffn2_down", "w_branch_a", "w_branch_b", "w_out"):
        lw[k] = lw[k].astype(BF16)
    lw["w_big"], lw["w_small"] = split_w_in(w["w_in"][l], dims)
    return lw


def kernel(x_prompt, x_sample, state_conv, state_gdn, state_mlstm_C, state_mlstm_n, state_mlstm_m,
           norm1, ffn1_gate, ffn1_up, ffn1_down, norm2, w_in, mlstm_b_i, mlstm_b_f, mlstm_norm,
           gdn_conv, gdn_A_log, gdn_dt_bias, gdn_norm, w_branch_a, w_branch_b, w_out,
           norm3, ffn2_gate, ffn2_up, ffn2_down, norm_f):
    depth = norm1.shape[0]
    bp, sp, d = x_prompt.shape
    bs, ss, _ = x_sample.shape
    _, _, h_a, dk_a, dv_a = state_mlstm_C.shape
    _, _, h_b, dk_b, dv_b = state_gdn.shape
    width = gdn_conv.shape[1]
    dims = (h_a, dk_a, dv_a, h_b, dk_b, dv_b, width)
    w = dict(norm1=norm1, ffn1_gate=ffn1_gate, ffn1_up=ffn1_up, ffn1_down=ffn1_down, norm2=norm2,
             w_in=w_in, mlstm_b_i=mlstm_b_i, mlstm_b_f=mlstm_b_f, mlstm_norm=mlstm_norm,
             gdn_conv=gdn_conv, gdn_A_log=gdn_A_log, gdn_dt_bias=gdn_dt_bias, gdn_norm=gdn_norm,
             w_branch_a=w_branch_a, w_branch_b=w_branch_b, w_out=w_out, norm3=norm3,
             ffn2_gate=ffn2_gate, ffn2_up=ffn2_up, ffn2_down=ffn2_down)

    x_parts = [x_prompt.reshape(bp * sp, d), x_sample.reshape(bs * ss, d)]
    zeros = lambda *shape: jnp.zeros(shape, F32)
    prompt = dict(row0=0, bsz=bp, seq=sp, chunk_a=math.gcd(sp, 256), chunk_b=math.gcd(sp, CHUNK))
    sample = dict(row0=bp * sp, bsz=bs, seq=ss, chunk_a=math.gcd(ss, 256), chunk_b=math.gcd(ss, CHUNK))

    per_layer = [[], []]
    for l in range(depth):
        lw = _prep_layer_weights(l, w, dims)
        prompt.update(conv=zeros(bp, width - 1, (2 * dk_b + dv_b) * h_b), S=zeros(bp, h_b, dk_b, dv_b),
                      C=zeros(bp, h_a, dk_a, dv_a), n=zeros(bp, h_a, dk_a), m=zeros(bp, h_a))
        sample.update(conv=state_conv[l], S=state_gdn[l], C=state_mlstm_C[l], n=state_mlstm_n[l],
                      m=state_mlstm_m[l])
        x, states = _layer(x_parts, [prompt, sample], lw, dims)
        x_parts = [x]
        per_layer[0].append(states[0])
        per_layer[1].append(states[1])

    part_rows = (bp * sp, bs * ss)
    y_prompt, y_sample = rmsnorm_rows(x, norm_f, part_rows,
                                      bm=_block(math.gcd(*part_rows), 512))
    stack = lambda sts: tuple(jnp.stack(z) for z in zip(*sts))
    return ((y_prompt.reshape(bp, sp, d), y_sample.reshape(bs, ss, d))
            + stack(per_layer[0]) + stack(per_layer[1]))
```

```python
import functools
import math

import jax
import jax.numpy as jnp
from jax import lax
from jax.experimental import pallas as pl
from jax.experimental.pallas import tpu as pltpu

F32 = jnp.float32
BF16 = jnp.bfloat16
HIGHEST = lax.Precision.HIGHEST
EPS = 1e-6
NEG = -1e30
CHUNK = 64
GATE_LANES = 128
V7X_VMEM_BYTES = 64 * 1024 * 1024
VMEM_LIMIT = V7X_VMEM_BYTES - 8 * 1024 * 1024


def _block(n, preferred):
    return math.gcd(n, preferred)


def _params(*semantics):
    return pltpu.CompilerParams(dimension_semantics=semantics, vmem_limit_bytes=VMEM_LIMIT)


def _sigmoid(x):
    return 1.0 / (1.0 + jnp.exp(-x))


def _softplus(x):
    return jnp.maximum(x, 0.0) + jnp.log1p(jnp.exp(-jnp.abs(x)))


def _rms_rows(x, gain):
    return x * lax.rsqrt(jnp.mean(x * x, axis=-1, keepdims=True) + EPS) * gain


def _mm(a, b):
    return jnp.dot(a, b, preferred_element_type=F32)


def _bmm(a, b):
    return jnp.einsum('hij,hjk->hik', a, b, preferred_element_type=F32)


def _bmm_nt(a, b):
    return jnp.einsum('hik,hjk->hij', a, b, preferred_element_type=F32)


def _bmm_tn(a, b):
    return jnp.einsum('hsk,hsv->hkv', a, b, preferred_element_type=F32)


def _split_heads(seqs, heads, width):
    return jnp.stack([a[:, i * width:(i + 1) * width] for a in seqs for i in range(heads)])


def _part_starts(parts, bm):
    starts, total = [], 0
    for p in parts:
        assert p.shape[0] % bm == 0
        starts.append(total)
        total += p.shape[0] // bm
    return tuple(starts), total


def _part_specs(parts, bm, cols, col_of, **spec_kwargs):
    starts, _ = _part_starts(parts, bm)
    specs = []
    for p, s in zip(parts, starts):
        nb = p.shape[0] // bm
        specs.append(pl.BlockSpec(
            (bm, cols), lambda i, *js, s=s, nb=nb: (jnp.clip(i - s, 0, nb - 1), col_of(*js)),
            **spec_kwargs))
    return specs


def _read_parts(refs, starts):
    val = refs[0][...]
    for r, s in zip(refs[1:], starts[1:]):
        val = jnp.where(pl.program_id(0) >= s, r[...], val)
    return val


def _ffn_up_body(*refs, starts):
    x_refs = refs[:len(starts)]
    g_ref, wg_ref, wu_ref, act_ref, h_ref = refs[len(starts):]

    @pl.when(pl.program_id(1) == 0)
    def _():
        h_ref[...] = _rms_rows(_read_parts(x_refs, starts), g_ref[...]).astype(BF16)

    h = h_ref[...]
    gate = _mm(h, wg_ref[...].astype(BF16))
    up = _mm(h, wu_ref[...].astype(BF16))
    act_ref[...] = (gate * _sigmoid(gate) * up).astype(BF16)


def ffn_up(x_parts, gain, wg, wu, *, bm, bf):
    d, ff = wg.shape
    starts, nrow = _part_starts(x_parts, bm)
    return pl.pallas_call(
        functools.partial(_ffn_up_body, starts=starts),
        grid=(nrow, ff // bf),
        in_specs=_part_specs(x_parts, bm, d, lambda j: 0, pipeline_mode=pl.Buffered(1)) + [
            pl.BlockSpec((1, d), lambda i, j: (0, 0)),
            pl.BlockSpec((d, bf), lambda i, j: (0, j)),
            pl.BlockSpec((d, bf), lambda i, j: (0, j)),
        ],
        out_specs=pl.BlockSpec((bm, bf), lambda i, j: (i, j)),
        out_shape=jax.ShapeDtypeStruct((nrow * bm, ff), BF16),
        scratch_shapes=[pltpu.VMEM((bm, d), BF16)],
        compiler_params=_params("parallel", "arbitrary"),
        name="ffn_up",
    )(*x_parts, gain.reshape(1, d), wg, wu)


def _mm_res_body(a_ref, w_ref, *refs, scale, starts):
    x_refs, o_ref = refs[:-1], refs[-1]
    o_ref[...] = _read_parts(x_refs, starts) + scale * _mm(a_ref[...], w_ref[...])


def matmul_residual(a, w, x_parts, *, scale, bm, bn):
    m, k = a.shape
    n = w.shape[1]
    starts, nrow = _part_starts(x_parts, bm)
    assert nrow * bm == m
    return pl.pallas_call(
        functools.partial(_mm_res_body, scale=scale, starts=starts),
        grid=(nrow, n // bn),
        in_specs=[
            pl.BlockSpec((bm, k), lambda i, j: (i, 0)),
            pl.BlockSpec((k, bn), lambda i, j: (0, j)),
        ] + _part_specs(x_parts, bm, bn, lambda j: j),
        out_specs=pl.BlockSpec((bm, bn), lambda i, j: (i, j)),
        out_shape=jax.ShapeDtypeStruct((m, n), F32),
        compiler_params=_params("parallel", "arbitrary"),
        name="matmul_residual",
    )(a, w, *x_parts)


def _in_proj_body(x_ref, g_ref, w_ref, ws_ref, p_ref, ps_ref, h_ref):
    @pl.when(pl.program_id(1) == 0)
    def _():
        h = _rms_rows(x_ref[...], g_ref[...])
        h_hi = h.astype(BF16)
        h_ref[...] = h_hi
        h_lo = (h - h_hi.astype(F32)).astype(BF16)
        both = _mm(h_hi, ws_ref[...])
        ps_ref[...] = (both[:, :GATE_LANES] + both[:, GATE_LANES:]
                       + _mm(h_lo, ws_ref[:, :GATE_LANES]))

    p_ref[...] = _mm(h_ref[...], w_ref[...])


def in_proj(x, gain, w_big, w_small, *, bm, bn):
    m, d = x.shape
    n = w_big.shape[1]
    ws_hi = w_small.astype(BF16)
    ws_lo = (w_small - ws_hi.astype(F32)).astype(BF16)
    ws_split = jnp.concatenate([ws_hi, ws_lo], axis=1)
    return pl.pallas_call(
        _in_proj_body,
        grid=(m // bm, n // bn),
        in_specs=[
            pl.BlockSpec((bm, d), lambda i, j: (i, 0)),
            pl.BlockSpec((1, d), lambda i, j: (0, 0)),
            pl.BlockSpec((d, bn), lambda i, j: (0, j)),
            pl.BlockSpec((d, 2 * GATE_LANES), lambda i, j: (0, 0)),
        ],
        out_specs=[
            pl.BlockSpec((bm, bn), lambda i, j: (i, j)),
            pl.BlockSpec((bm, GATE_LANES), lambda i, j: (i, 0)),
        ],
        out_shape=[
            jax.ShapeDtypeStruct((m, n), F32),
            jax.ShapeDtypeStruct((m, GATE_LANES), F32),
        ],
        scratch_shapes=[pltpu.VMEM((bm, d), BF16)],
        compiler_params=_params("parallel", "arbitrary"),
        name="in_proj",
    )(x, gain.reshape(1, d), w_big, ws_split)


def _merge_body(*refs, starts):
    n = len(starts)
    ha_refs, ob_refs = refs[:n], refs[n:2 * n]
    wa_ref, wb_ref, ga_ref, gb_ref, o_ref = refs[2 * n:]
    ya = _mm(_read_parts(ha_refs, starts), wa_ref[...])
    yb = _mm(_read_parts(ob_refs, starts), wb_ref[...])
    o_ref[...] = (_sigmoid(ga_ref[...]) * ya + _sigmoid(gb_ref[...]) * yb).astype(BF16)


def merge(ha_parts, ob_parts, wa, wb, p_big, *, ga_off, gb_off, bm, bn):
    ka, n = wa.shape
    kb = wb.shape[0]
    assert ga_off % bn == 0 and gb_off % bn == 0
    ga_blk, gb_blk = ga_off // bn, gb_off // bn
    starts, nrow = _part_starts(ha_parts, bm)
    assert _part_starts(ob_parts, bm) == (starts, nrow) and nrow * bm == p_big.shape[0]
    return pl.pallas_call(
        functools.partial(_merge_body, starts=starts),
        grid=(nrow, n // bn),
        in_specs=_part_specs(ha_parts, bm, ka, lambda j: 0) + _part_specs(ob_parts, bm, kb, lambda j: 0) + [
            pl.BlockSpec((ka, bn), lambda i, j: (0, j)),
            pl.BlockSpec((kb, bn), lambda i, j: (0, j)),
            pl.BlockSpec((bm, bn), lambda i, j: (i, ga_blk + j)),
            pl.BlockSpec((bm, bn), lambda i, j: (i, gb_blk + j)),
        ],
        out_specs=pl.BlockSpec((bm, bn), lambda i, j: (i, j)),
        out_shape=jax.ShapeDtypeStruct((nrow * bm, n), BF16),
        compiler_params=_params("parallel", "arbitrary"),
        name="merge",
    )(*ha_parts, *ob_parts, wa, wb, p_big, p_big)


def _rmsnorm_body(x_ref, g_ref, *o_refs, starts, nrow):
    y = _rms_rows(x_ref[...], g_ref[...])
    i = pl.program_id(0)
    ends = starts[1:] + (nrow,)
    for o_ref, lo, hi in zip(o_refs, starts, ends):
        @pl.when((i >= lo) & (i < hi))
        def _(o_ref=o_ref):
            o_ref[...] = y


def rmsnorm_rows(x, gain, part_rows, *, bm):
    m, d = x.shape
    parts = [jax.ShapeDtypeStruct((r, d), F32) for r in part_rows]
    starts, nrow = _part_starts(parts, bm)
    assert nrow * bm == m
    return pl.pallas_call(
        functools.partial(_rmsnorm_body, starts=starts, nrow=nrow),
        grid=(nrow,),
        in_specs=[pl.BlockSpec((bm, d), lambda i: (i, 0)), pl.BlockSpec((1, d), lambda i: (0, 0))],
        out_specs=_part_specs(parts, bm, d, lambda: 0),
        out_shape=parts,
        compiler_params=_params("arbitrary"),
        name="rmsnorm_rows",
    )(x, gain.reshape(1, d))


def _mlstm_body(q_ref, k_ref, v_ref, ao_ref, gc_ref, gbias_ref, norm_ref, c0_ref, n0_ref, m0_ref,
                ha_ref, c_ref, n_ref, m_ref, *, heads, dk, dv, chunk, bb, single_chunk):
    L = chunk
    if single_chunk:
        c_src, n_src, m_src = c0_ref, n0_ref, m0_ref
    else:
        c_src, n_src, m_src = c_ref, n_ref, m_ref

        @pl.when(pl.program_id(1) == 0)
        def _():
            c_ref[...] = c0_ref[...]
            n_ref[...] = n0_ref[...]
            m_ref[...] = m0_ref[...]

    row = lax.broadcasted_iota(jnp.int32, (L, L), 0)
    col = lax.broadcasted_iota(jnp.int32, (L, L), 1)
    causal = (row >= col)[None]
    tri = jnp.where(row >= col, 1.0, 0.0).astype(F32)
    pairs = [(bi, h) for bi in range(bb) for h in range(heads)]

    ig_cols, ig_rows, b_cols, b_rows = [], [], [], []
    for bi in range(bb):
        gates = gc_ref[bi * L:(bi + 1) * L, :] + gbias_ref[...]
        log_f = jnp.minimum(gates, 0.0) - jnp.log1p(jnp.exp(-jnp.abs(gates)))
        b_all = jnp.dot(tri, log_f, precision=HIGHEST, preferred_element_type=F32)
        gates_t, b_all_t = gates.T, b_all.T
        ig_cols += [gates[:, h:h + 1] for h in range(heads)]
        ig_rows += [gates_t[h:h + 1, :] for h in range(heads)]
        b_cols += [b_all[:, heads + h:heads + h + 1] for h in range(heads)]
        b_rows += [b_all_t[heads + h:heads + h + 1, :] for h in range(heads)]
    ig_c, ig_r = jnp.stack(ig_cols), jnp.stack(ig_rows)
    b_c, b_r = jnp.stack(b_cols), jnp.stack(b_rows)
    m_prev = jnp.stack([m_src[bi][0:1, h:h + 1] for bi, h in pairs])

    log_d = jnp.where(causal, b_c - b_r + ig_r, NEG)
    m_inter = m_prev + b_c
    m_t = jnp.maximum(m_inter, jnp.max(log_d, axis=-1, keepdims=True))
    d_mat = jnp.exp(log_d - m_t)
    inter = jnp.exp(m_inter - m_t)

    seq_rows = lambda ref: [ref[bi * L:(bi + 1) * L, :] for bi in range(bb)]
    q = _split_heads(seq_rows(q_ref), heads, dk)
    k = _split_heads(seq_rows(k_ref), heads, dk) * (dk ** -0.5)
    v = _split_heads(seq_rows(v_ref), heads, dv)
    qb, kb, vb = q.astype(BF16), k.astype(BF16), v.astype(BF16)
    c_old = c_src[...].reshape(bb * heads, dk, dv)
    n_old = jnp.stack([n_src[bi, h:h + 1, :] for bi, h in pairs])

    s = _bmm_nt(qb, kb) * d_mat
    num = inter * _bmm(qb, c_old.astype(BF16)) + _bmm(s.astype(BF16), vb)
    den = inter * jnp.sum(q * n_old, axis=-1, keepdims=True) + jnp.sum(s, axis=-1, keepdims=True)
    h_out = num / jnp.maximum(jnp.abs(den), jnp.exp(-m_t))

    m_new = m_t[:, L - 1:L, :]
    b_last = b_c[:, L - 1:L, :]
    carry = jnp.exp(m_prev + b_last - m_new)
    kw = k * jnp.exp(b_last - b_c + ig_c - m_new)
    c_new = carry * c_old + _bmm_tn(kw.astype(BF16), vb)
    c_ref[...] = c_new.reshape(bb, heads, dk, dv)
    n_new = carry * n_old + jnp.sum(kw, axis=1, keepdims=True)

    gain = jnp.stack([norm_ref[:, h * dv:(h + 1) * dv] for _, h in pairs])
    gate_o = _sigmoid(_split_heads(seq_rows(ao_ref), heads, dv))
    ha = (_rms_rows(h_out, gain) * gate_o).astype(BF16)

    lane = lax.broadcasted_iota(jnp.int32, (8, GATE_LANES), 1)
    for bi in range(bb):
        m_next = m_src[bi]
        for h in range(heads):
            i = bi * heads + h
            n_ref[bi, h:h + 1, :] = n_new[i]
            m_next = jnp.where(lane == h, m_new[i], m_next)
            ha_ref[bi * L:(bi + 1) * L, h * dv:(h + 1) * dv] = ha[i]
        m_ref[bi] = m_next


def mlstm(p_big, p_small, gbias, norm_row, c0, n0, m0, *, row0, seq, chunk, heads, dk, dv,
          q_off, k_off, v_off, ao_off):
    bsz = c0.shape[0]
    nc = seq // chunk
    bb = math.gcd(bsz, 4) if nc == 1 else 1
    blk = bb * chunk
    qk_w, v_w = heads * dk, heads * dv
    assert q_off % qk_w == 0 and k_off % qk_w == 0 and v_off % v_w == 0 and ao_off % v_w == 0
    assert row0 % blk == 0
    rows = lambda b, c: row0 // blk + b * nc + c
    body = functools.partial(_mlstm_body, heads=heads, dk=dk, dv=dv, chunk=chunk, bb=bb,
                             single_chunk=(nc == 1))
    return pl.pallas_call(
        body,
        grid=(bsz // bb, nc),
        in_specs=[
            pl.BlockSpec((blk, qk_w), lambda b, c: (rows(b, c), q_off // qk_w)),
            pl.BlockSpec((blk, qk_w), lambda b, c: (rows(b, c), k_off // qk_w)),
            pl.BlockSpec((blk, v_w), lambda b, c: (rows(b, c), v_off // v_w)),
            pl.BlockSpec((blk, v_w), lambda b, c: (rows(b, c), ao_off // v_w)),
            pl.BlockSpec((blk, GATE_LANES), lambda b, c: (rows(b, c), 0)),
            pl.BlockSpec((1, GATE_LANES), lambda b, c: (0, 0)),
            pl.BlockSpec((1, v_w), lambda b, c: (0, 0)),
            pl.BlockSpec((bb, heads, dk, dv), lambda b, c: (b, 0, 0, 0)),
            pl.BlockSpec((bb, heads, dk), lambda b, c: (b, 0, 0)),
            pl.BlockSpec((bb, 8, GATE_LANES), lambda b, c: (b, 0, 0)),
        ],
        out_specs=[
            pl.BlockSpec((blk, v_w), lambda b, c: (b * nc + c, 0)),
            pl.BlockSpec((bb, heads, dk, dv), lambda b, c: (b, 0, 0, 0)),
            pl.BlockSpec((bb, heads, dk), lambda b, c: (b, 0, 0)),
            pl.BlockSpec((bb, 8, GATE_LANES), lambda b, c: (b, 0, 0)),
        ],
        out_shape=[
            jax.ShapeDtypeStruct((bsz * seq, v_w), BF16),
            jax.ShapeDtypeStruct(c0.shape, F32),
            jax.ShapeDtypeStruct(n0.shape, F32),
            jax.ShapeDtypeStruct(m0.shape, F32),
        ],
        compiler_params=_params("parallel", "arbitrary"),
        name="mlstm",
    )(p_big, p_big, p_big, p_big, p_small, gbias, norm_row, c0, n0, m0)


def _shift_rows(x, prev, j):
    xs = pltpu.roll(x, j, axis=0)
    row = lax.broadcasted_iota(jnp.int32, prev.shape, 0)
    head = jnp.where(row < j, pltpu.roll(prev, j, axis=0), xs[0:8])
    if x.shape[0] == 8:
        return head
    return jnp.concatenate([head, xs[8:]], axis=0)


def _conv_silu(x_ref, w_ref, tail_src, tail_ref, width):
    outs = []
    for bi in range(tail_ref.shape[0]):
        L = x_ref.shape[0] // tail_ref.shape[0]
        x = x_ref[bi * L:(bi + 1) * L, :]
        prev = tail_src[bi]
        acc = x * w_ref[width - 1:width, :]
        for j in range(1, width):
            acc = acc + _shift_rows(x, prev, j) * w_ref[width - 1 - j:width - j, :]
        tail_ref[bi] = x[L - 8:, :]
        outs.append(acc * _sigmoid(acc))
    return outs


def _gdn_body(xq_ref, xk_ref, xv_ref, z_ref, wq_ref, wk_ref, wv_ref, tq0_ref, tk0_ref, tv0_ref,
              gc_ref, gbias_ref, alog_ref, norm_ref, s0_ref,
              ob_ref, s_ref, tq_ref, tk_ref, tv_ref, *, heads, dk, dv, chunk, width, gl_col, beta_col,
              single_chunk):
    L = chunk
    bb = s_ref.shape[0]
    if single_chunk:
        s_src, tq_src, tk_src, tv_src = s0_ref, tq0_ref, tk0_ref, tv0_ref
    else:
        s_src, tq_src, tk_src, tv_src = s_ref, tq_ref, tk_ref, tv_ref

        @pl.when(pl.program_id(1) == 0)
        def _():
            s_ref[...] = s0_ref[...]
            tq_ref[...] = tq0_ref[...]
            tk_ref[...] = tk0_ref[...]
            tv_ref[...] = tv0_ref[...]

    q = _split_heads(_conv_silu(xq_ref, wq_ref, tq_src, tq_ref, width), heads, dk)
    k = _split_heads(_conv_silu(xk_ref, wk_ref, tk_src, tk_ref, width), heads, dk)
    v = _split_heads(_conv_silu(xv_ref, wv_ref, tv_src, tv_ref, width), heads, dv)
    q = q * lax.rsqrt(jnp.sum(q * q, axis=-1, keepdims=True) + EPS) * (dk ** -0.5)
    k = k * lax.rsqrt(jnp.sum(k * k, axis=-1, keepdims=True) + EPS)

    row = lax.broadcasted_iota(jnp.int32, (L, L), 0)
    col = lax.broadcasted_iota(jnp.int32, (L, L), 1)
    causal = (row >= col)[None]
    strict = (row > col)[None]
    tri = jnp.where(row >= col, 1.0, 0.0).astype(F32)

    g_cols, g_rows, beta_cols = [], [], []
    for bi in range(bb):
        raw = gc_ref[bi * L:(bi + 1) * L, :] + gbias_ref[...]
        gl_all = -jnp.exp(alog_ref[...]) * _softplus(raw)
        g_all = jnp.dot(tri, gl_all, precision=HIGHEST, preferred_element_type=F32)
        g_all_t = g_all.T
        beta_all = _sigmoid(raw)
        g_cols += [g_all[:, gl_col + i:gl_col + i + 1] for i in range(heads)]
        g_rows += [g_all_t[gl_col + i:gl_col + i + 1, :] for i in range(heads)]
        beta_cols += [beta_all[:, beta_col + i:beta_col + i + 1] for i in range(heads)]
    g_c, g_r, beta_c = jnp.stack(g_cols), jnp.stack(g_rows), jnp.stack(beta_cols)
    eg_c = jnp.exp(g_c)
    decay = jnp.exp(jnp.where(causal, g_c - g_r, NEG))

    kb = k.astype(BF16)
    both = _bmm_nt(jnp.concatenate([k, q], axis=1).astype(BF16), kb)
    kk, qk = both[:, :L], both[:, L:]

    x_pow = jnp.where(strict, -(beta_c * kk * decay), 0.0)
    n_inv = x_pow
    for _ in range(max(int(math.ceil(math.log2(L))) - 1, 0)):
        xb = x_pow.astype(BF16)
        x_pow = _bmm(xb, xb)
        n_inv = n_inv + x_pow + _bmm(n_inv.astype(BF16), x_pow.astype(BF16))
    rhs = jnp.concatenate([v * beta_c, k * (beta_c * eg_c)], axis=-1)
    sol = rhs + _bmm(n_inv.astype(BF16), rhs.astype(BF16))
    u, w = sol[..., :dv], sol[..., dv:]

    s_old = s_src[...].reshape(bb * heads, dk, dv)
    sb = s_old.astype(BF16)
    ws_qs = _bmm(jnp.concatenate([w, q], axis=1).astype(BF16), sb)
    delta = u - ws_qs[:, :L]
    db = delta.astype(BF16)
    o = eg_c * ws_qs[:, L:] + _bmm((qk * decay).astype(BF16), db)
    g_last = g_c[:, L - 1:L, :]
    kd = k * jnp.exp(g_last - g_c)
    s_new = jnp.exp(g_last) * s_old + _bmm_tn(kd.astype(BF16), db)
    s_ref[...] = s_new.reshape(bb, heads, dk, dv)

    o = _rms_rows(o, norm_ref[...])
    for bi in range(bb):
        for i in range(heads):
            z = z_ref[bi * L:(bi + 1) * L, i * dv:(i + 1) * dv]
            ob_ref[bi * L:(bi + 1) * L, i * dv:(i + 1) * dv] = (
                o[bi * heads + i] * (z * _sigmoid(z))).astype(BF16)


def gdn(p_big, p_small, gbias, alog_row, norm_row, conv_w, tails0, s0, *, row0, seq, chunk,
        q_off, k_off, v_off, z_off, gl_col, beta_col):
    bsz, heads, dk, dv = s0.shape
    assert dk == dv
    width = conv_w[0].shape[0]
    nc = seq // chunk
    bb = math.gcd(bsz, 4) if nc == 1 else 1
    blk = bb * chunk
    hw = heads * dk
    assert all(off % hw == 0 for off in (q_off, k_off, v_off, z_off)) and row0 % blk == 0
    rows = lambda b, c: row0 // blk + b * nc + c
    x_spec = lambda off: pl.BlockSpec((blk, hw), lambda b, c: (rows(b, c), off // hw))
    w_spec = pl.BlockSpec((width, hw), lambda b, c: (0, 0))
    tail_spec = pl.BlockSpec((bb, 8, hw), lambda b, c: (b, 0, 0))
    s_spec = pl.BlockSpec((bb, heads, dk, dv), lambda b, c: (b, 0, 0, 0))
    vec_spec = lambda n: pl.BlockSpec((1, n), lambda b, c: (0, 0))
    body = functools.partial(_gdn_body, heads=heads, dk=dk, dv=dv, chunk=chunk, width=width,
                             gl_col=gl_col, beta_col=beta_col, single_chunk=(nc == 1))
    tail_shape = jax.ShapeDtypeStruct(tails0[0].shape, F32)
    return pl.pallas_call(
        body,
        grid=(bsz // bb, nc),
        in_specs=[
            x_spec(q_off), x_spec(k_off), x_spec(v_off), x_spec(z_off),
            w_spec, w_spec, w_spec, tail_spec, tail_spec, tail_spec,
            pl.BlockSpec((blk, GATE_LANES), lambda b, c: (rows(b, c), 0)),
            vec_spec(GATE_LANES), vec_spec(GATE_LANES), vec_spec(dv), s_spec,
        ],
        out_specs=[
            pl.BlockSpec((blk, hw), lambda b, c: (b * nc + c, 0)),
            s_spec, tail_spec, tail_spec, tail_spec,
        ],
        out_shape=[
            jax.ShapeDtypeStruct((bsz * seq, heads * dv), BF16),
            jax.ShapeDtypeStruct(s0.shape, F32),
            tail_shape, tail_shape, tail_shape,
        ],
        compiler_params=_params("parallel", "arbitrary"),
        name="gdn",
    )(p_big, p_big, p_big, p_big, conv_w[0], conv_w[1], conv_w[2], tails0[0], tails0[1], tails0[2],
      p_small, gbias, alog_row, norm_row, s0)


def _pad_lanes(vec, offset):
    return jnp.zeros((1, GATE_LANES), F32).at[0, offset:offset + vec.shape[0]].set(vec.astype(F32))


def _layer(x_parts, streams, lw, dims):
    h_a, dk_a, dv_a, h_b, dk_b, dv_b, width = dims
    a_qk, a_v, b_k, b_v = h_a * dk_a, h_a * dv_a, h_b * dk_b, h_b * dv_b
    d = x_parts[0].shape[1]
    ff = lw["ffn1_gate"].shape[1]
    bm = 1024
    for p in x_parts:
        bm = _block(p.shape[0], bm)
    bf, bd, bd_down, bd_out = _block(ff, 512), _block(d, 512), _block(d, 512), _block(d, 1024)

    act = ffn_up(x_parts, lw["norm1"], lw["ffn1_gate"], lw["ffn1_up"], bm=bm, bf=bf)
    x = matmul_residual(act, lw["ffn1_down"], x_parts, scale=0.5, bm=bm, bn=bd_down)

    offs = dict(q=0, k=a_qk, v=2 * a_qk, ao=2 * a_qk + a_v)
    offs["bq"] = offs["ao"] + a_v
    offs["bk"] = offs["bq"] + b_k
    offs["bv"] = offs["bk"] + b_k
    offs["bz"] = offs["bv"] + b_v
    offs["ga"] = offs["bz"] + b_v
    offs["gb"] = offs["ga"] + d
    p_big, p_small = in_proj(x, lw["norm2"], lw["w_big"], lw["w_small"], bm=bm,
                             bn=_block(lw["w_big"].shape[1], 1024))

    gl_col, beta_col = 2 * h_a, 2 * h_a + h_b
    gbias = (_pad_lanes(lw["mlstm_b_i"], 0) + _pad_lanes(lw["mlstm_b_f"], h_a)
             + _pad_lanes(lw["gdn_dt_bias"], gl_col))
    alog_row = _pad_lanes(lw["gdn_A_log"], gl_col)
    conv_w = [lw["gdn_conv"][:, i * b_k:(i + 1) * b_k] for i in range(3)]

    ha_parts, ob_parts, new_states = [], [], []
    for st in streams:
        bsz = st["bsz"]
        m0 = jnp.zeros((bsz, 8, GATE_LANES), F32).at[:, :, :h_a].set(
            jnp.broadcast_to(st["m"][:, None, :], (bsz, 8, h_a)))
        ha, c1, n1, m1 = mlstm(
            p_big, p_small, gbias, lw["mlstm_norm"].reshape(1, a_v), st["C"], st["n"], m0,
            row0=st["row0"], seq=st["seq"], chunk=st["chunk_a"], heads=h_a, dk=dk_a, dv=dv_a,
            q_off=offs["q"], k_off=offs["k"], v_off=offs["v"], ao_off=offs["ao"])
        tails0 = [jnp.pad(st["conv"][:, :, i * b_k:(i + 1) * b_k], ((0, 0), (8 - (width - 1), 0), (0, 0)))
                  for i in range(3)]
        ob, s1, tq, tk, tv = gdn(
            p_big, p_small, gbias, alog_row, lw["gdn_norm"].reshape(1, dv_b), conv_w, tails0, st["S"],
            row0=st["row0"], seq=st["seq"], chunk=st["chunk_b"],
            q_off=offs["bq"], k_off=offs["bk"], v_off=offs["bv"], z_off=offs["bz"],
            gl_col=gl_col, beta_col=beta_col)
        conv1 = jnp.concatenate([tq, tk, tv], axis=-1)[:, 8 - (width - 1):, :]
        ha_parts.append(ha)
        ob_parts.append(ob)
        new_states.append((conv1, s1, c1, n1, m1[:, 0, :h_a]))

    merged = merge(ha_parts, ob_parts, lw["w_branch_a"], lw["w_branch_b"], p_big,
                   ga_off=offs["ga"], gb_off=offs["gb"], bm=bm, bn=bd)
    x = matmul_residual(merged, lw["w_out"], [x], scale=1.0, bm=bm, bn=bd_out)

    act = ffn_up([x], lw["norm3"], lw["ffn2_gate"], lw["ffn2_up"], bm=bm, bf=bf)
    x = matmul_residual(act, lw["ffn2_down"], [x], scale=0.5, bm=bm, bn=bd_down)
    return x, new_states


def _split_w_in_body(wt_ref, big_ref):
    big_ref[...] = wt_ref[...].T.astype(BF16)


def split_w_in(w_in, dims):
    h_a, dk_a, dv_a, h_b, dk_b, dv_b, _ = dims
    a_qk, a_v, b_k, b_v = h_a * dk_a, h_a * dv_a, h_b * dk_b, h_b * dv_b
    d, d_in = w_in.shape
    runs = [(2 * a_qk + a_v, True), (2 * h_a, False), (a_v + 2 * b_k + b_v, True), (2 * h_b, False),
            (b_v + 2 * d, True)]
    assert sum(r for r, _ in runs) == d_in
    big_segs, small_segs, src, big_dst = [], [], 0, 0
    for width, is_big in runs:
        if is_big:
            big_segs.append((src, width, big_dst))
            big_dst += width
        else:
            small_segs.append((src, width))
        src += width
    rb = 1024
    for s, width, _ in big_segs:
        assert s % 8 == 0
        rb = _block(width, rb)
    w_t = jnp.swapaxes(w_in, 0, 1)

    def src_row(i):
        row = i * rb
        for s, _, dst in big_segs[1:]:
            row = jnp.where(i * rb >= dst, i * rb + (s - dst), row)
        return pl.multiple_of(row, 8)

    w_big = pl.pallas_call(
        _split_w_in_body,
        grid=(big_dst // rb,),
        in_specs=[pl.BlockSpec((pl.Element(rb), pl.Element(d)), lambda i: (src_row(i), 0))],
        out_specs=pl.BlockSpec((d, rb), lambda i: (0, i)),
        out_shape=jax.ShapeDtypeStruct((d, big_dst), BF16),
        compiler_params=_params("parallel"),
        name="split_w_in",
    )(w_t)
    small = jnp.concatenate([w_t[s:s + width] for s, width in small_segs], axis=0)
    assert small.shape[0] <= GATE_LANES
    w_small = jnp.pad(small, ((0, GATE_LANES - small.shape[0]), (0, 0))).T
    return w_big, w_small


def _prep_layer_weights(l, w, dims):
    lw = {k: w[k][l] for k in w if k != "w_in"}
    for k in ("ffn1_down", "ffn2_down", "w_branch_a", "w_branch_b", "w_out"):
        lw[k] = lw[k].astype(BF16)
    lw["w_big"], lw["w_small"] = split_w_in(w["w_in"][l], dims)
    return lw


def kernel(x_prompt, x_sample, state_conv, state_gdn, state_mlstm_C, state_mlstm_n, state_mlstm_m,
           norm1, ffn1_gate, ffn1_up, ffn1_down, norm2, w_in, mlstm_b_i, mlstm_b_f, mlstm_norm,
           gdn_conv, gdn_A_log, gdn_dt_bias, gdn_norm, w_branch_a, w_branch_b, w_out,
           norm3, ffn2_gate, ffn2_up, ffn2_down, norm_f):
    depth = norm1.shape[0]
    bp, sp, d = x_prompt.shape
    bs, ss, _ = x_sample.shape
    _, _, h_a, dk_a, dv_a = state_mlstm_C.shape
    _, _, h_b, dk_b, dv_b = state_gdn.shape
    width = gdn_conv.shape[1]
    dims = (h_a, dk_a, dv_a, h_b, dk_b, dv_b, width)
    w = dict(norm1=norm1, ffn1_gate=ffn1_gate, ffn1_up=ffn1_up, ffn1_down=ffn1_down, norm2=norm2,
             w_in=w_in, mlstm_b_i=mlstm_b_i, mlstm_b_f=mlstm_b_f, mlstm_norm=mlstm_norm,
             gdn_conv=gdn_conv, gdn_A_log=gdn_A_log, gdn_dt_bias=gdn_dt_bias, gdn_norm=gdn_norm,
             w_branch_a=w_branch_a, w_branch_b=w_branch_b, w_out=w_out, norm3=norm3,
             ffn2_gate=ffn2_gate, ffn2_up=ffn2_up, ffn2_down=ffn2_down)

    x_parts = [x_prompt.reshape(bp * sp, d), x_sample.reshape(bs * ss, d)]
    zeros = lambda *shape: jnp.zeros(shape, F32)
    prompt = dict(row0=0, bsz=bp, seq=sp, chunk_a=math.gcd(sp, 256), chunk_b=math.gcd(sp, CHUNK))
    sample = dict(row0=bp * sp, bsz=bs, seq=ss, chunk_a=math.gcd(ss, 256), chunk_b=math.gcd(ss, CHUNK))

    per_layer = [[], []]
    for l in range(depth):
        lw = _prep_layer_weights(l, w, dims)
        prompt.update(conv=zeros(bp, width - 1, (2 * dk_b + dv_b) * h_b), S=zeros(bp, h_b, dk_b, dv_b),
                      C=zeros(bp, h_a, dk_a, dv_a), n=zeros(bp, h_a, dk_a), m=zeros(bp, h_a))
        sample.update(conv=state_conv[l], S=state_gdn[l], C=state_mlstm_C[l], n=state_mlstm_n[l],
                      m=state_mlstm_m[l])
        x, states = _layer(x_parts, [prompt, sample], lw, dims)
        x_parts = [x]
        per_layer[0].append(states[0])
        per_layer[1].append(states[1])

    part_rows = (bp * sp, bs * ss)
    y_prompt, y_sample = rmsnorm_rows(x, norm_f, part_rows,
                                      bm=_block(math.gcd(*part_rows), 512))
    stack = lambda sts: tuple(jnp.stack(z) for z in zip(*sts))
    return ((y_prompt.reshape(bp, sp, d), y_sample.reshape(bs, ss, d))
            + stack(per_layer[0]) + stack(per_layer[1]))
```

```python
import functools
import math

import jax
import jax.numpy as jnp
from jax import lax
from jax.experimental import pallas as pl
from jax.experimental.pallas import tpu as pltpu

F32 = jnp.float32
BF16 = jnp.bfloat16
HIGHEST = lax.Precision.HIGHEST
EPS = 1e-6
NEG = -1e30
CHUNK = 64
GATE_LANES = 128
V7X_VMEM_BYTES = 64 * 1024 * 1024
VMEM_LIMIT = V7X_VMEM_BYTES - 8 * 1024 * 1024


def _block(n, preferred):
    return math.gcd(n, preferred)


def _params(*semantics):
    return pltpu.CompilerParams(dimension_semantics=semantics, vmem_limit_bytes=VMEM_LIMIT)


def _sigmoid(x):
    return 1.0 / (1.0 + jnp.exp(-x))


def _softplus(x):
    return jnp.maximum(x, 0.0) + jnp.log1p(jnp.exp(-jnp.abs(x)))


def _rms_rows(x, gain):
    return x * lax.rsqrt(jnp.mean(x * x, axis=-1, keepdims=True) + EPS) * gain


def _mm(a, b):
    return jnp.dot(a, b, preferred_element_type=F32)


def _bmm(a, b):
    return jnp.einsum('hij,hjk->hik', a, b, preferred_element_type=F32)


def _bmm_nt(a, b):
    return jnp.einsum('hik,hjk->hij', a, b, preferred_element_type=F32)


def _bmm_tn(a, b):
    return jnp.einsum('hsk,hsv->hkv', a, b, preferred_element_type=F32)


def _split_heads(seqs, heads, width):
    return jnp.stack([a[:, i * width:(i + 1) * width] for a in seqs for i in range(heads)])


def _part_starts(parts, bm):
    starts, total = [], 0
    for p in parts:
        assert p.shape[0] % bm == 0
        starts.append(total)
        total += p.shape[0] // bm
    return tuple(starts), total


def _part_specs(parts, bm, cols, col_of, **spec_kwargs):
    starts, _ = _part_starts(parts, bm)
    specs = []
    for p, s in zip(parts, starts):
        nb = p.shape[0] // bm
        specs.append(pl.BlockSpec(
            (bm, cols), lambda i, *js, s=s, nb=nb: (jnp.clip(i - s, 0, nb - 1), col_of(*js)),
            **spec_kwargs))
    return specs


def _read_parts(refs, starts):
    val = refs[0][...]
    for r, s in zip(refs[1:], starts[1:]):
        val = jnp.where(pl.program_id(0) >= s, r[...], val)
    return val


def _on_owner_part(part_refs, starts, nrow, fn, also=True):
    if len(part_refs) == 1:
        if also is True:
            fn(part_refs[0])
        else:
            pl.when(also)(functools.partial(fn, part_refs[0]))
        return
    i = pl.program_id(0)
    ends = starts[1:] + (nrow,)
    for refs, lo, hi in zip(part_refs, starts, ends):
        pl.when((i >= lo) & (i < hi) & also)(functools.partial(fn, refs))


def _ffn_up_body(*refs, starts, nrow):
    x_refs = refs[:len(starts)]
    g_ref, wg_ref, wu_ref, act_ref, h_ref = refs[len(starts):]

    def norm_rows(x_ref):
        h_ref[...] = _rms_rows(x_ref[...], g_ref[...]).astype(BF16)

    _on_owner_part(x_refs, starts, nrow, norm_rows, also=pl.program_id(1) == 0)
    h = h_ref[...]
    gate = _mm(h, wg_ref[...].astype(BF16))
    up = _mm(h, wu_ref[...].astype(BF16))
    act_ref[...] = (gate * _sigmoid(gate) * up).astype(BF16)


def ffn_up(x_parts, gain, wg, wu, *, bm, bf):
    d, ff = wg.shape
    starts, nrow = _part_starts(x_parts, bm)
    x_mode = dict(pipeline_mode=pl.Buffered(1)) if len(x_parts) > 1 else {}
    return pl.pallas_call(
        functools.partial(_ffn_up_body, starts=starts, nrow=nrow),
        grid=(nrow, ff // bf),
        in_specs=_part_specs(x_parts, bm, d, lambda j: 0, **x_mode) + [
            pl.BlockSpec((1, d), lambda i, j: (0, 0)),
            pl.BlockSpec((d, bf), lambda i, j: (0, j)),
            pl.BlockSpec((d, bf), lambda i, j: (0, j)),
        ],
        out_specs=pl.BlockSpec((bm, bf), lambda i, j: (i, j)),
        out_shape=jax.ShapeDtypeStruct((nrow * bm, ff), BF16),
        scratch_shapes=[pltpu.VMEM((bm, d), BF16)],
        compiler_params=_params("parallel", "arbitrary"),
        name="ffn_up",
    )(*x_parts, gain.reshape(1, d), wg, wu)


def _mm_res_body(a_ref, w_ref, *refs, scale, starts):
    x_refs, o_ref = refs[:-1], refs[-1]
    o_ref[...] = _read_parts(x_refs, starts) + scale * _mm(a_ref[...], w_ref[...])


def matmul_residual(a, w, x_parts, *, scale, bm, bn):
    m, k = a.shape
    n = w.shape[1]
    starts, nrow = _part_starts(x_parts, bm)
    assert nrow * bm == m
    return pl.pallas_call(
        functools.partial(_mm_res_body, scale=scale, starts=starts),
        grid=(nrow, n // bn),
        in_specs=[
            pl.BlockSpec((bm, k), lambda i, j: (i, 0)),
            pl.BlockSpec((k, bn), lambda i, j: (0, j)),
        ] + _part_specs(x_parts, bm, bn, lambda j: j),
        out_specs=pl.BlockSpec((bm, bn), lambda i, j: (i, j)),
        out_shape=jax.ShapeDtypeStruct((m, n), F32),
        compiler_params=_params("parallel", "arbitrary"),
        name="matmul_residual",
    )(a, w, *x_parts)


def _in_proj_body(x_ref, g_ref, w_ref, ws_ref, p_ref, ps_ref, h_ref):
    @pl.when(pl.program_id(1) == 0)
    def _():
        h = _rms_rows(x_ref[...], g_ref[...])
        h_hi = h.astype(BF16)
        h_ref[...] = h_hi
        h_lo = (h - h_hi.astype(F32)).astype(BF16)
        both = _mm(h_hi, ws_ref[...])
        ps_ref[...] = (both[:, :GATE_LANES] + both[:, GATE_LANES:]
                       + _mm(h_lo, ws_ref[:, :GATE_LANES]))

    p_ref[...] = _mm(h_ref[...], w_ref[...])


def in_proj(x, gain, w_big, w_small, *, bm, bn):
    m, d = x.shape
    n = w_big.shape[1]
    ws_hi = w_small.astype(BF16)
    ws_lo = (w_small - ws_hi.astype(F32)).astype(BF16)
    ws_split = jnp.concatenate([ws_hi, ws_lo], axis=1)
    return pl.pallas_call(
        _in_proj_body,
        grid=(m // bm, n // bn),
        in_specs=[
            pl.BlockSpec((bm, d), lambda i, j: (i, 0)),
            pl.BlockSpec((1, d), lambda i, j: (0, 0)),
            pl.BlockSpec((d, bn), lambda i, j: (0, j)),
            pl.BlockSpec((d, 2 * GATE_LANES), lambda i, j: (0, 0)),
        ],
        out_specs=[
            pl.BlockSpec((bm, bn), lambda i, j: (i, j)),
            pl.BlockSpec((bm, GATE_LANES), lambda i, j: (i, 0)),
        ],
        out_shape=[
            jax.ShapeDtypeStruct((m, n), F32),
            jax.ShapeDtypeStruct((m, GATE_LANES), F32),
        ],
        scratch_shapes=[pltpu.VMEM((bm, d), BF16)],
        compiler_params=_params("parallel", "arbitrary"),
        name="in_proj",
    )(x, gain.reshape(1, d), w_big, ws_split)


def _merge_body(*refs, starts):
    n = len(starts)
    ha_refs, ob_refs = refs[:n], refs[n:2 * n]
    wa_ref, wb_ref, ga_ref, gb_ref, o_ref = refs[2 * n:]
    ya = _mm(_read_parts(ha_refs, starts), wa_ref[...])
    yb = _mm(_read_parts(ob_refs, starts), wb_ref[...])
    o_ref[...] = (_sigmoid(ga_ref[...]) * ya + _sigmoid(gb_ref[...]) * yb).astype(BF16)


def merge(ha_parts, ob_parts, wa, wb, p_big, *, ga_off, gb_off, bm, bn):
    ka, n = wa.shape
    kb = wb.shape[0]
    assert ga_off % bn == 0 and gb_off % bn == 0
    ga_blk, gb_blk = ga_off // bn, gb_off // bn
    starts, nrow = _part_starts(ha_parts, bm)
    assert _part_starts(ob_parts, bm) == (starts, nrow) and nrow * bm == p_big.shape[0]
    return pl.pallas_call(
        functools.partial(_merge_body, starts=starts),
        grid=(nrow, n // bn),
        in_specs=_part_specs(ha_parts, bm, ka, lambda j: 0) + _part_specs(ob_parts, bm, kb, lambda j: 0) + [
            pl.BlockSpec((ka, bn), lambda i, j: (0, j)),
            pl.BlockSpec((kb, bn), lambda i, j: (0, j)),
            pl.BlockSpec((bm, bn), lambda i, j: (i, ga_blk + j)),
            pl.BlockSpec((bm, bn), lambda i, j: (i, gb_blk + j)),
        ],
        out_specs=pl.BlockSpec((bm, bn), lambda i, j: (i, j)),
        out_shape=jax.ShapeDtypeStruct((nrow * bm, n), BF16),
        compiler_params=_params("parallel", "arbitrary"),
        name="merge",
    )(*ha_parts, *ob_parts, wa, wb, p_big, p_big)


def _rmsnorm_body(x_ref, g_ref, *o_refs, starts, nrow):
    y = _rms_rows(x_ref[...], g_ref[...])

    def store(o_ref):
        o_ref[...] = y

    _on_owner_part(o_refs, starts, nrow, store)


def rmsnorm_rows(x, gain, part_rows, *, bm):
    m, d = x.shape
    parts = [jax.ShapeDtypeStruct((r, d), F32) for r in part_rows]
    starts, nrow = _part_starts(parts, bm)
    assert nrow * bm == m
    return pl.pallas_call(
        functools.partial(_rmsnorm_body, starts=starts, nrow=nrow),
        grid=(nrow,),
        in_specs=[pl.BlockSpec((bm, d), lambda i: (i, 0)), pl.BlockSpec((1, d), lambda i: (0, 0))],
        out_specs=_part_specs(parts, bm, d, lambda: 0),
        out_shape=parts,
        compiler_params=_params("arbitrary"),
        name="rmsnorm_rows",
    )(x, gain.reshape(1, d))


def _mlstm_body(q_ref, k_ref, v_ref, ao_ref, gc_ref, gbias_ref, norm_ref, c0_ref, n0_ref, m0_ref,
                ha_ref, c_ref, n_ref, m_ref, *, heads, dk, dv, chunk, bb, single_chunk):
    L = chunk
    if single_chunk:
        c_src, n_src, m_src = c0_ref, n0_ref, m0_ref
    else:
        c_src, n_src, m_src = c_ref, n_ref, m_ref

        @pl.when(pl.program_id(1) == 0)
        def _():
            c_ref[...] = c0_ref[...]
            n_ref[...] = n0_ref[...]
            m_ref[...] = m0_ref[...]

    row = lax.broadcasted_iota(jnp.int32, (L, L), 0)
    col = lax.broadcasted_iota(jnp.int32, (L, L), 1)
    causal = (row >= col)[None]
    tri = jnp.where(row >= col, 1.0, 0.0).astype(F32)
    pairs = [(bi, h) for bi in range(bb) for h in range(heads)]

    ig_cols, ig_rows, b_cols, b_rows = [], [], [], []
    for bi in range(bb):
        gates = gc_ref[bi * L:(bi + 1) * L, :] + gbias_ref[...]
        log_f = jnp.minimum(gates, 0.0) - jnp.log1p(jnp.exp(-jnp.abs(gates)))
        b_all = jnp.dot(tri, log_f, precision=HIGHEST, preferred_element_type=F32)
        gates_t, b_all_t = gates.T, b_all.T
        ig_cols += [gates[:, h:h + 1] for h in range(heads)]
        ig_rows += [gates_t[h:h + 1, :] for h in range(heads)]
        b_cols += [b_all[:, heads + h:heads + h + 1] for h in range(heads)]
        b_rows += [b_all_t[heads + h:heads + h + 1, :] for h in range(heads)]
    ig_c, ig_r = jnp.stack(ig_cols), jnp.stack(ig_rows)
    b_c, b_r = jnp.stack(b_cols), jnp.stack(b_rows)
    m_prev = jnp.stack([m_src[bi][0:1, h:h + 1] for bi, h in pairs])

    log_d = jnp.where(causal, b_c - b_r + ig_r, NEG)
    m_inter = m_prev + b_c
    m_t = jnp.maximum(m_inter, jnp.max(log_d, axis=-1, keepdims=True))
    d_mat = jnp.exp(log_d - m_t)
    inter = jnp.exp(m_inter - m_t)

    seq_rows = lambda ref: [ref[bi * L:(bi + 1) * L, :] for bi in range(bb)]
    q = _split_heads(seq_rows(q_ref), heads, dk)
    k = _split_heads(seq_rows(k_ref), heads, dk) * (dk ** -0.5)
    v = _split_heads(seq_rows(v_ref), heads, dv)
    qb, kb, vb = q.astype(BF16), k.astype(BF16), v.astype(BF16)
    c_old = c_src[...].reshape(bb * heads, dk, dv)
    n_old = jnp.stack([n_src[bi, h:h + 1, :] for bi, h in pairs])

    s = _bmm_nt(qb, kb) * d_mat
    num = inter * _bmm(qb, c_old.astype(BF16)) + _bmm(s.astype(BF16), vb)
    den = inter * jnp.sum(q * n_old, axis=-1, keepdims=True) + jnp.sum(s, axis=-1, keepdims=True)
    h_out = num / jnp.maximum(jnp.abs(den), jnp.exp(-m_t))

    m_new = m_t[:, L - 1:L, :]
    b_last = b_c[:, L - 1:L, :]
    carry = jnp.exp(m_prev + b_last - m_new)
    kw = k * jnp.exp(b_last - b_c + ig_c - m_new)
    c_new = carry * c_old + _bmm_tn(kw.astype(BF16), vb)
    c_ref[...] = c_new.reshape(bb, heads, dk, dv)
    n_new = carry * n_old + jnp.sum(kw, axis=1, keepdims=True)

    gain = jnp.stack([norm_ref[:, h * dv:(h + 1) * dv] for _, h in pairs])
    gate_o = _sigmoid(_split_heads(seq_rows(ao_ref), heads, dv))
    ha = (_rms_rows(h_out, gain) * gate_o).astype(BF16)

    lane = lax.broadcasted_iota(jnp.int32, (8, GATE_LANES), 1)
    for bi in range(bb):
        m_next = m_src[bi]
        for h in range(heads):
            i = bi * heads + h
            n_ref[bi, h:h + 1, :] = n_new[i]
            m_next = jnp.where(lane == h, m_new[i], m_next)
            ha_ref[bi * L:(bi + 1) * L, h * dv:(h + 1) * dv] = ha[i]
        m_ref[bi] = m_next


def mlstm(p_big, p_small, gbias, norm_row, c0, n0, m0, *, row0, seq, chunk, heads, dk, dv,
          q_off, k_off, v_off, ao_off):
    bsz = c0.shape[0]
    nc = seq // chunk
    bb = math.gcd(bsz, 4) if nc == 1 else 1
    blk = bb * chunk
    qk_w, v_w = heads * dk, heads * dv
    assert q_off % qk_w == 0 and k_off % qk_w == 0 and v_off % v_w == 0 and ao_off % v_w == 0
    assert row0 % blk == 0
    rows = lambda b, c: row0 // blk + b * nc + c
    body = functools.partial(_mlstm_body, heads=heads, dk=dk, dv=dv, chunk=chunk, bb=bb,
                             single_chunk=(nc == 1))
    return pl.pallas_call(
        body,
        grid=(bsz // bb, nc),
        in_specs=[
            pl.BlockSpec((blk, qk_w), lambda b, c: (rows(b, c), q_off // qk_w)),
            pl.BlockSpec((blk, qk_w), lambda b, c: (rows(b, c), k_off // qk_w)),
            pl.BlockSpec((blk, v_w), lambda b, c: (rows(b, c), v_off // v_w)),
            pl.BlockSpec((blk, v_w), lambda b, c: (rows(b, c), ao_off // v_w)),
            pl.BlockSpec((blk, GATE_LANES), lambda b, c: (rows(b, c), 0)),
            pl.BlockSpec((1, GATE_LANES), lambda b, c: (0, 0)),
            pl.BlockSpec((1, v_w), lambda b, c: (0, 0)),
            pl.BlockSpec((bb, heads, dk, dv), lambda b, c: (b, 0, 0, 0)),
            pl.BlockSpec((bb, heads, dk), lambda b, c: (b, 0, 0)),
            pl.BlockSpec((bb, 8, GATE_LANES), lambda b, c: (b, 0, 0)),
        ],
        out_specs=[
            pl.BlockSpec((blk, v_w), lambda b, c: (b * nc + c, 0)),
            pl.BlockSpec((bb, heads, dk, dv), lambda b, c: (b, 0, 0, 0)),
            pl.BlockSpec((bb, heads, dk), lambda b, c: (b, 0, 0)),
            pl.BlockSpec((bb, 8, GATE_LANES), lambda b, c: (b, 0, 0)),
        ],
        out_shape=[
            jax.ShapeDtypeStruct((bsz * seq, v_w), BF16),
            jax.ShapeDtypeStruct(c0.shape, F32),
            jax.ShapeDtypeStruct(n0.shape, F32),
            jax.ShapeDtypeStruct(m0.shape, F32),
        ],
        compiler_params=_params("parallel", "arbitrary"),
        name="mlstm",
    )(p_big, p_big, p_big, p_big, p_small, gbias, norm_row, c0, n0, m0)


def _shift_rows(x, prev, j):
    xs = pltpu.roll(x, j, axis=0)
    row = lax.broadcasted_iota(jnp.int32, prev.shape, 0)
    head = jnp.where(row < j, pltpu.roll(prev, j, axis=0), xs[0:8])
    if x.shape[0] == 8:
        return head
    return jnp.concatenate([head, xs[8:]], axis=0)


def _conv_silu(x_ref, w_ref, tail_src, tail_ref, width):
    outs = []
    for bi in range(tail_ref.shape[0]):
        L = x_ref.shape[0] // tail_ref.shape[0]
        x = x_ref[bi * L:(bi + 1) * L, :]
        prev = tail_src[bi]
        acc = x * w_ref[width - 1:width, :]
        for j in range(1, width):
            acc = acc + _shift_rows(x, prev, j) * w_ref[width - 1 - j:width - j, :]
        tail_ref[bi] = x[L - 8:, :]
        outs.append(acc * _sigmoid(acc))
    return outs


def _gdn_body(xq_ref, xk_ref, xv_ref, z_ref, wq_ref, wk_ref, wv_ref, tq0_ref, tk0_ref, tv0_ref,
              gc_ref, gbias_ref, alog_ref, norm_ref, s0_ref,
              ob_ref, s_ref, tq_ref, tk_ref, tv_ref, *, heads, dk, dv, chunk, width, gl_col, beta_col,
              single_chunk):
    L = chunk
    bb = s_ref.shape[0]
    if single_chunk:
        s_src, tq_src, tk_src, tv_src = s0_ref, tq0_ref, tk0_ref, tv0_ref
    else:
        s_src, tq_src, tk_src, tv_src = s_ref, tq_ref, tk_ref, tv_ref

        @pl.when(pl.program_id(1) == 0)
        def _():
            s_ref[...] = s0_ref[...]
            tq_ref[...] = tq0_ref[...]
            tk_ref[...] = tk0_ref[...]
            tv_ref[...] = tv0_ref[...]

    q = _split_heads(_conv_silu(xq_ref, wq_ref, tq_src, tq_ref, width), heads, dk)
    k = _split_heads(_conv_silu(xk_ref, wk_ref, tk_src, tk_ref, width), heads, dk)
    v = _split_heads(_conv_silu(xv_ref, wv_ref, tv_src, tv_ref, width), heads, dv)
    q = q * lax.rsqrt(jnp.sum(q * q, axis=-1, keepdims=True) + EPS) * (dk ** -0.5)
    k = k * lax.rsqrt(jnp.sum(k * k, axis=-1, keepdims=True) + EPS)

    row = lax.broadcasted_iota(jnp.int32, (L, L), 0)
    col = lax.broadcasted_iota(jnp.int32, (L, L), 1)
    causal = (row >= col)[None]
    strict = (row > col)[None]
    tri = jnp.where(row >= col, 1.0, 0.0).astype(F32)

    g_cols, g_rows, beta_cols = [], [], []
    for bi in range(bb):
        raw = gc_ref[bi * L:(bi + 1) * L, :] + gbias_ref[...]
        gl_all = -jnp.exp(alog_ref[...]) * _softplus(raw)
        g_all = jnp.dot(tri, gl_all, precision=HIGHEST, preferred_element_type=F32)
        g_all_t = g_all.T
        beta_all = _sigmoid(raw)
        g_cols += [g_all[:, gl_col + i:gl_col + i + 1] for i in range(heads)]
        g_rows += [g_all_t[gl_col + i:gl_col + i + 1, :] for i in range(heads)]
        beta_cols += [beta_all[:, beta_col + i:beta_col + i + 1] for i in range(heads)]
    g_c, g_r, beta_c = jnp.stack(g_cols), jnp.stack(g_rows), jnp.stack(beta_cols)
    eg_c = jnp.exp(g_c)
    decay = jnp.exp(jnp.where(causal, g_c - g_r, NEG))

    kb = k.astype(BF16)
    both = _bmm_nt(jnp.concatenate([k, q], axis=1).astype(BF16), kb)
    kk, qk = both[:, :L], both[:, L:]

    x_pow = jnp.where(strict, -(beta_c * kk * decay), 0.0)
    n_inv = x_pow
    for _ in range(max(int(math.ceil(math.log2(L))) - 1, 0)):
        xb = x_pow.astype(BF16)
        x_pow = _bmm(xb, xb)
        n_inv = n_inv + x_pow + _bmm(n_inv.astype(BF16), x_pow.astype(BF16))
    rhs = jnp.concatenate([v * beta_c, k * (beta_c * eg_c)], axis=-1)
    sol = rhs + _bmm(n_inv.astype(BF16), rhs.astype(BF16))
    u, w = sol[..., :dv], sol[..., dv:]

    s_old = s_src[...].reshape(bb * heads, dk, dv)
    sb = s_old.astype(BF16)
    ws_qs = _bmm(jnp.concatenate([w, q], axis=1).astype(BF16), sb)
    delta = u - ws_qs[:, :L]
    db = delta.astype(BF16)
    o = eg_c * ws_qs[:, L:] + _bmm((qk * decay).astype(BF16), db)
    g_last = g_c[:, L - 1:L, :]
    kd = k * jnp.exp(g_last - g_c)
    s_new = jnp.exp(g_last) * s_old + _bmm_tn(kd.astype(BF16), db)
    s_ref[...] = s_new.reshape(bb, heads, dk, dv)

    o = _rms_rows(o, norm_ref[...])
    for bi in range(bb):
        for i in range(heads):
            z = z_ref[bi * L:(bi + 1) * L, i * dv:(i + 1) * dv]
            ob_ref[bi * L:(bi + 1) * L, i * dv:(i + 1) * dv] = (
                o[bi * heads + i] * (z * _sigmoid(z))).astype(BF16)


def gdn(p_big, p_small, gbias, alog_row, norm_row, conv_w, tails0, s0, *, row0, seq, chunk,
        q_off, k_off, v_off, z_off, gl_col, beta_col):
    bsz, heads, dk, dv = s0.shape
    assert dk == dv
    width = conv_w[0].shape[0]
    nc = seq // chunk
    bb = math.gcd(bsz, 4) if nc == 1 else 1
    blk = bb * chunk
    hw = heads * dk
    assert all(off % hw == 0 for off in (q_off, k_off, v_off, z_off)) and row0 % blk == 0
    rows = lambda b, c: row0 // blk + b * nc + c
    x_spec = lambda off: pl.BlockSpec((blk, hw), lambda b, c: (rows(b, c), off // hw))
    w_spec = pl.BlockSpec((width, hw), lambda b, c: (0, 0))
    tail_spec = pl.BlockSpec((bb, 8, hw), lambda b, c: (b, 0, 0))
    s_spec = pl.BlockSpec((bb, heads, dk, dv), lambda b, c: (b, 0, 0, 0))
    vec_spec = lambda n: pl.BlockSpec((1, n), lambda b, c: (0, 0))
    body = functools.partial(_gdn_body, heads=heads, dk=dk, dv=dv, chunk=chunk, width=width,
                             gl_col=gl_col, beta_col=beta_col, single_chunk=(nc == 1))
    tail_shape = jax.ShapeDtypeStruct(tails0[0].shape, F32)
    return pl.pallas_call(
        body,
        grid=(bsz // bb, nc),
        in_specs=[
            x_spec(q_off), x_spec(k_off), x_spec(v_off), x_spec(z_off),
            w_spec, w_spec, w_spec, tail_spec, tail_spec, tail_spec,
            pl.BlockSpec((blk, GATE_LANES), lambda b, c: (rows(b, c), 0)),
            vec_spec(GATE_LANES), vec_spec(GATE_LANES), vec_spec(dv), s_spec,
        ],
        out_specs=[
            pl.BlockSpec((blk, hw), lambda b, c: (b * nc + c, 0)),
            s_spec, tail_spec, tail_spec, tail_spec,
        ],
        out_shape=[
            jax.ShapeDtypeStruct((bsz * seq, heads * dv), BF16),
            jax.ShapeDtypeStruct(s0.shape, F32),
            tail_shape, tail_shape, tail_shape,
        ],
        compiler_params=_params("parallel", "arbitrary"),
        name="gdn",
    )(p_big, p_big, p_big, p_big, conv_w[0], conv_w[1], conv_w[2], tails0[0], tails0[1], tails0[2],
      p_small, gbias, alog_row, norm_row, s0)


def _pad_lanes(vec, offset):
    return jnp.zeros((1, GATE_LANES), F32).at[0, offset:offset + vec.shape[0]].set(vec.astype(F32))


def _layer(x_parts, streams, lw, dims):
    h_a, dk_a, dv_a, h_b, dk_b, dv_b, width = dims
    a_qk, a_v, b_k, b_v = h_a * dk_a, h_a * dv_a, h_b * dk_b, h_b * dv_b
    d = x_parts[0].shape[1]
    ff = lw["ffn1_gate"].shape[1]
    bm = 1024
    for p in x_parts:
        bm = _block(p.shape[0], bm)
    bf, bd, bd_down, bd_out = _block(ff, 512), _block(d, 512), _block(d, 512), _block(d, 1024)

    act = ffn_up(x_parts, lw["norm1"], lw["ffn1_gate"], lw["ffn1_up"], bm=bm, bf=bf)
    x = matmul_residual(act, lw["ffn1_down"], x_parts, scale=0.5, bm=bm, bn=bd_down)

    offs = dict(q=0, k=a_qk, v=2 * a_qk, ao=2 * a_qk + a_v)
    offs["bq"] = offs["ao"] + a_v
    offs["bk"] = offs["bq"] + b_k
    offs["bv"] = offs["bk"] + b_k
    offs["bz"] = offs["bv"] + b_v
    offs["ga"] = offs["bz"] + b_v
    offs["gb"] = offs["ga"] + d
    p_big, p_small = in_proj(x, lw["norm2"], lw["w_big"], lw["w_small"], bm=bm,
                             bn=_block(lw["w_big"].shape[1], 1024))

    gl_col, beta_col = 2 * h_a, 2 * h_a + h_b
    gbias = (_pad_lanes(lw["mlstm_b_i"], 0) + _pad_lanes(lw["mlstm_b_f"], h_a)
             + _pad_lanes(lw["gdn_dt_bias"], gl_col))
    alog_row = _pad_lanes(lw["gdn_A_log"], gl_col)
    conv_w = [lw["gdn_conv"][:, i * b_k:(i + 1) * b_k] for i in range(3)]

    ha_parts, ob_parts, new_states = [], [], []
    for st in streams:
        bsz = st["bsz"]
        m0 = jnp.zeros((bsz, 8, GATE_LANES), F32).at[:, :, :h_a].set(
            jnp.broadcast_to(st["m"][:, None, :], (bsz, 8, h_a)))
        ha, c1, n1, m1 = mlstm(
            p_big, p_small, gbias, lw["mlstm_norm"].reshape(1, a_v), st["C"], st["n"], m0,
            row0=st["row0"], seq=st["seq"], chunk=st["chunk_a"], heads=h_a, dk=dk_a, dv=dv_a,
            q_off=offs["q"], k_off=offs["k"], v_off=offs["v"], ao_off=offs["ao"])
        tails0 = [jnp.pad(st["conv"][:, :, i * b_k:(i + 1) * b_k], ((0, 0), (8 - (width - 1), 0), (0, 0)))
                  for i in range(3)]
        ob, s1, tq, tk, tv = gdn(
            p_big, p_small, gbias, alog_row, lw["gdn_norm"].reshape(1, dv_b), conv_w, tails0, st["S"],
            row0=st["row0"], seq=st["seq"], chunk=st["chunk_b"],
            q_off=offs["bq"], k_off=offs["bk"], v_off=offs["bv"], z_off=offs["bz"],
            gl_col=gl_col, beta_col=beta_col)
        conv1 = jnp.concatenate([tq, tk, tv], axis=-1)[:, 8 - (width - 1):, :]
        ha_parts.append(ha)
        ob_parts.append(ob)
        new_states.append((conv1, s1, c1, n1, m1[:, 0, :h_a]))

    merged = merge(ha_parts, ob_parts, lw["w_branch_a"], lw["w_branch_b"], p_big,
                   ga_off=offs["ga"], gb_off=offs["gb"], bm=bm, bn=bd)
    x = matmul_residual(merged, lw["w_out"], [x], scale=1.0, bm=bm, bn=bd_out)

    act = ffn_up([x], lw["norm3"], lw["ffn2_gate"], lw["ffn2_up"], bm=bm, bf=bf)
    x = matmul_residual(act, lw["ffn2_down"], [x], scale=0.5, bm=bm, bn=bd_down)
    return x, new_states


def _split_w_in_body(wt_ref, big_ref):
    big_ref[...] = wt_ref[...].T.astype(BF16)


def split_w_in(w_in, dims):
    h_a, dk_a, dv_a, h_b, dk_b, dv_b, _ = dims
    a_qk, a_v, b_k, b_v = h_a * dk_a, h_a * dv_a, h_b * dk_b, h_b * dv_b
    d, d_in = w_in.shape
    runs = [(2 * a_qk + a_v, True), (2 * h_a, False), (a_v + 2 * b_k + b_v, True), (2 * h_b, False),
            (b_v + 2 * d, True)]
    assert sum(r for r, _ in runs) == d_in
    big_segs, small_segs, src, big_dst = [], [], 0, 0
    for width, is_big in runs:
        if is_big:
            big_segs.append((src, width, big_dst))
            big_dst += width
        else:
            small_segs.append((src, width))
        src += width
    rb = 1024
    for s, width, _ in big_segs:
        assert s % 8 == 0
        rb = _block(width, rb)
    w_t = jnp.swapaxes(w_in, 0, 1)

    def src_row(i):
        row = i * rb
        for s, _, dst in big_segs[1:]:
            row = jnp.where(i * rb >= dst, i * rb + (s - dst), row)
        return pl.multiple_of(row, 8)

    w_big = pl.pallas_call(
        _split_w_in_body,
        grid=(big_dst // rb,),
        in_specs=[pl.BlockSpec((pl.Element(rb), pl.Element(d)), lambda i: (src_row(i), 0))],
        out_specs=pl.BlockSpec((d, rb), lambda i: (0, i)),
        out_shape=jax.ShapeDtypeStruct((d, big_dst), BF16),
        compiler_params=_params("parallel"),
        name="split_w_in",
    )(w_t)
    small = jnp.concatenate([w_t[s:s + width] for s, width in small_segs], axis=0)
    assert small.shape[0] <= GATE_LANES
    w_small = jnp.pad(small, ((0, GATE_LANES - small.shape[0]), (0, 0))).T
    return w_big, w_small


def _prep_layer_weights(l, w, dims):
    lw = {k: w[k][l] for k in w if k != "w_in"}
    for k in ("ffn1_down", "ffn2_down", "w_branch_a", "w_branch_b", "w_out"):
        lw[k] = lw[k].astype(BF16)
    lw["w_big"], lw["w_small"] = split_w_in(w["w_in"][l], dims)
    return lw


def kernel(x_prompt, x_sample, state_conv, state_gdn, state_mlstm_C, state_mlstm_n, state_mlstm_m,
           norm1, ffn1_gate, ffn1_up, ffn1_down, norm2, w_in, mlstm_b_i, mlstm_b_f, mlstm_norm,
           gdn_conv, gdn_A_log, gdn_dt_bias, gdn_norm, w_branch_a, w_branch_b, w_out,
           norm3, ffn2_gate, ffn2_up, ffn2_down, norm_f):
    depth = norm1.shape[0]
    bp, sp, d = x_prompt.shape
    bs, ss, _ = x_sample.shape
    _, _, h_a, dk_a, dv_a = state_mlstm_C.shape
    _, _, h_b, dk_b, dv_b = state_gdn.shape
    width = gdn_conv.shape[1]
    dims = (h_a, dk_a, dv_a, h_b, dk_b, dv_b, width)
    w = dict(norm1=norm1, ffn1_gate=ffn1_gate, ffn1_up=ffn1_up, ffn1_down=ffn1_down, norm2=norm2,
             w_in=w_in, mlstm_b_i=mlstm_b_i, mlstm_b_f=mlstm_b_f, mlstm_norm=mlstm_norm,
             gdn_conv=gdn_conv, gdn_A_log=gdn_A_log, gdn_dt_bias=gdn_dt_bias, gdn_norm=gdn_norm,
             w_branch_a=w_branch_a, w_branch_b=w_branch_b, w_out=w_out, norm3=norm3,
             ffn2_gate=ffn2_gate, ffn2_up=ffn2_up, ffn2_down=ffn2_down)

    x_parts = [x_prompt.reshape(bp * sp, d), x_sample.reshape(bs * ss, d)]
    zeros = lambda *shape: jnp.zeros(shape, F32)
    prompt = dict(row0=0, bsz=bp, seq=sp, chunk_a=math.gcd(sp, 256), chunk_b=math.gcd(sp, CHUNK))
    sample = dict(row0=bp * sp, bsz=bs, seq=ss, chunk_a=math.gcd(ss, 256), chunk_b=math.gcd(ss, CHUNK))

    per_layer = [[], []]
    for l in range(depth):
        lw = _prep_layer_weights(l, w, dims)
        prompt.update(conv=zeros(bp, width - 1, (2 * dk_b + dv_b) * h_b), S=zeros(bp, h_b, dk_b, dv_b),
                      C=zeros(bp, h_a, dk_a, dv_a), n=zeros(bp, h_a, dk_a), m=zeros(bp, h_a))
        sample.update(conv=state_conv[l], S=state_gdn[l], C=state_mlstm_C[l], n=state_mlstm_n[l],
                      m=state_mlstm_m[l])
        x, states = _layer(x_parts, [prompt, sample], lw, dims)
        x_parts = [x]
        per_layer[0].append(states[0])
        per_layer[1].append(states[1])

    part_rows = (bp * sp, bs * ss)
    y_prompt, y_sample = rmsnorm_rows(x, norm_f, part_rows,
                                      bm=_block(math.gcd(*part_rows), 512))
    stack = lambda sts: tuple(jnp.stack(z) for z in zip(*sts))
    return ((y_prompt.reshape(bp, sp, d), y_sample.reshape(bs, ss, d))
            + stack(per_layer[0]) + stack(per_layer[1]))
```

```python
import functools
import math

import jax
import jax.numpy as jnp
from jax import lax
from jax.experimental import pallas as pl
from jax.experimental.pallas import tpu as pltpu

F32 = jnp.float32
BF16 = jnp.bfloat16
HIGHEST = lax.Precision.HIGHEST
EPS = 1e-6
NEG = -1e30
CHUNK = 64
GATE_LANES = 128
V7X_VMEM_BYTES = 64 * 1024 * 1024
VMEM_LIMIT = V7X_VMEM_BYTES - 8 * 1024 * 1024


def _block(n, preferred):
    return math.gcd(n, preferred)


def _params(*semantics):
    return pltpu.CompilerParams(dimension_semantics=semantics, vmem_limit_bytes=VMEM_LIMIT)


def _sigmoid(x):
    return 1.0 / (1.0 + jnp.exp(-x))


def _softplus(x):
    return jnp.maximum(x, 0.0) + jnp.log1p(jnp.exp(-jnp.abs(x)))


def _rms_rows(x, gain):
    return x * lax.rsqrt(jnp.mean(x * x, axis=-1, keepdims=True) + EPS) * gain


def _mm(a, b):
    return jnp.dot(a, b, preferred_element_type=F32)


def _bmm(a, b):
    return jnp.einsum('hij,hjk->hik', a, b, preferred_element_type=F32)


def _bmm_nt(a, b):
    return jnp.einsum('hik,hjk->hij', a, b, preferred_element_type=F32)


def _bmm_tn(a, b):
    return jnp.einsum('hsk,hsv->hkv', a, b, preferred_element_type=F32)


def _split_heads(seqs, heads, width):
    return jnp.stack([a[:, i * width:(i + 1) * width] for a in seqs for i in range(heads)])


def _part_starts(parts, bm):
    starts, total = [], 0
    for p in parts:
        assert p.shape[0] % bm == 0
        starts.append(total)
        total += p.shape[0] // bm
    return tuple(starts), total


def _part_specs(parts, bm, cols, col_of, **spec_kwargs):
    starts, _ = _part_starts(parts, bm)
    specs = []
    for p, s in zip(parts, starts):
        nb = p.shape[0] // bm
        specs.append(pl.BlockSpec(
            (bm, cols), lambda i, *js, s=s, nb=nb: (jnp.clip(i - s, 0, nb - 1), col_of(*js)),
            **spec_kwargs))
    return specs


def _read_parts(refs, starts):
    val = refs[0][...]
    for r, s in zip(refs[1:], starts[1:]):
        val = jnp.where(pl.program_id(0) >= s, r[...], val)
    return val


def _on_owner_part(part_refs, starts, nrow, fn, also=True):
    if len(part_refs) == 1:
        if also is True:
            fn(part_refs[0])
        else:
            pl.when(also)(functools.partial(fn, part_refs[0]))
        return
    i = pl.program_id(0)
    ends = starts[1:] + (nrow,)
    for refs, lo, hi in zip(part_refs, starts, ends):
        pl.when((i >= lo) & (i < hi) & also)(functools.partial(fn, refs))


def _ffn_up_body(*refs, starts, nrow):
    x_refs = refs[:len(starts)]
    g_ref, wg_ref, wu_ref, act_ref, h_ref = refs[len(starts):]

    def norm_rows(x_ref):
        h_ref[...] = _rms_rows(x_ref[...], g_ref[...]).astype(BF16)

    _on_owner_part(x_refs, starts, nrow, norm_rows, also=pl.program_id(1) == 0)
    h = h_ref[...]
    gate = _mm(h, wg_ref[...].astype(BF16))
    up = _mm(h, wu_ref[...].astype(BF16))
    act_ref[...] = (gate * _sigmoid(gate) * up).astype(BF16)


def ffn_up(x_parts, gain, wg, wu, *, bm, bf):
    d, ff = wg.shape
    starts, nrow = _part_starts(x_parts, bm)
    x_mode = dict(pipeline_mode=pl.Buffered(1)) if len(x_parts) > 1 else {}
    return pl.pallas_call(
        functools.partial(_ffn_up_body, starts=starts, nrow=nrow),
        grid=(nrow, ff // bf),
        in_specs=_part_specs(x_parts, bm, d, lambda j: 0, **x_mode) + [
            pl.BlockSpec((1, d), lambda i, j: (0, 0)),
            pl.BlockSpec((d, bf), lambda i, j: (0, j)),
            pl.BlockSpec((d, bf), lambda i, j: (0, j)),
        ],
        out_specs=pl.BlockSpec((bm, bf), lambda i, j: (i, j)),
        out_shape=jax.ShapeDtypeStruct((nrow * bm, ff), BF16),
        scratch_shapes=[pltpu.VMEM((bm, d), BF16)],
        compiler_params=_params("parallel", "arbitrary"),
        name="ffn_up",
    )(*x_parts, gain.reshape(1, d), wg, wu)


def _mm_res_body(a_ref, w_ref, *refs, scale, starts):
    x_refs, o_ref = refs[:-1], refs[-1]
    o_ref[...] = _read_parts(x_refs, starts) + scale * _mm(a_ref[...], w_ref[...])


def matmul_residual(a, w, x_parts, *, scale, bm, bn):
    m, k = a.shape
    n = w.shape[1]
    starts, nrow = _part_starts(x_parts, bm)
    assert nrow * bm == m
    return pl.pallas_call(
        functools.partial(_mm_res_body, scale=scale, starts=starts),
        grid=(nrow, n // bn),
        in_specs=[
            pl.BlockSpec((bm, k), lambda i, j: (i, 0)),
            pl.BlockSpec((k, bn), lambda i, j: (0, j)),
        ] + _part_specs(x_parts, bm, bn, lambda j: j),
        out_specs=pl.BlockSpec((bm, bn), lambda i, j: (i, j)),
        out_shape=jax.ShapeDtypeStruct((m, n), F32),
        compiler_params=_params("parallel", "arbitrary"),
        name="matmul_residual",
    )(a, w, *x_parts)


def _in_proj_body(x_ref, g_ref, w_ref, ws_ref, p_ref, ps_ref, h_ref):
    @pl.when(pl.program_id(1) == 0)
    def _():
        h = _rms_rows(x_ref[...], g_ref[...])
        h_hi = h.astype(BF16)
        h_ref[...] = h_hi
        h_lo = (h - h_hi.astype(F32)).astype(BF16)
        both = _mm(h_hi, ws_ref[...])
        ps_ref[...] = (both[:, :GATE_LANES] + both[:, GATE_LANES:]
                       + _mm(h_lo, ws_ref[:, :GATE_LANES]))

    p_ref[...] = _mm(h_ref[...], w_ref[...])


def in_proj(x, gain, w_big, w_small, *, bm, bn):
    m, d = x.shape
    n = w_big.shape[1]
    ws_hi = w_small.astype(BF16)
    ws_lo = (w_small - ws_hi.astype(F32)).astype(BF16)
    ws_split = jnp.concatenate([ws_hi, ws_lo], axis=1)
    return pl.pallas_call(
        _in_proj_body,
        grid=(m // bm, n // bn),
        in_specs=[
            pl.BlockSpec((bm, d), lambda i, j: (i, 0)),
            pl.BlockSpec((1, d), lambda i, j: (0, 0)),
            pl.BlockSpec((d, bn), lambda i, j: (0, j)),
            pl.BlockSpec((d, 2 * GATE_LANES), lambda i, j: (0, 0)),
        ],
        out_specs=[
            pl.BlockSpec((bm, bn), lambda i, j: (i, j)),
            pl.BlockSpec((bm, GATE_LANES), lambda i, j: (i, 0)),
        ],
        out_shape=[
            jax.ShapeDtypeStruct((m, n), F32),
            jax.ShapeDtypeStruct((m, GATE_LANES), F32),
        ],
        scratch_shapes=[pltpu.VMEM((bm, d), BF16)],
        compiler_params=_params("parallel", "arbitrary"),
        name="in_proj",
    )(x, gain.reshape(1, d), w_big, ws_split)


def _merge_body(*refs, starts):
    n = len(starts)
    ha_refs, ob_refs = refs[:n], refs[n:2 * n]
    wa_ref, wb_ref, ga_ref, gb_ref, o_ref = refs[2 * n:]
    ya = _mm(_read_parts(ha_refs, starts), wa_ref[...])
    yb = _mm(_read_parts(ob_refs, starts), wb_ref[...])
    o_ref[...] = (_sigmoid(ga_ref[...]) * ya + _sigmoid(gb_ref[...]) * yb).astype(BF16)


def merge(ha_parts, ob_parts, wa, wb, p_big, *, ga_off, gb_off, bm, bn):
    ka, n = wa.shape
    kb = wb.shape[0]
    assert ga_off % bn == 0 and gb_off % bn == 0
    ga_blk, gb_blk = ga_off // bn, gb_off // bn
    starts, nrow = _part_starts(ha_parts, bm)
    assert _part_starts(ob_parts, bm) == (starts, nrow) and nrow * bm == p_big.shape[0]
    return pl.pallas_call(
        functools.partial(_merge_body, starts=starts),
        grid=(nrow, n // bn),
        in_specs=_part_specs(ha_parts, bm, ka, lambda j: 0) + _part_specs(ob_parts, bm, kb, lambda j: 0) + [
            pl.BlockSpec((ka, bn), lambda i, j: (0, j)),
            pl.BlockSpec((kb, bn), lambda i, j: (0, j)),
            pl.BlockSpec((bm, bn), lambda i, j: (i, ga_blk + j)),
            pl.BlockSpec((bm, bn), lambda i, j: (i, gb_blk + j)),
        ],
        out_specs=pl.BlockSpec((bm, bn), lambda i, j: (i, j)),
        out_shape=jax.ShapeDtypeStruct((nrow * bm, n), BF16),
        compiler_params=_params("parallel", "arbitrary"),
        name="merge",
    )(*ha_parts, *ob_parts, wa, wb, p_big, p_big)


def _rmsnorm_body(x_ref, g_ref, *o_refs, starts, nrow):
    y = _rms_rows(x_ref[...], g_ref[...])

    def store(o_ref):
        o_ref[...] = y

    _on_owner_part(o_refs, starts, nrow, store)


def rmsnorm_rows(x, gain, part_rows, *, bm):
    m, d = x.shape
    parts = [jax.ShapeDtypeStruct((r, d), F32) for r in part_rows]
    starts, nrow = _part_starts(parts, bm)
    assert nrow * bm == m
    return pl.pallas_call(
        functools.partial(_rmsnorm_body, starts=starts, nrow=nrow),
        grid=(nrow,),
        in_specs=[pl.BlockSpec((bm, d), lambda i: (i, 0)), pl.BlockSpec((1, d), lambda i: (0, 0))],
        out_specs=_part_specs(parts, bm, d, lambda: 0),
        out_shape=parts,
        compiler_params=_params("arbitrary"),
        name="rmsnorm_rows",
    )(x, gain.reshape(1, d))


def _mlstm_body(q_ref, k_ref, v_ref, ao_ref, gc_ref, gbias_ref, norm_ref, c0_ref, n0_ref, m0_ref,
                ha_ref, c_ref, n_ref, m_ref, *, heads, dk, dv, chunk, bb, single_chunk):
    L = chunk
    if single_chunk:
        c_src, n_src, m_src = c0_ref, n0_ref, m0_ref
    else:
        c_src, n_src, m_src = c_ref, n_ref, m_ref

        @pl.when(pl.program_id(1) == 0)
        def _():
            c_ref[...] = c0_ref[...]
            n_ref[...] = n0_ref[...]
            m_ref[...] = m0_ref[...]

    row = lax.broadcasted_iota(jnp.int32, (L, L), 0)
    col = lax.broadcasted_iota(jnp.int32, (L, L), 1)
    causal = (row >= col)[None]
    tri = jnp.where(row >= col, 1.0, 0.0).astype(F32)
    pairs = [(bi, h) for bi in range(bb) for h in range(heads)]

    ig_cols, ig_rows, b_cols, b_rows = [], [], [], []
    for bi in range(bb):
        gates = gc_ref[bi * L:(bi + 1) * L, :] + gbias_ref[...]
        log_f = jnp.minimum(gates, 0.0) - jnp.log1p(jnp.exp(-jnp.abs(gates)))
        b_all = jnp.dot(tri, log_f, precision=HIGHEST, preferred_element_type=F32)
        gates_t, b_all_t = gates.T, b_all.T
        ig_cols += [gates[:, h:h + 1] for h in range(heads)]
        ig_rows += [gates_t[h:h + 1, :] for h in range(heads)]
        b_cols += [b_all[:, heads + h:heads + h + 1] for h in range(heads)]
        b_rows += [b_all_t[heads + h:heads + h + 1, :] for h in range(heads)]
    ig_c, ig_r = jnp.stack(ig_cols), jnp.stack(ig_rows)
    b_c, b_r = jnp.stack(b_cols), jnp.stack(b_rows)
    m_prev = jnp.stack([m_src[bi][0:1, h:h + 1] for bi, h in pairs])

    log_d = jnp.where(causal, b_c - b_r + ig_r, NEG)
    m_inter = m_prev + b_c
    m_t = jnp.maximum(m_inter, jnp.max(log_d, axis=-1, keepdims=True))
    d_mat = jnp.exp(log_d - m_t)
    inter = jnp.exp(m_inter - m_t)

    seq_rows = lambda ref: [ref[bi * L:(bi + 1) * L, :] for bi in range(bb)]
    q = _split_heads(seq_rows(q_ref), heads, dk)
    k = _split_heads(seq_rows(k_ref), heads, dk) * (dk ** -0.5)
    v = _split_heads(seq_rows(v_ref), heads, dv)
    qb, kb, vb = q.astype(BF16), k.astype(BF16), v.astype(BF16)
    c_old = c_src[...].reshape(bb * heads, dk, dv)
    n_old = jnp.stack([n_src[bi, h:h + 1, :] for bi, h in pairs])

    s = _bmm_nt(qb, kb) * d_mat
    num = inter * _bmm(qb, c_old.astype(BF16)) + _bmm(s.astype(BF16), vb)
    den = inter * jnp.sum(q * n_old, axis=-1, keepdims=True) + jnp.sum(s, axis=-1, keepdims=True)
    h_out = num / jnp.maximum(jnp.abs(den), jnp.exp(-m_t))

    m_new = m_t[:, L - 1:L, :]
    b_last = b_c[:, L - 1:L, :]
    carry = jnp.exp(m_prev + b_last - m_new)
    kw = k * jnp.exp(b_last - b_c + ig_c - m_new)
    c_new = carry * c_old + _bmm_tn(kw.astype(BF16), vb)
    c_ref[...] = c_new.reshape(bb, heads, dk, dv)
    n_new = carry * n_old + jnp.sum(kw, axis=1, keepdims=True)

    gain = jnp.stack([norm_ref[:, h * dv:(h + 1) * dv] for _, h in pairs])
    gate_o = _sigmoid(_split_heads(seq_rows(ao_ref), heads, dv))
    ha = (_rms_rows(h_out, gain) * gate_o).astype(BF16)

    lane = lax.broadcasted_iota(jnp.int32, (8, GATE_LANES), 1)
    for bi in range(bb):
        m_next = m_src[bi]
        for h in range(heads):
            i = bi * heads + h
            n_ref[bi, h:h + 1, :] = n_new[i]
            m_next = jnp.where(lane == h, m_new[i], m_next)
            ha_ref[bi * L:(bi + 1) * L, h * dv:(h + 1) * dv] = ha[i]
        m_ref[bi] = m_next


def mlstm(p_big, p_small, gbias, norm_row, c0, n0, m0, *, row0, seq, chunk, heads, dk, dv,
          q_off, k_off, v_off, ao_off):
    bsz = c0.shape[0]
    nc = seq // chunk
    bb = math.gcd(bsz, 4) if nc == 1 else 1
    blk = bb * chunk
    qk_w, v_w = heads * dk, heads * dv
    assert q_off % qk_w == 0 and k_off % qk_w == 0 and v_off % v_w == 0 and ao_off % v_w == 0
    assert row0 % blk == 0
    rows = lambda b, c: row0 // blk + b * nc + c
    body = functools.partial(_mlstm_body, heads=heads, dk=dk, dv=dv, chunk=chunk, bb=bb,
                             single_chunk=(nc == 1))
    return pl.pallas_call(
        body,
        grid=(bsz // bb, nc),
        in_specs=[
            pl.BlockSpec((blk, qk_w), lambda b, c: (rows(b, c), q_off // qk_w)),
            pl.BlockSpec((blk, qk_w), lambda b, c: (rows(b, c), k_off // qk_w)),
            pl.BlockSpec((blk, v_w), lambda b, c: (rows(b, c), v_off // v_w)),
            pl.BlockSpec((blk, v_w), lambda b, c: (rows(b, c), ao_off // v_w)),
            pl.BlockSpec((blk, GATE_LANES), lambda b, c: (rows(b, c), 0)),
            pl.BlockSpec((1, GATE_LANES), lambda b, c: (0, 0)),
            pl.BlockSpec((1, v_w), lambda b, c: (0, 0)),
            pl.BlockSpec((bb, heads, dk, dv), lambda b, c: (b, 0, 0, 0)),
            pl.BlockSpec((bb, heads, dk), lambda b, c: (b, 0, 0)),
            pl.BlockSpec((bb, 8, GATE_LANES), lambda b, c: (b, 0, 0)),
        ],
        out_specs=[
            pl.BlockSpec((blk, v_w), lambda b, c: (b * nc + c, 0)),
            pl.BlockSpec((bb, heads, dk, dv), lambda b, c: (b, 0, 0, 0)),
            pl.BlockSpec((bb, heads, dk), lambda b, c: (b, 0, 0)),
            pl.BlockSpec((bb, 8, GATE_LANES), lambda b, c: (b, 0, 0)),
        ],
        out_shape=[
            jax.ShapeDtypeStruct((bsz * seq, v_w), BF16),
            jax.ShapeDtypeStruct(c0.shape, F32),
            jax.ShapeDtypeStruct(n0.shape, F32),
            jax.ShapeDtypeStruct(m0.shape, F32),
        ],
        compiler_params=_params("parallel", "arbitrary"),
        name="mlstm",
    )(p_big, p_big, p_big, p_big, p_small, gbias, norm_row, c0, n0, m0)


def _shift_rows(x, prev, j):
    xs = pltpu.roll(x, j, axis=0)
    row = lax.broadcasted_iota(jnp.int32, prev.shape, 0)
    head = jnp.where(row < j, pltpu.roll(prev, j, axis=0), xs[0:8])
    if x.shape[0] == 8:
        return head
    return jnp.concatenate([head, xs[8:]], axis=0)


def _conv_silu(x_ref, w_ref, tail_src, tail_ref, width):
    outs = []
    for bi in range(tail_ref.shape[0]):
        L = x_ref.shape[0] // tail_ref.shape[0]
        x = x_ref[bi * L:(bi + 1) * L, :]
        prev = tail_src[bi]
        acc = x * w_ref[width - 1:width, :]
        for j in range(1, width):
            acc = acc + _shift_rows(x, prev, j) * w_ref[width - 1 - j:width - j, :]
        tail_ref[bi] = x[L - 8:, :]
        outs.append(acc * _sigmoid(acc))
    return outs


def _gdn_body(x_ref, z_ref, w_ref, t0_ref, gc_ref, gbias_ref, alog_ref, norm_ref, s0_ref,
              ob_ref, s_ref, t_ref, *, heads, dk, dv, chunk, width, gl_col, beta_col, single_chunk):
    L = chunk
    bb = s_ref.shape[0]
    if single_chunk:
        s_src, t_src = s0_ref, t0_ref
    else:
        s_src, t_src = s_ref, t_ref

        @pl.when(pl.program_id(1) == 0)
        def _():
            s_ref[...] = s0_ref[...]
            t_ref[...] = t0_ref[...]

    hw = heads * dk
    qkv = _conv_silu(x_ref, w_ref, t_src, t_ref, width)
    q = _split_heads([c[:, :hw] for c in qkv], heads, dk)
    k = _split_heads([c[:, hw:2 * hw] for c in qkv], heads, dk)
    v = _split_heads([c[:, 2 * hw:] for c in qkv], heads, dv)
    q = q * lax.rsqrt(jnp.sum(q * q, axis=-1, keepdims=True) + EPS) * (dk ** -0.5)
    k = k * lax.rsqrt(jnp.sum(k * k, axis=-1, keepdims=True) + EPS)

    row = lax.broadcasted_iota(jnp.int32, (L, L), 0)
    col = lax.broadcasted_iota(jnp.int32, (L, L), 1)
    causal = (row >= col)[None]
    strict = (row > col)[None]
    tri = jnp.where(row >= col, 1.0, 0.0).astype(F32)

    g_cols, g_rows, beta_cols = [], [], []
    for bi in range(bb):
        raw = gc_ref[bi * L:(bi + 1) * L, :] + gbias_ref[...]
        gl_all = -jnp.exp(alog_ref[...]) * _softplus(raw)
        g_all = jnp.dot(tri, gl_all, precision=HIGHEST, preferred_element_type=F32)
        g_all_t = g_all.T
        beta_all = _sigmoid(raw)
        g_cols += [g_all[:, gl_col + i:gl_col + i + 1] for i in range(heads)]
        g_rows += [g_all_t[gl_col + i:gl_col + i + 1, :] for i in range(heads)]
        beta_cols += [beta_all[:, beta_col + i:beta_col + i + 1] for i in range(heads)]
    g_c, g_r, beta_c = jnp.stack(g_cols), jnp.stack(g_rows), jnp.stack(beta_cols)
    eg_c = jnp.exp(g_c)
    decay = jnp.exp(jnp.where(causal, g_c - g_r, NEG))

    kb = k.astype(BF16)
    both = _bmm_nt(jnp.concatenate([k, q], axis=1).astype(BF16), kb)
    kk, qk = both[:, :L], both[:, L:]

    x_pow = jnp.where(strict, -(beta_c * kk * decay), 0.0)
    n_inv = x_pow
    for _ in range(max(int(math.ceil(math.log2(L))) - 1, 0)):
        xb = x_pow.astype(BF16)
        x_pow = _bmm(xb, xb)
        n_inv = n_inv + x_pow + _bmm(n_inv.astype(BF16), x_pow.astype(BF16))
    rhs = jnp.concatenate([v * beta_c, k * (beta_c * eg_c)], axis=-1)
    sol = rhs + _bmm(n_inv.astype(BF16), rhs.astype(BF16))
    u, w = sol[..., :dv], sol[..., dv:]

    s_old = s_src[...].reshape(bb * heads, dk, dv)
    sb = s_old.astype(BF16)
    ws_qs = _bmm(jnp.concatenate([w, q], axis=1).astype(BF16), sb)
    delta = u - ws_qs[:, :L]
    db = delta.astype(BF16)
    o = eg_c * ws_qs[:, L:] + _bmm((qk * decay).astype(BF16), db)
    g_last = g_c[:, L - 1:L, :]
    kd = k * jnp.exp(g_last - g_c)
    s_new = jnp.exp(g_last) * s_old + _bmm_tn(kd.astype(BF16), db)
    s_ref[...] = s_new.reshape(bb, heads, dk, dv)

    o = _rms_rows(o, norm_ref[...])
    for bi in range(bb):
        for i in range(heads):
            z = z_ref[bi * L:(bi + 1) * L, i * dv:(i + 1) * dv]
            ob_ref[bi * L:(bi + 1) * L, i * dv:(i + 1) * dv] = (
                o[bi * heads + i] * (z * _sigmoid(z))).astype(BF16)


def gdn(p_big, p_small, gbias, alog_row, norm_row, conv_w, tail0, s0, *, row0, seq, chunk,
        qkv_off, z_off, gl_col, beta_col):
    bsz, heads, dk, dv = s0.shape
    assert dk == dv
    width = conv_w.shape[0]
    nc = seq // chunk
    bb = math.gcd(bsz, 4) if nc == 1 else 1
    blk = bb * chunk
    hw = heads * dk
    assert qkv_off % (3 * hw) == 0 and z_off % hw == 0 and row0 % blk == 0
    rows = lambda b, c: row0 // blk + b * nc + c
    tail_spec = pl.BlockSpec((bb, 8, 3 * hw), lambda b, c: (b, 0, 0))
    s_spec = pl.BlockSpec((bb, heads, dk, dv), lambda b, c: (b, 0, 0, 0))
    vec_spec = lambda n: pl.BlockSpec((1, n), lambda b, c: (0, 0))
    body = functools.partial(_gdn_body, heads=heads, dk=dk, dv=dv, chunk=chunk, width=width,
                             gl_col=gl_col, beta_col=beta_col, single_chunk=(nc == 1))
    return pl.pallas_call(
        body,
        grid=(bsz // bb, nc),
        in_specs=[
            pl.BlockSpec((blk, 3 * hw), lambda b, c: (rows(b, c), qkv_off // (3 * hw))),
            pl.BlockSpec((blk, hw), lambda b, c: (rows(b, c), z_off // hw)),
            pl.BlockSpec((width, 3 * hw), lambda b, c: (0, 0)),
            tail_spec,
            pl.BlockSpec((blk, GATE_LANES), lambda b, c: (rows(b, c), 0)),
            vec_spec(GATE_LANES), vec_spec(GATE_LANES), vec_spec(dv), s_spec,
        ],
        out_specs=[pl.BlockSpec((blk, hw), lambda b, c: (b * nc + c, 0)), s_spec, tail_spec],
        out_shape=[
            jax.ShapeDtypeStruct((bsz * seq, heads * dv), BF16),
            jax.ShapeDtypeStruct(s0.shape, F32),
            jax.ShapeDtypeStruct(tail0.shape, F32),
        ],
        compiler_params=_params("parallel", "arbitrary"),
        name="gdn",
    )(p_big, p_big, conv_w, tail0, p_small, gbias, alog_row, norm_row, s0)


def _pad_lanes(vec, offset):
    return jnp.zeros((1, GATE_LANES), F32).at[0, offset:offset + vec.shape[0]].set(vec.astype(F32))


def _layer(x_parts, streams, lw, dims):
    h_a, dk_a, dv_a, h_b, dk_b, dv_b, width = dims
    a_qk, a_v, b_k, b_v = h_a * dk_a, h_a * dv_a, h_b * dk_b, h_b * dv_b
    d = x_parts[0].shape[1]
    ff = lw["ffn1_gate"].shape[1]
    bm = 1024
    for p in x_parts:
        bm = _block(p.shape[0], bm)
    bf, bd, bd_down, bd_out = _block(ff, 512), _block(d, 512), _block(d, 512), _block(d, 1024)

    act = ffn_up(x_parts, lw["norm1"], lw["ffn1_gate"], lw["ffn1_up"], bm=bm, bf=bf)
    x = matmul_residual(act, lw["ffn1_down"], x_parts, scale=0.5, bm=bm, bn=bd_down)

    offs = dict(q=0, k=a_qk, v=2 * a_qk, ao=2 * a_qk + a_v)
    offs["bq"] = offs["ao"] + a_v
    offs["bk"] = offs["bq"] + b_k
    offs["bv"] = offs["bk"] + b_k
    offs["bz"] = offs["bv"] + b_v
    offs["ga"] = offs["bz"] + b_v
    offs["gb"] = offs["ga"] + d
    p_big, p_small = in_proj(x, lw["norm2"], lw["w_big"], lw["w_small"], bm=bm,
                             bn=_block(lw["w_big"].shape[1], 1024))

    gl_col, beta_col = 2 * h_a, 2 * h_a + h_b
    gbias = (_pad_lanes(lw["mlstm_b_i"], 0) + _pad_lanes(lw["mlstm_b_f"], h_a)
             + _pad_lanes(lw["gdn_dt_bias"], gl_col))
    alog_row = _pad_lanes(lw["gdn_A_log"], gl_col)

    ha_parts, ob_parts, new_states = [], [], []
    for st in streams:
        bsz = st["bsz"]
        m0 = jnp.zeros((bsz, 8, GATE_LANES), F32).at[:, :, :h_a].set(
            jnp.broadcast_to(st["m"][:, None, :], (bsz, 8, h_a)))
        ha, c1, n1, m1 = mlstm(
            p_big, p_small, gbias, lw["mlstm_norm"].reshape(1, a_v), st["C"], st["n"], m0,
            row0=st["row0"], seq=st["seq"], chunk=st["chunk_a"], heads=h_a, dk=dk_a, dv=dv_a,
            q_off=offs["q"], k_off=offs["k"], v_off=offs["v"], ao_off=offs["ao"])
        tail0 = jnp.pad(st["conv"], ((0, 0), (8 - (width - 1), 0), (0, 0)))
        ob, s1, tail1 = gdn(
            p_big, p_small, gbias, alog_row, lw["gdn_norm"].reshape(1, dv_b), lw["gdn_conv"], tail0,
            st["S"], row0=st["row0"], seq=st["seq"], chunk=st["chunk_b"],
            qkv_off=offs["bq"], z_off=offs["bz"], gl_col=gl_col, beta_col=beta_col)
        conv1 = tail1[:, 8 - (width - 1):, :]
        ha_parts.append(ha)
        ob_parts.append(ob)
        new_states.append((conv1, s1, c1, n1, m1[:, 0, :h_a]))

    merged = merge(ha_parts, ob_parts, lw["w_branch_a"], lw["w_branch_b"], p_big,
                   ga_off=offs["ga"], gb_off=offs["gb"], bm=bm, bn=bd)
    x = matmul_residual(merged, lw["w_out"], [x], scale=1.0, bm=bm, bn=bd_out)

    act = ffn_up([x], lw["norm3"], lw["ffn2_gate"], lw["ffn2_up"], bm=bm, bf=bf)
    x = matmul_residual(act, lw["ffn2_down"], [x], scale=0.5, bm=bm, bn=bd_down)
    return x, new_states


def _split_w_in_body(wt_ref, big_ref):
    big_ref[...] = wt_ref[...].T.astype(BF16)


def split_w_in(w_in, dims):
    h_a, dk_a, dv_a, h_b, dk_b, dv_b, _ = dims
    a_qk, a_v, b_k, b_v = h_a * dk_a, h_a * dv_a, h_b * dk_b, h_b * dv_b
    d, d_in = w_in.shape
    runs = [(2 * a_qk + a_v, True), (2 * h_a, False), (a_v + 2 * b_k + b_v, True), (2 * h_b, False),
            (b_v + 2 * d, True)]
    assert sum(r for r, _ in runs) == d_in
    big_segs, small_segs, src, big_dst = [], [], 0, 0
    for width, is_big in runs:
        if is_big:
            big_segs.append((src, width, big_dst))
            big_dst += width
        else:
            small_segs.append((src, width))
        src += width
    rb = 1024
    for s, width, _ in big_segs:
        assert s % 8 == 0
        rb = _block(width, rb)
    w_t = jnp.swapaxes(w_in, 0, 1)

    def src_row(i):
        row = i * rb
        for s, _, dst in big_segs[1:]:
            row = jnp.where(i * rb >= dst, i * rb + (s - dst), row)
        return pl.multiple_of(row, 8)

    w_big = pl.pallas_call(
        _split_w_in_body,
        grid=(big_dst // rb,),
        in_specs=[pl.BlockSpec((pl.Element(rb), pl.Element(d)), lambda i: (src_row(i), 0))],
        out_specs=pl.BlockSpec((d, rb), lambda i: (0, i)),
        out_shape=jax.ShapeDtypeStruct((d, big_dst), BF16),
        compiler_params=_params("parallel"),
        name="split_w_in",
    )(w_t)
    small = jnp.concatenate([w_t[s:s + width] for s, width in small_segs], axis=0)
    assert small.shape[0] <= GATE_LANES
    w_small = jnp.pad(small, ((0, GATE_LANES - small.shape[0]), (0, 0))).T
    return w_big, w_small


def _prep_layer_weights(l, w, dims):
    lw = {k: w[k][l] for k in w if k != "w_in"}
    for k in ("ffn1_down", "ffn2_down", "w_branch_a", "w_branch_b", "w_out"):
        lw[k] = lw[k].astype(BF16)
    lw["w_big"], lw["w_small"] = split_w_in(w["w_in"][l], dims)
    return lw


def kernel(x_prompt, x_sample, state_conv, state_gdn, state_mlstm_C, state_mlstm_n, state_mlstm_m,
           norm1, ffn1_gate, ffn1_up, ffn1_down, norm2, w_in, mlstm_b_i, mlstm_b_f, mlstm_norm,
           gdn_conv, gdn_A_log, gdn_dt_bias, gdn_norm, w_branch_a, w_branch_b, w_out,
           norm3, ffn2_gate, ffn2_up, ffn2_down, norm_f):
    depth = norm1.shape[0]
    bp, sp, d = x_prompt.shape
    bs, ss, _ = x_sample.shape
    _, _, h_a, dk_a, dv_a = state_mlstm_C.shape
    _, _, h_b, dk_b, dv_b = state_gdn.shape
    width = gdn_conv.shape[1]
    dims = (h_a, dk_a, dv_a, h_b, dk_b, dv_b, width)
    w = dict(norm1=norm1, ffn1_gate=ffn1_gate, ffn1_up=ffn1_up, ffn1_down=ffn1_down, norm2=norm2,
             w_in=w_in, mlstm_b_i=mlstm_b_i, mlstm_b_f=mlstm_b_f, mlstm_norm=mlstm_norm,
             gdn_conv=gdn_conv, gdn_A_log=gdn_A_log, gdn_dt_bias=gdn_dt_bias, gdn_norm=gdn_norm,
             w_branch_a=w_branch_a, w_branch_b=w_branch_b, w_out=w_out, norm3=norm3,
             ffn2_gate=ffn2_gate, ffn2_up=ffn2_up, ffn2_down=ffn2_down)

    x_parts = [x_prompt.reshape(bp * sp, d), x_sample.reshape(bs * ss, d)]
    zeros = lambda *shape: jnp.zeros(shape, F32)
    prompt = dict(row0=0, bsz=bp, seq=sp, chunk_a=math.gcd(sp, 256), chunk_b=math.gcd(sp, CHUNK))
    sample = dict(row0=bp * sp, bsz=bs, seq=ss, chunk_a=math.gcd(ss, 256), chunk_b=math.gcd(ss, CHUNK))

    per_layer = [[], []]
    for l in range(depth):
        lw = _prep_layer_weights(l, w, dims)
        prompt.update(conv=zeros(bp, width - 1, (2 * dk_b + dv_b) * h_b), S=zeros(bp, h_b, dk_b, dv_b),
                      C=zeros(bp, h_a, dk_a, dv_a), n=zeros(bp, h_a, dk_a), m=zeros(bp, h_a))
        sample.update(conv=state_conv[l], S=state_gdn[l], C=state_mlstm_C[l], n=state_mlstm_n[l],
                      m=state_mlstm_m[l])
        x, states = _layer(x_parts, [prompt, sample], lw, dims)
        x_parts = [x]
        per_layer[0].append(states[0])
        per_layer[1].append(states[1])

    part_rows = (bp * sp, bs * ss)
    y_prompt, y_sample = rmsnorm_rows(x, norm_f, part_rows,
                                      bm=_block(math.gcd(*part_rows), 512))
    stack = lambda sts: tuple(jnp.stack(z) for z in zip(*sts))
    return ((y_prompt.reshape(bp, sp, d), y_sample.reshape(bs, ss, d))
            + stack(per_layer[0]) + stack(per_layer[1]))
```

```python
import functools
import math

import jax
import jax.numpy as jnp
from jax import lax
from jax.experimental import pallas as pl
from jax.experimental.pallas import tpu as pltpu

F32 = jnp.float32
BF16 = jnp.bfloat16
HIGHEST = lax.Precision.HIGHEST
EPS = 1e-6
NEG = -1e30
CHUNK = 64
GATE_LANES = 128
V7X_VMEM_BYTES = 64 * 1024 * 1024
VMEM_LIMIT = V7X_VMEM_BYTES - 8 * 1024 * 1024


def _block(n, preferred):
    return math.gcd(n, preferred)


def _params(*semantics):
    return pltpu.CompilerParams(dimension_semantics=semantics, vmem_limit_bytes=VMEM_LIMIT)


def _sigmoid(x):
    return 1.0 / (1.0 + jnp.exp(-x))


def _softplus(x):
    return jnp.maximum(x, 0.0) + jnp.log1p(jnp.exp(-jnp.abs(x)))


def _rms_rows(x, gain):
    return x * lax.rsqrt(jnp.mean(x * x, axis=-1, keepdims=True) + EPS) * gain


def _mm(a, b):
    return jnp.dot(a, b, preferred_element_type=F32)


def _bmm(a, b):
    return jnp.einsum('hij,hjk->hik', a, b, preferred_element_type=F32)


def _bmm_nt(a, b):
    return jnp.einsum('hik,hjk->hij', a, b, preferred_element_type=F32)


def _bmm_tn(a, b):
    return jnp.einsum('hsk,hsv->hkv', a, b, preferred_element_type=F32)


def _split_heads(seqs, heads, width):
    return jnp.stack([a[:, i * width:(i + 1) * width] for a in seqs for i in range(heads)])


def _part_starts(parts, bm):
    starts, total = [], 0
    for p in parts:
        assert p.shape[0] % bm == 0
        starts.append(total)
        total += p.shape[0] // bm
    return tuple(starts), total


def _part_specs(parts, bm, cols, col_of, **spec_kwargs):
    starts, _ = _part_starts(parts, bm)
    specs = []
    for p, s in zip(parts, starts):
        nb = p.shape[0] // bm
        specs.append(pl.BlockSpec(
            (bm, cols), lambda i, *js, s=s, nb=nb: (jnp.clip(i - s, 0, nb - 1), col_of(*js)),
            **spec_kwargs))
    return specs


def _read_parts(refs, starts):
    val = refs[0][...]
    for r, s in zip(refs[1:], starts[1:]):
        val = jnp.where(pl.program_id(0) >= s, r[...], val)
    return val


def _on_owner_part(part_refs, starts, nrow, fn, also=True):
    if len(part_refs) == 1:
        if also is True:
            fn(part_refs[0])
        else:
            pl.when(also)(functools.partial(fn, part_refs[0]))
        return
    i = pl.program_id(0)
    ends = starts[1:] + (nrow,)
    for refs, lo, hi in zip(part_refs, starts, ends):
        pl.when((i >= lo) & (i < hi) & also)(functools.partial(fn, refs))


def _ffn_up_body(*refs, starts, nrow):
    x_refs = refs[:len(starts)]
    g_ref, wg_ref, wu_ref, act_ref, h_ref = refs[len(starts):]

    def norm_rows(x_ref):
        h_ref[...] = _rms_rows(x_ref[...], g_ref[...]).astype(BF16)

    _on_owner_part(x_refs, starts, nrow, norm_rows, also=pl.program_id(1) == 0)
    h = h_ref[...]
    gate = _mm(h, wg_ref[...].astype(BF16))
    up = _mm(h, wu_ref[...].astype(BF16))
    act_ref[...] = (gate * _sigmoid(gate) * up).astype(BF16)


def ffn_up(x_parts, gain, wg, wu, *, bm, bf):
    d, ff = wg.shape
    starts, nrow = _part_starts(x_parts, bm)
    x_mode = dict(pipeline_mode=pl.Buffered(1)) if len(x_parts) > 1 else {}
    return pl.pallas_call(
        functools.partial(_ffn_up_body, starts=starts, nrow=nrow),
        grid=(nrow, ff // bf),
        in_specs=_part_specs(x_parts, bm, d, lambda j: 0, **x_mode) + [
            pl.BlockSpec((1, d), lambda i, j: (0, 0)),
            pl.BlockSpec((d, bf), lambda i, j: (0, j)),
            pl.BlockSpec((d, bf), lambda i, j: (0, j)),
        ],
        out_specs=pl.BlockSpec((bm, bf), lambda i, j: (i, j)),
        out_shape=jax.ShapeDtypeStruct((nrow * bm, ff), BF16),
        scratch_shapes=[pltpu.VMEM((bm, d), BF16)],
        compiler_params=_params("parallel", "arbitrary"),
        name="ffn_up",
    )(*x_parts, gain.reshape(1, d), wg, wu)


def _mm_res_body(a_ref, w_ref, *refs, scale, starts):
    x_refs, o_ref = refs[:-1], refs[-1]
    o_ref[...] = _read_parts(x_refs, starts) + scale * _mm(a_ref[...], w_ref[...])


def matmul_residual(a, w, x_parts, *, scale, bm, bn):
    m, k = a.shape
    n = w.shape[1]
    starts, nrow = _part_starts(x_parts, bm)
    assert nrow * bm == m
    return pl.pallas_call(
        functools.partial(_mm_res_body, scale=scale, starts=starts),
        grid=(nrow, n // bn),
        in_specs=[
            pl.BlockSpec((bm, k), lambda i, j: (i, 0)),
            pl.BlockSpec((k, bn), lambda i, j: (0, j)),
        ] + _part_specs(x_parts, bm, bn, lambda j: j),
        out_specs=pl.BlockSpec((bm, bn), lambda i, j: (i, j)),
        out_shape=jax.ShapeDtypeStruct((m, n), F32),
        compiler_params=_params("parallel", "arbitrary"),
        name="matmul_residual",
    )(a, w, *x_parts)


def _in_proj_body(x_ref, g_ref, w_ref, ws_ref, p_ref, ps_ref, h_ref):
    @pl.when(pl.program_id(1) == 0)
    def _():
        h = _rms_rows(x_ref[...], g_ref[...])
        h_hi = h.astype(BF16)
        h_ref[...] = h_hi
        h_lo = (h - h_hi.astype(F32)).astype(BF16)
        both = _mm(h_hi, ws_ref[...])
        ps_ref[...] = (both[:, :GATE_LANES] + both[:, GATE_LANES:]
                       + _mm(h_lo, ws_ref[:, :GATE_LANES]))

    p_ref[...] = _mm(h_ref[...], w_ref[...])


def in_proj(x, gain, w_big, w_small, *, bm, bn):
    m, d = x.shape
    n = w_big.shape[1]
    ws_hi = w_small.astype(BF16)
    ws_lo = (w_small - ws_hi.astype(F32)).astype(BF16)
    ws_split = jnp.concatenate([ws_hi, ws_lo], axis=1)
    return pl.pallas_call(
        _in_proj_body,
        grid=(m // bm, n // bn),
        in_specs=[
            pl.BlockSpec((bm, d), lambda i, j: (i, 0)),
            pl.BlockSpec((1, d), lambda i, j: (0, 0)),
            pl.BlockSpec((d, bn), lambda i, j: (0, j)),
            pl.BlockSpec((d, 2 * GATE_LANES), lambda i, j: (0, 0)),
        ],
        out_specs=[
            pl.BlockSpec((bm, bn), lambda i, j: (i, j)),
            pl.BlockSpec((bm, GATE_LANES), lambda i, j: (i, 0)),
        ],
        out_shape=[
            jax.ShapeDtypeStruct((m, n), F32),
            jax.ShapeDtypeStruct((m, GATE_LANES), F32),
        ],
        scratch_shapes=[pltpu.VMEM((bm, d), BF16)],
        compiler_params=_params("parallel", "arbitrary"),
        name="in_proj",
    )(x, gain.reshape(1, d), w_big, ws_split)


def _merge_body(*refs, starts):
    n = len(starts)
    ha_refs, ob_refs = refs[:n], refs[n:2 * n]
    wa_ref, wb_ref, ga_ref, gb_ref, o_ref = refs[2 * n:]
    ya = _mm(_read_parts(ha_refs, starts), wa_ref[...])
    yb = _mm(_read_parts(ob_refs, starts), wb_ref[...])
    o_ref[...] = (_sigmoid(ga_ref[...]) * ya + _sigmoid(gb_ref[...]) * yb).astype(BF16)


def merge(ha_parts, ob_parts, wa, wb, p_big, *, ga_off, gb_off, bm, bn):
    ka, n = wa.shape
    kb = wb.shape[0]
    assert ga_off % bn == 0 and gb_off % bn == 0
    ga_blk, gb_blk = ga_off // bn, gb_off // bn
    starts, nrow = _part_starts(ha_parts, bm)
    assert _part_starts(ob_parts, bm) == (starts, nrow) and nrow * bm == p_big.shape[0]
    return pl.pallas_call(
        functools.partial(_merge_body, starts=starts),
        grid=(nrow, n // bn),
        in_specs=_part_specs(ha_parts, bm, ka, lambda j: 0) + _part_specs(ob_parts, bm, kb, lambda j: 0) + [
            pl.BlockSpec((ka, bn), lambda i, j: (0, j)),
            pl.BlockSpec((kb, bn), lambda i, j: (0, j)),
            pl.BlockSpec((bm, bn), lambda i, j: (i, ga_blk + j)),
            pl.BlockSpec((bm, bn), lambda i, j: (i, gb_blk + j)),
        ],
        out_specs=pl.BlockSpec((bm, bn), lambda i, j: (i, j)),
        out_shape=jax.ShapeDtypeStruct((nrow * bm, n), BF16),
        compiler_params=_params("parallel", "arbitrary"),
        name="merge",
    )(*ha_parts, *ob_parts, wa, wb, p_big, p_big)


def _rmsnorm_body(x_ref, g_ref, *o_refs, starts, nrow):
    y = _rms_rows(x_ref[...], g_ref[...])

    def store(o_ref):
        o_ref[...] = y

    _on_owner_part(o_refs, starts, nrow, store)


def rmsnorm_rows(x, gain, part_rows, *, bm):
    m, d = x.shape
    parts = [jax.ShapeDtypeStruct((r, d), F32) for r in part_rows]
    starts, nrow = _part_starts(parts, bm)
    assert nrow * bm == m
    return pl.pallas_call(
        functools.partial(_rmsnorm_body, starts=starts, nrow=nrow),
        grid=(nrow,),
        in_specs=[pl.BlockSpec((bm, d), lambda i: (i, 0)), pl.BlockSpec((1, d), lambda i: (0, 0))],
        out_specs=_part_specs(parts, bm, d, lambda: 0),
        out_shape=parts,
        compiler_params=_params("arbitrary"),
        name="rmsnorm_rows",
    )(x, gain.reshape(1, d))


def _mlstm_body(q_ref, k_ref, v_ref, ao_ref, gc_ref, gbias_ref, norm_ref, c0_ref, n0_ref, m0_ref,
                ha_ref, c_ref, n_ref, m_ref, *, heads, dk, dv, chunk, bb, single_chunk):
    L = chunk
    if single_chunk:
        c_src, n_src, m_src = c0_ref, n0_ref, m0_ref
    else:
        c_src, n_src, m_src = c_ref, n_ref, m_ref

        @pl.when(pl.program_id(1) == 0)
        def _():
            c_ref[...] = c0_ref[...]
            n_ref[...] = n0_ref[...]
            m_ref[...] = m0_ref[...]

    row = lax.broadcasted_iota(jnp.int32, (L, L), 0)
    col = lax.broadcasted_iota(jnp.int32, (L, L), 1)
    causal = (row >= col)[None]
    tri = jnp.where(row >= col, 1.0, 0.0).astype(F32)
    pairs = [(bi, h) for bi in range(bb) for h in range(heads)]

    ig_cols, ig_rows, b_cols, b_rows = [], [], [], []
    for bi in range(bb):
        gates = gc_ref[bi * L:(bi + 1) * L, :] + gbias_ref[...]
        log_f = jnp.minimum(gates, 0.0) - jnp.log1p(jnp.exp(-jnp.abs(gates)))
        b_all = jnp.dot(tri, log_f, precision=HIGHEST, preferred_element_type=F32)
        gates_t, b_all_t = gates.T, b_all.T
        ig_cols += [gates[:, h:h + 1] for h in range(heads)]
        ig_rows += [gates_t[h:h + 1, :] for h in range(heads)]
        b_cols += [b_all[:, heads + h:heads + h + 1] for h in range(heads)]
        b_rows += [b_all_t[heads + h:heads + h + 1, :] for h in range(heads)]
    ig_c, ig_r = jnp.stack(ig_cols), jnp.stack(ig_rows)
    b_c, b_r = jnp.stack(b_cols), jnp.stack(b_rows)
    m_prev = jnp.stack([m_src[bi][0:1, h:h + 1] for bi, h in pairs])

    log_d = jnp.where(causal, b_c - b_r + ig_r, NEG)
    m_inter = m_prev + b_c
    m_t = jnp.maximum(m_inter, jnp.max(log_d, axis=-1, keepdims=True))
    d_mat = jnp.exp(log_d - m_t)
    inter = jnp.exp(m_inter - m_t)

    seq_rows = lambda ref: [ref[bi * L:(bi + 1) * L, :] for bi in range(bb)]
    q = _split_heads(seq_rows(q_ref), heads, dk)
    k = _split_heads(seq_rows(k_ref), heads, dk) * (dk ** -0.5)
    v = _split_heads(seq_rows(v_ref), heads, dv)
    qb, kb, vb = q.astype(BF16), k.astype(BF16), v.astype(BF16)
    c_old = c_src[...].reshape(bb * heads, dk, dv)
    n_old = jnp.stack([n_src[bi, h:h + 1, :] for bi, h in pairs])

    s = _bmm_nt(qb, kb) * d_mat
    num = inter * _bmm(qb, c_old.astype(BF16)) + _bmm(s.astype(BF16), vb)
    den = inter * jnp.sum(q * n_old, axis=-1, keepdims=True) + jnp.sum(s, axis=-1, keepdims=True)
    h_out = num / jnp.maximum(jnp.abs(den), jnp.exp(-m_t))

    m_new = m_t[:, L - 1:L, :]
    b_last = b_c[:, L - 1:L, :]
    carry = jnp.exp(m_prev + b_last - m_new)
    kw = k * jnp.exp(b_last - b_c + ig_c - m_new)
    c_new = carry * c_old + _bmm_tn(kw.astype(BF16), vb)
    c_ref[...] = c_new.reshape(bb, heads, dk, dv)
    n_new = carry * n_old + jnp.sum(kw, axis=1, keepdims=True)

    gain = jnp.stack([norm_ref[:, h * dv:(h + 1) * dv] for _, h in pairs])
    gate_o = _sigmoid(_split_heads(seq_rows(ao_ref), heads, dv))
    ha = (_rms_rows(h_out, gain) * gate_o).astype(BF16)

    lane = lax.broadcasted_iota(jnp.int32, (8, GATE_LANES), 1)
    for bi in range(bb):
        m_next = m_src[bi]
        for h in range(heads):
            i = bi * heads + h
            n_ref[bi, h:h + 1, :] = n_new[i]
            m_next = jnp.where(lane == h, m_new[i], m_next)
            ha_ref[bi * L:(bi + 1) * L, h * dv:(h + 1) * dv] = ha[i]
        m_ref[bi] = m_next


def mlstm(p_big, p_small, gbias, norm_row, c0, n0, m0, *, row0, seq, chunk, heads, dk, dv,
          q_off, k_off, v_off, ao_off):
    bsz = c0.shape[0]
    nc = seq // chunk
    bb = math.gcd(bsz, 4) if nc == 1 else 1
    blk = bb * chunk
    qk_w, v_w = heads * dk, heads * dv
    assert q_off % qk_w == 0 and k_off % qk_w == 0 and v_off % v_w == 0 and ao_off % v_w == 0
    assert row0 % blk == 0
    rows = lambda b, c: row0 // blk + b * nc + c
    body = functools.partial(_mlstm_body, heads=heads, dk=dk, dv=dv, chunk=chunk, bb=bb,
                             single_chunk=(nc == 1))
    return pl.pallas_call(
        body,
        grid=(bsz // bb, nc),
        in_specs=[
            pl.BlockSpec((blk, qk_w), lambda b, c: (rows(b, c), q_off // qk_w)),
            pl.BlockSpec((blk, qk_w), lambda b, c: (rows(b, c), k_off // qk_w)),
            pl.BlockSpec((blk, v_w), lambda b, c: (rows(b, c), v_off // v_w)),
            pl.BlockSpec((blk, v_w), lambda b, c: (rows(b, c), ao_off // v_w)),
            pl.BlockSpec((blk, GATE_LANES), lambda b, c: (rows(b, c), 0)),
            pl.BlockSpec((1, GATE_LANES), lambda b, c: (0, 0)),
            pl.BlockSpec((1, v_w), lambda b, c: (0, 0)),
            pl.BlockSpec((bb, heads, dk, dv), lambda b, c: (b, 0, 0, 0)),
            pl.BlockSpec((bb, heads, dk), lambda b, c: (b, 0, 0)),
            pl.BlockSpec((bb, 8, GATE_LANES), lambda b, c: (b, 0, 0)),
        ],
        out_specs=[
            pl.BlockSpec((blk, v_w), lambda b, c: (b * nc + c, 0)),
            pl.BlockSpec((bb, heads, dk, dv), lambda b, c: (b, 0, 0, 0)),
            pl.BlockSpec((bb, heads, dk), lambda b, c: (b, 0, 0)),
            pl.BlockSpec((bb, 8, GATE_LANES), lambda b, c: (b, 0, 0)),
        ],
        out_shape=[
            jax.ShapeDtypeStruct((bsz * seq, v_w), BF16),
            jax.ShapeDtypeStruct(c0.shape, F32),
            jax.ShapeDtypeStruct(n0.shape, F32),
            jax.ShapeDtypeStruct(m0.shape, F32),
        ],
        compiler_params=_params("parallel", "arbitrary"),
        name="mlstm",
    )(p_big, p_big, p_big, p_big, p_small, gbias, norm_row, c0, n0, m0)


def _shift_rows(x, prev, j):
    xs = pltpu.roll(x, j, axis=0)
    row = lax.broadcasted_iota(jnp.int32, prev.shape, 0)
    head = jnp.where(row < j, pltpu.roll(prev, j, axis=0), xs[0:8])
    if x.shape[0] == 8:
        return head
    return jnp.concatenate([head, xs[8:]], axis=0)


def _conv_silu(x_ref, w_ref, tail_src, tail_ref, width, chunk, nsub):
    L = chunk
    outs = []
    for bi in range(tail_ref.shape[0]):
        for sub in range(nsub):
            r0 = (bi * nsub + sub) * L
            x = x_ref[r0:r0 + L, :]
            prev = tail_src[bi] if sub == 0 else x_ref[r0 - 8:r0, :]
            acc = x * w_ref[width - 1:width, :]
            for j in range(1, width):
                acc = acc + _shift_rows(x, prev, j) * w_ref[width - 1 - j:width - j, :]
            outs.append(acc * _sigmoid(acc))
        tail_ref[bi] = x_ref[(bi + 1) * nsub * L - 8:(bi + 1) * nsub * L, :]
    return outs


def _gdn_body(x_ref, z_ref, w_ref, t0_ref, gc_ref, gbias_ref, alog_ref, norm_ref, s0_ref,
              ob_ref, s_ref, t_ref, *, heads, dk, dv, chunk, nsub, width, gl_col, beta_col,
              single_chunk):
    L = chunk
    bb = s_ref.shape[0]
    assert bb == 1 or nsub == 1
    if single_chunk:
        s_src, t_src = s0_ref, t0_ref
    else:
        s_src, t_src = s_ref, t_ref

        @pl.when(pl.program_id(1) == 0)
        def _():
            s_ref[...] = s0_ref[...]
            t_ref[...] = t0_ref[...]

    hw = heads * dk
    qkv = _conv_silu(x_ref, w_ref, t_src, t_ref, width, L, nsub)
    q = _split_heads([c[:, :hw] for c in qkv], heads, dk)
    k = _split_heads([c[:, hw:2 * hw] for c in qkv], heads, dk)
    v = _split_heads([c[:, 2 * hw:] for c in qkv], heads, dv)
    q = q * lax.rsqrt(jnp.sum(q * q, axis=-1, keepdims=True) + EPS) * (dk ** -0.5)
    k = k * lax.rsqrt(jnp.sum(k * k, axis=-1, keepdims=True) + EPS)

    row = lax.broadcasted_iota(jnp.int32, (L, L), 0)
    col = lax.broadcasted_iota(jnp.int32, (L, L), 1)
    causal = (row >= col)[None]
    strict = (row > col)[None]
    tri = jnp.where(row >= col, 1.0, 0.0).astype(F32)

    g_cols, g_rows, beta_cols = [], [], []
    for u in range(bb * nsub):
        raw = gc_ref[u * L:(u + 1) * L, :] + gbias_ref[...]
        gl_all = -jnp.exp(alog_ref[...]) * _softplus(raw)
        g_all = jnp.dot(tri, gl_all, precision=HIGHEST, preferred_element_type=F32)
        g_all_t = g_all.T
        beta_all = _sigmoid(raw)
        g_cols += [g_all[:, gl_col + i:gl_col + i + 1] for i in range(heads)]
        g_rows += [g_all_t[gl_col + i:gl_col + i + 1, :] for i in range(heads)]
        beta_cols += [beta_all[:, beta_col + i:beta_col + i + 1] for i in range(heads)]
    g_c, g_r, beta_c = jnp.stack(g_cols), jnp.stack(g_rows), jnp.stack(beta_cols)
    eg_c = jnp.exp(g_c)
    decay = jnp.exp(jnp.where(causal, g_c - g_r, NEG))

    kb = k.astype(BF16)
    both = _bmm_nt(jnp.concatenate([k, q], axis=1).astype(BF16), kb)
    kk, qk = both[:, :L], both[:, L:]

    x_pow = jnp.where(strict, -(beta_c * kk * decay), 0.0)
    n_inv = x_pow
    for _ in range(max(int(math.ceil(math.log2(L))) - 1, 0)):
        xb = x_pow.astype(BF16)
        x_pow = _bmm(xb, xb)
        n_inv = n_inv + x_pow + _bmm(n_inv.astype(BF16), x_pow.astype(BF16))
    rhs = jnp.concatenate([v * beta_c, k * (beta_c * eg_c)], axis=-1)
    sol = rhs + _bmm(n_inv.astype(BF16), rhs.astype(BF16))
    wq = jnp.concatenate([sol[..., dv:], q], axis=1).astype(BF16)
    u_all = sol[..., :dv]
    qkd = (qk * decay).astype(BF16)
    g_last = g_c[:, L - 1:L, :]
    kd = (k * jnp.exp(g_last - g_c)).astype(BF16)

    s_cur = s_src[...].reshape(bb * heads, dk, dv)
    hs = bb * heads
    for sub in range(nsub):
        sl = slice(sub * hs, (sub + 1) * hs)
        ws_qs = _bmm(wq[sl], s_cur.astype(BF16))
        db = (u_all[sl] - ws_qs[:, :L]).astype(BF16)
        o = _rms_rows(eg_c[sl] * ws_qs[:, L:] + _bmm(qkd[sl], db), norm_ref[...])
        s_cur = jnp.exp(g_last[sl]) * s_cur + _bmm_tn(kd[sl], db)
        for bi in range(bb):
            r0 = (bi * nsub + sub) * L
            for i in range(heads):
                z = z_ref[r0:r0 + L, i * dv:(i + 1) * dv]
                ob_ref[r0:r0 + L, i * dv:(i + 1) * dv] = (
                    o[bi * heads + i] * (z * _sigmoid(z))).astype(BF16)
    s_ref[...] = s_cur.reshape(bb, heads, dk, dv)


def gdn(p_big, p_small, gbias, alog_row, norm_row, conv_w, tail0, s0, *, row0, seq, chunk,
        qkv_off, z_off, gl_col, beta_col):
    bsz, heads, dk, dv = s0.shape
    assert dk == dv
    width = conv_w.shape[0]
    nc = seq // chunk
    bb = math.gcd(bsz, 4) if nc == 1 else 1
    nsub = 2 if nc % 2 == 0 else 1
    steps = nc // nsub
    blk = bb * nsub * chunk
    hw = heads * dk
    assert qkv_off % (3 * hw) == 0 and z_off % hw == 0 and row0 % blk == 0
    rows = lambda b, c: row0 // blk + b * steps + c
    tail_spec = pl.BlockSpec((bb, 8, 3 * hw), lambda b, c: (b, 0, 0))
    s_spec = pl.BlockSpec((bb, heads, dk, dv), lambda b, c: (b, 0, 0, 0))
    vec_spec = lambda n: pl.BlockSpec((1, n), lambda b, c: (0, 0))
    body = functools.partial(_gdn_body, heads=heads, dk=dk, dv=dv, chunk=chunk, nsub=nsub, width=width,
                             gl_col=gl_col, beta_col=beta_col, single_chunk=(nc == 1))
    return pl.pallas_call(
        body,
        grid=(bsz // bb, steps),
        in_specs=[
            pl.BlockSpec((blk, 3 * hw), lambda b, c: (rows(b, c), qkv_off // (3 * hw))),
            pl.BlockSpec((blk, hw), lambda b, c: (rows(b, c), z_off // hw)),
            pl.BlockSpec((width, 3 * hw), lambda b, c: (0, 0)),
            tail_spec,
            pl.BlockSpec((blk, GATE_LANES), lambda b, c: (rows(b, c), 0)),
            vec_spec(GATE_LANES), vec_spec(GATE_LANES), vec_spec(dv), s_spec,
        ],
        out_specs=[pl.BlockSpec((blk, hw), lambda b, c: (b * steps + c, 0)), s_spec, tail_spec],
        out_shape=[
            jax.ShapeDtypeStruct((bsz * seq, heads * dv), BF16),
            jax.ShapeDtypeStruct(s0.shape, F32),
            jax.ShapeDtypeStruct(tail0.shape, F32),
        ],
        compiler_params=_params("parallel", "arbitrary"),
        name="gdn",
    )(p_big, p_big, conv_w, tail0, p_small, gbias, alog_row, norm_row, s0)


def _pad_lanes(vec, offset):
    return jnp.zeros((1, GATE_LANES), F32).at[0, offset:offset + vec.shape[0]].set(vec.astype(F32))


def _layer(x_parts, streams, lw, dims):
    h_a, dk_a, dv_a, h_b, dk_b, dv_b, width = dims
    a_qk, a_v, b_k, b_v = h_a * dk_a, h_a * dv_a, h_b * dk_b, h_b * dv_b
    d = x_parts[0].shape[1]
    ff = lw["ffn1_gate"].shape[1]
    bm = 1024
    for p in x_parts:
        bm = _block(p.shape[0], bm)
    bf, bd, bd_down, bd_out = _block(ff, 512), _block(d, 512), _block(d, 512), _block(d, 1024)

    act = ffn_up(x_parts, lw["norm1"], lw["ffn1_gate"], lw["ffn1_up"], bm=bm, bf=bf)
    x = matmul_residual(act, lw["ffn1_down"], x_parts, scale=0.5, bm=bm, bn=bd_down)

    offs = dict(q=0, k=a_qk, v=2 * a_qk, ao=2 * a_qk + a_v)
    offs["bq"] = offs["ao"] + a_v
    offs["bk"] = offs["bq"] + b_k
    offs["bv"] = offs["bk"] + b_k
    offs["bz"] = offs["bv"] + b_v
    offs["ga"] = offs["bz"] + b_v
    offs["gb"] = offs["ga"] + d
    p_big, p_small = in_proj(x, lw["norm2"], lw["w_big"], lw["w_small"], bm=bm,
                             bn=_block(lw["w_big"].shape[1], 1024))

    gl_col, beta_col = 2 * h_a, 2 * h_a + h_b
    gbias = (_pad_lanes(lw["mlstm_b_i"], 0) + _pad_lanes(lw["mlstm_b_f"], h_a)
             + _pad_lanes(lw["gdn_dt_bias"], gl_col))
    alog_row = _pad_lanes(lw["gdn_A_log"], gl_col)

    ha_parts, ob_parts, new_states = [], [], []
    for st in streams:
        bsz = st["bsz"]
        m0 = jnp.zeros((bsz, 8, GATE_LANES), F32).at[:, :, :h_a].set(
            jnp.broadcast_to(st["m"][:, None, :], (bsz, 8, h_a)))
        ha, c1, n1, m1 = mlstm(
            p_big, p_small, gbias, lw["mlstm_norm"].reshape(1, a_v), st["C"], st["n"], m0,
            row0=st["row0"], seq=st["seq"], chunk=st["chunk_a"], heads=h_a, dk=dk_a, dv=dv_a,
            q_off=offs["q"], k_off=offs["k"], v_off=offs["v"], ao_off=offs["ao"])
        tail0 = jnp.pad(st["conv"], ((0, 0), (8 - (width - 1), 0), (0, 0)))
        ob, s1, tail1 = gdn(
            p_big, p_small, gbias, alog_row, lw["gdn_norm"].reshape(1, dv_b), lw["gdn_conv"], tail0,
            st["S"], row0=st["row0"], seq=st["seq"], chunk=st["chunk_b"],
            qkv_off=offs["bq"], z_off=offs["bz"], gl_col=gl_col, beta_col=beta_col)
        conv1 = tail1[:, 8 - (width - 1):, :]
        ha_parts.append(ha)
        ob_parts.append(ob)
        new_states.append((conv1, s1, c1, n1, m1[:, 0, :h_a]))

    merged = merge(ha_parts, ob_parts, lw["w_branch_a"], lw["w_branch_b"], p_big,
                   ga_off=offs["ga"], gb_off=offs["gb"], bm=bm, bn=bd)
    x = matmul_residual(merged, lw["w_out"], [x], scale=1.0, bm=bm, bn=bd_out)

    act = ffn_up([x], lw["norm3"], lw["ffn2_gate"], lw["ffn2_up"], bm=bm, bf=bf)
    x = matmul_residual(act, lw["ffn2_down"], [x], scale=0.5, bm=bm, bn=bd_down)
    return x, new_states


def _split_w_in_body(wt_ref, big_ref):
    big_ref[...] = wt_ref[...].T.astype(BF16)


def split_w_in(w_in, dims):
    h_a, dk_a, dv_a, h_b, dk_b, dv_b, _ = dims
    a_qk, a_v, b_k, b_v = h_a * dk_a, h_a * dv_a, h_b * dk_b, h_b * dv_b
    d, d_in = w_in.shape
    runs = [(2 * a_qk + a_v, True), (2 * h_a, False), (a_v + 2 * b_k + b_v, True), (2 * h_b, False),
            (b_v + 2 * d, True)]
    assert sum(r for r, _ in runs) == d_in
    big_segs, small_segs, src, big_dst = [], [], 0, 0
    for width, is_big in runs:
        if is_big:
            big_segs.append((src, width, big_dst))
            big_dst += width
        else:
            small_segs.append((src, width))
        src += width
    rb = 1024
    for s, width, _ in big_segs:
        assert s % 8 == 0
        rb = _block(width, rb)
    w_t = jnp.swapaxes(w_in, 0, 1)

    def src_row(i):
        row = i * rb
        for s, _, dst in big_segs[1:]:
            row = jnp.where(i * rb >= dst, i * rb + (s - dst), row)
        return pl.multiple_of(row, 8)

    w_big = pl.pallas_call(
        _split_w_in_body,
        grid=(big_dst // rb,),
        in_specs=[pl.BlockSpec((pl.Element(rb), pl.Element(d)), lambda i: (src_row(i), 0))],
        out_specs=pl.BlockSpec((d, rb), lambda i: (0, i)),
        out_shape=jax.ShapeDtypeStruct((d, big_dst), BF16),
        compiler_params=_params("parallel"),
        name="split_w_in",
    )(w_t)
    small = jnp.concatenate([w_t[s:s + width] for s, width in small_segs], axis=0)
    assert small.shape[0] <= GATE_LANES
    w_small = jnp.pad(small, ((0, GATE_LANES - small.shape[0]), (0, 0))).T
    return w_big, w_small


def _prep_layer_weights(l, w, dims):
    lw = {k: w[k][l] for k in w if k != "w_in"}
    for k in ("ffn1_down", "ffn2_down", "w_branch_a", "w_branch_b", "w_out"):
        lw[k] = lw[k].astype(BF16)
    lw["w_big"], lw["w_small"] = split_w_in(w["w_in"][l], dims)
    return lw


def kernel(x_prompt, x_sample, state_conv, state_gdn, state_mlstm_C, state_mlstm_n, state_mlstm_m,
           norm1, ffn1_gate, ffn1_up, ffn1_down, norm2, w_in, mlstm_b_i, mlstm_b_f, mlstm_norm,
           gdn_conv, gdn_A_log, gdn_dt_bias, gdn_norm, w_branch_a, w_branch_b, w_out,
           norm3, ffn2_gate, ffn2_up, ffn2_down, norm_f):
    depth = norm1.shape[0]
    bp, sp, d = x_prompt.shape
    bs, ss, _ = x_sample.shape
    _, _, h_a, dk_a, dv_a = state_mlstm_C.shape
    _, _, h_b, dk_b, dv_b = state_gdn.shape
    width = gdn_conv.shape[1]
    dims = (h_a, dk_a, dv_a, h_b, dk_b, dv_b, width)
    w = dict(norm1=norm1, ffn1_gate=ffn1_gate, ffn1_up=ffn1_up, ffn1_down=ffn1_down, norm2=norm2,
             w_in=w_in, mlstm_b_i=mlstm_b_i, mlstm_b_f=mlstm_b_f, mlstm_norm=mlstm_norm,
             gdn_conv=gdn_conv, gdn_A_log=gdn_A_log, gdn_dt_bias=gdn_dt_bias, gdn_norm=gdn_norm,
             w_branch_a=w_branch_a, w_branch_b=w_branch_b, w_out=w_out, norm3=norm3,
             ffn2_gate=ffn2_gate, ffn2_up=ffn2_up, ffn2_down=ffn2_down)

    x_parts = [x_prompt.reshape(bp * sp, d), x_sample.reshape(bs * ss, d)]
    zeros = lambda *shape: jnp.zeros(shape, F32)
    prompt = dict(row0=0, bsz=bp, seq=sp, chunk_a=math.gcd(sp, 256), chunk_b=math.gcd(sp, CHUNK))
    sample = dict(row0=bp * sp, bsz=bs, seq=ss, chunk_a=math.gcd(ss, 256), chunk_b=math.gcd(ss, CHUNK))

    per_layer = [[], []]
    for l in range(depth):
        lw = _prep_layer_weights(l, w, dims)
        prompt.update(conv=zeros(bp, width - 1, (2 * dk_b + dv_b) * h_b), S=zeros(bp, h_b, dk_b, dv_b),
                      C=zeros(bp, h_a, dk_a, dv_a), n=zeros(bp, h_a, dk_a), m=zeros(bp, h_a))
        sample.update(conv=state_conv[l], S=state_gdn[l], C=state_mlstm_C[l], n=state_mlstm_n[l],
                      m=state_mlstm_m[l])
        x, states = _layer(x_parts, [prompt, sample], lw, dims)
        x_parts = [x]
        per_layer[0].append(states[0])
        per_layer[1].append(states[1])

    part_rows = (bp * sp, bs * ss)
    y_prompt, y_sample = rmsnorm_rows(x, norm_f, part_rows,
                                      bm=_block(math.gcd(*part_rows), 512))
    stack = lambda sts: tuple(jnp.stack(z) for z in zip(*sts))
    return ((y_prompt.reshape(bp, sp, d), y_sample.reshape(bs, ss, d))
            + stack(per_layer[0]) + stack(per_layer[1]))
```

```python
import functools
import math

import jax
import jax.numpy as jnp
from jax import lax
from jax.experimental import pallas as pl
from jax.experimental.pallas import tpu as pltpu

F32 = jnp.float32
BF16 = jnp.bfloat16
HIGHEST = lax.Precision.HIGHEST
EPS = 1e-6
NEG = -1e30
CHUNK = 64
GATE_LANES = 128
V7X_VMEM_BYTES = 64 * 1024 * 1024
VMEM_LIMIT = V7X_VMEM_BYTES - 4 * 1024 * 1024


def _block(n, preferred):
    return math.gcd(n, preferred)


def _params(*semantics):
    return pltpu.CompilerParams(dimension_semantics=semantics, vmem_limit_bytes=VMEM_LIMIT)


def _sigmoid(x):
    return 1.0 / (1.0 + jnp.exp(-x))


def _softplus(x):
    return jnp.maximum(x, 0.0) + jnp.log1p(jnp.exp(-jnp.abs(x)))


def _rms_rows(x, gain):
    return x * lax.rsqrt(jnp.mean(x * x, axis=-1, keepdims=True) + EPS) * gain


def _mm(a, b):
    return jnp.dot(a, b, preferred_element_type=F32)


def _bmm(a, b):
    return jnp.einsum('hij,hjk->hik', a, b, preferred_element_type=F32)


def _bmm_nt(a, b):
    return jnp.einsum('hik,hjk->hij', a, b, preferred_element_type=F32)


def _bmm_tn(a, b):
    return jnp.einsum('hsk,hsv->hkv', a, b, preferred_element_type=F32)


def _split_heads(seqs, heads, width):
    return jnp.stack([a[:, i * width:(i + 1) * width] for a in seqs for i in range(heads)])


def _part_starts(parts, bm):
    starts, total = [], 0
    for p in parts:
        assert p.shape[0] % bm == 0
        starts.append(total)
        total += p.shape[0] // bm
    return tuple(starts), total


def _part_specs(parts, bm, cols, col_of, **spec_kwargs):
    starts, _ = _part_starts(parts, bm)
    specs = []
    for p, s in zip(parts, starts):
        nb = p.shape[0] // bm
        specs.append(pl.BlockSpec(
            (bm, cols), lambda i, *js, s=s, nb=nb: (jnp.clip(i - s, 0, nb - 1), col_of(*js)),
            **spec_kwargs))
    return specs


def _read_parts(refs, starts):
    val = refs[0][...]
    for r, s in zip(refs[1:], starts[1:]):
        val = jnp.where(pl.program_id(0) >= s, r[...], val)
    return val


def _on_owner_part(part_refs, starts, nrow, fn, also=True):
    if len(part_refs) == 1:
        if also is True:
            fn(part_refs[0])
        else:
            pl.when(also)(functools.partial(fn, part_refs[0]))
        return
    i = pl.program_id(0)
    ends = starts[1:] + (nrow,)
    for refs, lo, hi in zip(part_refs, starts, ends):
        pl.when((i >= lo) & (i < hi) & also)(functools.partial(fn, refs))


def _ffn_up_body(*refs, starts, nrow):
    x_refs = refs[:len(starts)]
    g_ref, wg_ref, wu_ref, act_ref, h_ref = refs[len(starts):]

    def norm_rows(x_ref):
        h_ref[...] = _rms_rows(x_ref[...], g_ref[...]).astype(BF16)

    _on_owner_part(x_refs, starts, nrow, norm_rows, also=pl.program_id(1) == 0)
    h = h_ref[...]
    gate = _mm(h, wg_ref[...].astype(BF16))
    up = _mm(h, wu_ref[...].astype(BF16))
    act_ref[...] = (gate * _sigmoid(gate) * up).astype(BF16)


def ffn_up(x_parts, gain, wg, wu, *, bm, bf):
    d, ff = wg.shape
    starts, nrow = _part_starts(x_parts, bm)
    x_mode = {}
    return pl.pallas_call(
        functools.partial(_ffn_up_body, starts=starts, nrow=nrow),
        grid=(nrow, ff // bf),
        in_specs=_part_specs(x_parts, bm, d, lambda j: 0, **x_mode) + [
            pl.BlockSpec((1, d), lambda i, j: (0, 0)),
            pl.BlockSpec((d, bf), lambda i, j: (0, j)),
            pl.BlockSpec((d, bf), lambda i, j: (0, j)),
        ],
        out_specs=pl.BlockSpec((bm, bf), lambda i, j: (i, j)),
        out_shape=jax.ShapeDtypeStruct((nrow * bm, ff), BF16),
        scratch_shapes=[pltpu.VMEM((bm, d), BF16)],
        compiler_params=_params("parallel", "arbitrary"),
        name="ffn_up",
    )(*x_parts, gain.reshape(1, d), wg, wu)


def _mm_res_body(a_ref, w_ref, *refs, scale, starts):
    x_refs, o_ref = refs[:-1], refs[-1]
    o_ref[...] = _read_parts(x_refs, starts) + scale * _mm(a_ref[...], w_ref[...])


def matmul_residual(a, w, x_parts, *, scale, bm, bn):
    m, k = a.shape
    n = w.shape[1]
    starts, nrow = _part_starts(x_parts, bm)
    assert nrow * bm == m
    return pl.pallas_call(
        functools.partial(_mm_res_body, scale=scale, starts=starts),
        grid=(nrow, n // bn),
        in_specs=[
            pl.BlockSpec((bm, k), lambda i, j: (i, 0)),
            pl.BlockSpec((k, bn), lambda i, j: (0, j)),
        ] + _part_specs(x_parts, bm, bn, lambda j: j),
        out_specs=pl.BlockSpec((bm, bn), lambda i, j: (i, j)),
        out_shape=jax.ShapeDtypeStruct((m, n), F32),
        compiler_params=_params("parallel", "arbitrary"),
        name="matmul_residual",
    )(a, w, *x_parts)


def _in_proj_body(x_ref, g_ref, w_ref, ws_ref, p_ref, ps_ref, h_ref):
    @pl.when(pl.program_id(1) == 0)
    def _():
        h = _rms_rows(x_ref[...], g_ref[...])
        h_hi = h.astype(BF16)
        h_ref[...] = h_hi
        h_lo = (h - h_hi.astype(F32)).astype(BF16)
        both = _mm(h_hi, ws_ref[...])
        ps_ref[...] = (both[:, :GATE_LANES] + both[:, GATE_LANES:]
                       + _mm(h_lo, ws_ref[:, :GATE_LANES]))

    p_ref[...] = _mm(h_ref[...], w_ref[...])


def in_proj(x, gain, w_big, w_small, *, bm, bn):
    m, d = x.shape
    n = w_big.shape[1]
    ws_hi = w_small.astype(BF16)
    ws_lo = (w_small - ws_hi.astype(F32)).astype(BF16)
    ws_split = jnp.concatenate([ws_hi, ws_lo], axis=1)
    return pl.pallas_call(
        _in_proj_body,
        grid=(m // bm, n // bn),
        in_specs=[
            pl.BlockSpec((bm, d), lambda i, j: (i, 0)),
            pl.BlockSpec((1, d), lambda i, j: (0, 0)),
            pl.BlockSpec((d, bn), lambda i, j: (0, j)),
            pl.BlockSpec((d, 2 * GATE_LANES), lambda i, j: (0, 0)),
        ],
        out_specs=[
            pl.BlockSpec((bm, bn), lambda i, j: (i, j)),
            pl.BlockSpec((bm, GATE_LANES), lambda i, j: (i, 0)),
        ],
        out_shape=[
            jax.ShapeDtypeStruct((m, n), F32),
            jax.ShapeDtypeStruct((m, GATE_LANES), F32),
        ],
        scratch_shapes=[pltpu.VMEM((bm, d), BF16)],
        compiler_params=_params("parallel", "arbitrary"),
        name="in_proj",
    )(x, gain.reshape(1, d), w_big, ws_split)


def _merge_body(*refs, starts):
    n = len(starts)
    ha_refs, ob_refs = refs[:n], refs[n:2 * n]
    wa_ref, wb_ref, ga_ref, gb_ref, o_ref = refs[2 * n:]
    ya = _mm(_read_parts(ha_refs, starts), wa_ref[...])
    yb = _mm(_read_parts(ob_refs, starts), wb_ref[...])
    o_ref[...] = (_sigmoid(ga_ref[...]) * ya + _sigmoid(gb_ref[...]) * yb).astype(BF16)


def merge(ha_parts, ob_parts, wa, wb, p_big, *, ga_off, gb_off, bm, bn):
    ka, n = wa.shape
    kb = wb.shape[0]
    assert ga_off % bn == 0 and gb_off % bn == 0
    ga_blk, gb_blk = ga_off // bn, gb_off // bn
    starts, nrow = _part_starts(ha_parts, bm)
    assert _part_starts(ob_parts, bm) == (starts, nrow) and nrow * bm == p_big.shape[0]
    return pl.pallas_call(
        functools.partial(_merge_body, starts=starts),
        grid=(nrow, n // bn),
        in_specs=_part_specs(ha_parts, bm, ka, lambda j: 0) + _part_specs(ob_parts, bm, kb, lambda j: 0) + [
            pl.BlockSpec((ka, bn), lambda i, j: (0, j)),
            pl.BlockSpec((kb, bn), lambda i, j: (0, j)),
            pl.BlockSpec((bm, bn), lambda i, j: (i, ga_blk + j)),
            pl.BlockSpec((bm, bn), lambda i, j: (i, gb_blk + j)),
        ],
        out_specs=pl.BlockSpec((bm, bn), lambda i, j: (i, j)),
        out_shape=jax.ShapeDtypeStruct((nrow * bm, n), BF16),
        compiler_params=_params("parallel", "arbitrary"),
        name="merge",
    )(*ha_parts, *ob_parts, wa, wb, p_big, p_big)


def _rmsnorm_body(x_ref, g_ref, *o_refs, starts, nrow):
    y = _rms_rows(x_ref[...], g_ref[...])

    def store(o_ref):
        o_ref[...] = y

    _on_owner_part(o_refs, starts, nrow, store)


def rmsnorm_rows(x, gain, part_rows, *, bm):
    m, d = x.shape
    parts = [jax.ShapeDtypeStruct((r, d), F32) for r in part_rows]
    starts, nrow = _part_starts(parts, bm)
    assert nrow * bm == m
    return pl.pallas_call(
        functools.partial(_rmsnorm_body, starts=starts, nrow=nrow),
        grid=(nrow,),
        in_specs=[pl.BlockSpec((bm, d), lambda i: (i, 0)), pl.BlockSpec((1, d), lambda i: (0, 0))],
        out_specs=_part_specs(parts, bm, d, lambda: 0),
        out_shape=parts,
        compiler_params=_params("arbitrary"),
        name="rmsnorm_rows",
    )(x, gain.reshape(1, d))


def _mlstm_body(q_ref, k_ref, v_ref, ao_ref, gc_ref, gbias_ref, norm_ref, c0_ref, n0_ref, m0_ref,
                ha_ref, c_ref, n_ref, m_ref, *, heads, dk, dv, chunk, bb, single_chunk):
    L = chunk
    if single_chunk:
        c_src, n_src, m_src = c0_ref, n0_ref, m0_ref
    else:
        c_src, n_src, m_src = c_ref, n_ref, m_ref

        @pl.when(pl.program_id(1) == 0)
        def _():
            c_ref[...] = c0_ref[...]
            n_ref[...] = n0_ref[...]
            m_ref[...] = m0_ref[...]

    row = lax.broadcasted_iota(jnp.int32, (L, L), 0)
    col = lax.broadcasted_iota(jnp.int32, (L, L), 1)
    causal = (row >= col)[None]
    tri = jnp.where(row >= col, 1.0, 0.0).astype(F32)
    pairs = [(bi, h) for bi in range(bb) for h in range(heads)]

    ig_cols, ig_rows, b_cols, b_rows = [], [], [], []
    for bi in range(bb):
        gates = gc_ref[bi * L:(bi + 1) * L, :] + gbias_ref[...]
        log_f = jnp.minimum(gates, 0.0) - jnp.log1p(jnp.exp(-jnp.abs(gates)))
        b_all = jnp.dot(tri, log_f, precision=HIGHEST, preferred_element_type=F32)
        gates_t, b_all_t = gates.T, b_all.T
        ig_cols += [gates[:, h:h + 1] for h in range(heads)]
        ig_rows += [gates_t[h:h + 1, :] for h in range(heads)]
        b_cols += [b_all[:, heads + h:heads + h + 1] for h in range(heads)]
        b_rows += [b_all_t[heads + h:heads + h + 1, :] for h in range(heads)]
    ig_c, ig_r = jnp.stack(ig_cols), jnp.stack(ig_rows)
    b_c, b_r = jnp.stack(b_cols), jnp.stack(b_rows)
    m_prev = jnp.stack([m_src[bi][0:1, h:h + 1] for bi, h in pairs])

    log_d = jnp.where(causal, b_c - b_r + ig_r, NEG)
    m_inter = m_prev + b_c
    m_t = jnp.maximum(m_inter, jnp.max(log_d, axis=-1, keepdims=True))
    d_mat = jnp.exp(log_d - m_t)
    inter = jnp.exp(m_inter - m_t)

    seq_rows = lambda ref: [ref[bi * L:(bi + 1) * L, :] for bi in range(bb)]
    q = _split_heads(seq_rows(q_ref), heads, dk)
    k = _split_heads(seq_rows(k_ref), heads, dk) * (dk ** -0.5)
    v = _split_heads(seq_rows(v_ref), heads, dv)
    qb, kb, vb = q.astype(BF16), k.astype(BF16), v.astype(BF16)
    c_old = c_src[...].reshape(bb * heads, dk, dv)
    n_old = jnp.stack([n_src[bi, h:h + 1, :] for bi, h in pairs])

    s = _bmm_nt(qb, kb) * d_mat
    num = inter * _bmm(qb, c_old.astype(BF16)) + _bmm(s.astype(BF16), vb)
    den = inter * jnp.sum(q * n_old, axis=-1, keepdims=True) + jnp.sum(s, axis=-1, keepdims=True)
    h_out = num / jnp.maximum(jnp.abs(den), jnp.exp(-m_t))

    m_new = m_t[:, L - 1:L, :]
    b_last = b_c[:, L - 1:L, :]
    carry = jnp.exp(m_prev + b_last - m_new)
    kw = k * jnp.exp(b_last - b_c + ig_c - m_new)
    c_new = carry * c_old + _bmm_tn(kw.astype(BF16), vb)
    c_ref[...] = c_new.reshape(bb, heads, dk, dv)
    n_new = carry * n_old + jnp.sum(kw, axis=1, keepdims=True)

    gain = jnp.stack([norm_ref[:, h * dv:(h + 1) * dv] for _, h in pairs])
    gate_o = _sigmoid(_split_heads(seq_rows(ao_ref), heads, dv))
    ha = (_rms_rows(h_out, gain) * gate_o).astype(BF16)

    lane = lax.broadcasted_iota(jnp.int32, (8, GATE_LANES), 1)
    for bi in range(bb):
        m_next = m_src[bi]
        for h in range(heads):
            i = bi * heads + h
            n_ref[bi, h:h + 1, :] = n_new[i]
            m_next = jnp.where(lane == h, m_new[i], m_next)
            ha_ref[bi * L:(bi + 1) * L, h * dv:(h + 1) * dv] = ha[i]
        m_ref[bi] = m_next


def mlstm(p_big, p_small, gbias, norm_row, c0, n0, m0, *, row0, seq, chunk, heads, dk, dv,
          q_off, k_off, v_off, ao_off):
    bsz = c0.shape[0]
    nc = seq // chunk
    bb = math.gcd(bsz, 4) if nc == 1 else 1
    blk = bb * chunk
    qk_w, v_w = heads * dk, heads * dv
    assert q_off % qk_w == 0 and k_off % qk_w == 0 and v_off % v_w == 0 and ao_off % v_w == 0
    assert row0 % blk == 0
    rows = lambda b, c: row0 // blk + b * nc + c
    body = functools.partial(_mlstm_body, heads=heads, dk=dk, dv=dv, chunk=chunk, bb=bb,
                             single_chunk=(nc == 1))
    return pl.pallas_call(
        body,
        grid=(bsz // bb, nc),
        in_specs=[
            pl.BlockSpec((blk, qk_w), lambda b, c: (rows(b, c), q_off // qk_w)),
            pl.BlockSpec((blk, qk_w), lambda b, c: (rows(b, c), k_off // qk_w)),
            pl.BlockSpec((blk, v_w), lambda b, c: (rows(b, c), v_off // v_w)),
            pl.BlockSpec((blk, v_w), lambda b, c: (rows(b, c), ao_off // v_w)),
            pl.BlockSpec((blk, GATE_LANES), lambda b, c: (rows(b, c), 0)),
            pl.BlockSpec((1, GATE_LANES), lambda b, c: (0, 0)),
            pl.BlockSpec((1, v_w), lambda b, c: (0, 0)),
            pl.BlockSpec((bb, heads, dk, dv), lambda b, c: (b, 0, 0, 0)),
            pl.BlockSpec((bb, heads, dk), lambda b, c: (b, 0, 0)),
            pl.BlockSpec((bb, 8, GATE_LANES), lambda b, c: (b, 0, 0)),
        ],
        out_specs=[
            pl.BlockSpec((blk, v_w), lambda b, c: (b * nc + c, 0)),
            pl.BlockSpec((bb, heads, dk, dv), lambda b, c: (b, 0, 0, 0)),
            pl.BlockSpec((bb, heads, dk), lambda b, c: (b, 0, 0)),
            pl.BlockSpec((bb, 8, GATE_LANES), lambda b, c: (b, 0, 0)),
        ],
        out_shape=[
            jax.ShapeDtypeStruct((bsz * seq, v_w), BF16),
            jax.ShapeDtypeStruct(c0.shape, F32),
            jax.ShapeDtypeStruct(n0.shape, F32),
            jax.ShapeDtypeStruct(m0.shape, F32),
        ],
        compiler_params=_params("parallel", "arbitrary"),
        name="mlstm",
    )(p_big, p_big, p_big, p_big, p_small, gbias, norm_row, c0, n0, m0)


def _shift_rows(x, prev, j):
    xs = pltpu.roll(x, j, axis=0)
    row = lax.broadcasted_iota(jnp.int32, prev.shape, 0)
    head = jnp.where(row < j, pltpu.roll(prev, j, axis=0), xs[0:8])
    if x.shape[0] == 8:
        return head
    return jnp.concatenate([head, xs[8:]], axis=0)


def _conv_silu(x_ref, w_ref, tail_src, tail_ref, width, chunk, nsub):
    L = chunk
    outs = []
    for bi in range(tail_ref.shape[0]):
        for sub in range(nsub):
            r0 = (bi * nsub + sub) * L
            x = x_ref[r0:r0 + L, :]
            prev = tail_src[bi] if sub == 0 else x_ref[r0 - 8:r0, :]
            acc = x * w_ref[width - 1:width, :]
            for j in range(1, width):
                acc = acc + _shift_rows(x, prev, j) * w_ref[width - 1 - j:width - j, :]
            outs.append(acc * _sigmoid(acc))
        tail_ref[bi] = x_ref[(bi + 1) * nsub * L - 8:(bi + 1) * nsub * L, :]
    return outs


def _gdn_body(x_ref, z_ref, w_ref, t0_ref, gc_ref, gbias_ref, alog_ref, norm_ref, s0_ref,
              ob_ref, s_ref, t_ref, *, heads, dk, dv, chunk, nsub, width, gl_col, beta_col,
              single_chunk):
    L = chunk
    bb = s_ref.shape[0]
    assert bb == 1 or nsub == 1
    if single_chunk:
        s_src, t_src = s0_ref, t0_ref
    else:
        s_src, t_src = s_ref, t_ref

        @pl.when(pl.program_id(1) == 0)
        def _():
            s_ref[...] = s0_ref[...]
            t_ref[...] = t0_ref[...]

    hw = heads * dk
    qkv = _conv_silu(x_ref, w_ref, t_src, t_ref, width, L, nsub)
    q = _split_heads([c[:, :hw] for c in qkv], heads, dk)
    k = _split_heads([c[:, hw:2 * hw] for c in qkv], heads, dk)
    v = _split_heads([c[:, 2 * hw:] for c in qkv], heads, dv)
    q = q * lax.rsqrt(jnp.sum(q * q, axis=-1, keepdims=True) + EPS) * (dk ** -0.5)
    k = k * lax.rsqrt(jnp.sum(k * k, axis=-1, keepdims=True) + EPS)

    row = lax.broadcasted_iota(jnp.int32, (L, L), 0)
    col = lax.broadcasted_iota(jnp.int32, (L, L), 1)
    causal = (row >= col)[None]
    strict = (row > col)[None]
    tri = jnp.where(row >= col, 1.0, 0.0).astype(F32)

    g_cols, g_rows, beta_cols = [], [], []
    for u in range(bb * nsub):
        raw = gc_ref[u * L:(u + 1) * L, :] + gbias_ref[...]
        gl_all = -jnp.exp(alog_ref[...]) * _softplus(raw)
        g_all = jnp.dot(tri, gl_all, precision=HIGHEST, preferred_element_type=F32)
        g_all_t = g_all.T
        beta_all = _sigmoid(raw)
        g_cols += [g_all[:, gl_col + i:gl_col + i + 1] for i in range(heads)]
        g_rows += [g_all_t[gl_col + i:gl_col + i + 1, :] for i in range(heads)]
        beta_cols += [beta_all[:, beta_col + i:beta_col + i + 1] for i in range(heads)]
    g_c, g_r, beta_c = jnp.stack(g_cols), jnp.stack(g_rows), jnp.stack(beta_cols)
    eg_c = jnp.exp(g_c)
    decay = jnp.exp(jnp.where(causal, g_c - g_r, NEG))

    kb = k.astype(BF16)
    both = _bmm_nt(jnp.concatenate([k, q], axis=1).astype(BF16), kb)
    kk, qk = both[:, :L], both[:, L:]

    x_pow = jnp.where(strict, -(beta_c * kk * decay), 0.0)
    n_inv = x_pow
    for _ in range(max(int(math.ceil(math.log2(L))) - 1, 0)):
        xb = x_pow.astype(BF16)
        x_pow = _bmm(xb, xb)
        n_inv = n_inv + x_pow + _bmm(n_inv.astype(BF16), x_pow.astype(BF16))
    rhs = jnp.concatenate([v * beta_c, k * (beta_c * eg_c)], axis=-1)
    sol = rhs + _bmm(n_inv.astype(BF16), rhs.astype(BF16))
    wq = jnp.concatenate([sol[..., dv:], q], axis=1).astype(BF16)
    u_all = sol[..., :dv]
    qkd = (qk * decay).astype(BF16)
    g_last = g_c[:, L - 1:L, :]
    kd = (k * jnp.exp(g_last - g_c)).astype(BF16)

    s_cur = s_src[...].reshape(bb * heads, dk, dv)
    hs = bb * heads
    for sub in range(nsub):
        sl = slice(sub * hs, (sub + 1) * hs)
        ws_qs = _bmm(wq[sl], s_cur.astype(BF16))
        db = (u_all[sl] - ws_qs[:, :L]).astype(BF16)
        o = _rms_rows(eg_c[sl] * ws_qs[:, L:] + _bmm(qkd[sl], db), norm_ref[...])
        s_cur = jnp.exp(g_last[sl]) * s_cur + _bmm_tn(kd[sl], db)
        for bi in range(bb):
            r0 = (bi * nsub + sub) * L
            for i in range(heads):
                z = z_ref[r0:r0 + L, i * dv:(i + 1) * dv]
                ob_ref[r0:r0 + L, i * dv:(i + 1) * dv] = (
                    o[bi * heads + i] * (z * _sigmoid(z))).astype(BF16)
    s_ref[...] = s_cur.reshape(bb, heads, dk, dv)


def gdn(p_big, p_small, gbias, alog_row, norm_row, conv_w, tail0, s0, *, row0, seq, chunk,
        qkv_off, z_off, gl_col, beta_col):
    bsz, heads, dk, dv = s0.shape
    assert dk == dv
    width = conv_w.shape[0]
    nc = seq // chunk
    bb = math.gcd(bsz, 4) if nc == 1 else 1
    nsub = 2 if nc % 2 == 0 else 1
    steps = nc // nsub
    blk = bb * nsub * chunk
    hw = heads * dk
    assert qkv_off % (3 * hw) == 0 and z_off % hw == 0 and row0 % blk == 0
    rows = lambda b, c: row0 // blk + b * steps + c
    tail_spec = pl.BlockSpec((bb, 8, 3 * hw), lambda b, c: (b, 0, 0))
    s_spec = pl.BlockSpec((bb, heads, dk, dv), lambda b, c: (b, 0, 0, 0))
    vec_spec = lambda n: pl.BlockSpec((1, n), lambda b, c: (0, 0))
    body = functools.partial(_gdn_body, heads=heads, dk=dk, dv=dv, chunk=chunk, nsub=nsub, width=width,
                             gl_col=gl_col, beta_col=beta_col, single_chunk=(nc == 1))
    return pl.pallas_call(
        body,
        grid=(bsz // bb, steps),
        in_specs=[
            pl.BlockSpec((blk, 3 * hw), lambda b, c: (rows(b, c), qkv_off // (3 * hw))),
            pl.BlockSpec((blk, hw), lambda b, c: (rows(b, c), z_off // hw)),
            pl.BlockSpec((width, 3 * hw), lambda b, c: (0, 0)),
            tail_spec,
            pl.BlockSpec((blk, GATE_LANES), lambda b, c: (rows(b, c), 0)),
            vec_spec(GATE_LANES), vec_spec(GATE_LANES), vec_spec(dv), s_spec,
        ],
        out_specs=[pl.BlockSpec((blk, hw), lambda b, c: (b * steps + c, 0)), s_spec, tail_spec],
        out_shape=[
            jax.ShapeDtypeStruct((bsz * seq, heads * dv), BF16),
            jax.ShapeDtypeStruct(s0.shape, F32),
            jax.ShapeDtypeStruct(tail0.shape, F32),
        ],
        compiler_params=_params("parallel", "arbitrary"),
        name="gdn",
    )(p_big, p_big, conv_w, tail0, p_small, gbias, alog_row, norm_row, s0)


def _pad_lanes(vec, offset):
    return jnp.zeros((1, GATE_LANES), F32).at[0, offset:offset + vec.shape[0]].set(vec.astype(F32))


def _layer(x_parts, streams, lw, dims):
    h_a, dk_a, dv_a, h_b, dk_b, dv_b, width = dims
    a_qk, a_v, b_k, b_v = h_a * dk_a, h_a * dv_a, h_b * dk_b, h_b * dv_b
    d = x_parts[0].shape[1]
    ff = lw["ffn1_gate"].shape[1]
    bm = 1024
    for p in x_parts:
        bm = _block(p.shape[0], bm)
    bf, bd, bd_down, bd_out = _block(ff, 512), _block(d, 512), _block(d, 512), _block(d, 1024)

    act = ffn_up(x_parts, lw["norm1"], lw["ffn1_gate"], lw["ffn1_up"], bm=bm, bf=bf)
    x = matmul_residual(act, lw["ffn1_down"], x_parts, scale=0.5, bm=bm, bn=bd_down)

    offs = dict(q=0, k=a_qk, v=2 * a_qk, ao=2 * a_qk + a_v)
    offs["bq"] = offs["ao"] + a_v
    offs["bk"] = offs["bq"] + b_k
    offs["bv"] = offs["bk"] + b_k
    offs["bz"] = offs["bv"] + b_v
    offs["ga"] = offs["bz"] + b_v
    offs["gb"] = offs["ga"] + d
    p_big, p_small = in_proj(x, lw["norm2"], lw["w_big"], lw["w_small"], bm=bm,
                             bn=_block(lw["w_big"].shape[1], 1024))

    gl_col, beta_col = 2 * h_a, 2 * h_a + h_b
    gbias = (_pad_lanes(lw["mlstm_b_i"], 0) + _pad_lanes(lw["mlstm_b_f"], h_a)
             + _pad_lanes(lw["gdn_dt_bias"], gl_col))
    alog_row = _pad_lanes(lw["gdn_A_log"], gl_col)

    ha_parts, ob_parts, new_states = [], [], []
    for st in streams:
        bsz = st["bsz"]
        m0 = jnp.zeros((bsz, 8, GATE_LANES), F32).at[:, :, :h_a].set(
            jnp.broadcast_to(st["m"][:, None, :], (bsz, 8, h_a)))
        ha, c1, n1, m1 = mlstm(
            p_big, p_small, gbias, lw["mlstm_norm"].reshape(1, a_v), st["C"], st["n"], m0,
            row0=st["row0"], seq=st["seq"], chunk=st["chunk_a"], heads=h_a, dk=dk_a, dv=dv_a,
            q_off=offs["q"], k_off=offs["k"], v_off=offs["v"], ao_off=offs["ao"])
        tail0 = jnp.pad(st["conv"], ((0, 0), (8 - (width - 1), 0), (0, 0)))
        ob, s1, tail1 = gdn(
            p_big, p_small, gbias, alog_row, lw["gdn_norm"].reshape(1, dv_b), lw["gdn_conv"], tail0,
            st["S"], row0=st["row0"], seq=st["seq"], chunk=st["chunk_b"],
            qkv_off=offs["bq"], z_off=offs["bz"], gl_col=gl_col, beta_col=beta_col)
        conv1 = tail1[:, 8 - (width - 1):, :]
        ha_parts.append(ha)
        ob_parts.append(ob)
        new_states.append((conv1, s1, c1, n1, m1[:, 0, :h_a]))

    merged = merge(ha_parts, ob_parts, lw["w_branch_a"], lw["w_branch_b"], p_big,
                   ga_off=offs["ga"], gb_off=offs["gb"], bm=bm, bn=bd)
    x = matmul_residual(merged, lw["w_out"], [x], scale=1.0, bm=bm, bn=bd_out)

    act = ffn_up([x], lw["norm3"], lw["ffn2_gate"], lw["ffn2_up"], bm=bm, bf=bf)
    x = matmul_residual(act, lw["ffn2_down"], [x], scale=0.5, bm=bm, bn=bd_down)
    return x, new_states


def _split_w_in_body(wt_ref, big_ref):
    big_ref[...] = wt_ref[...].T.astype(BF16)


def split_w_in(w_in, dims):
    h_a, dk_a, dv_a, h_b, dk_b, dv_b, _ = dims
    a_qk, a_v, b_k, b_v = h_a * dk_a, h_a * dv_a, h_b * dk_b, h_b * dv_b
    d, d_in = w_in.shape
    runs = [(2 * a_qk + a_v, True), (2 * h_a, False), (a_v + 2 * b_k + b_v, True), (2 * h_b, False),
            (b_v + 2 * d, True)]
    assert sum(r for r, _ in runs) == d_in
    big_segs, small_segs, src, big_dst = [], [], 0, 0
    for width, is_big in runs:
        if is_big:
            big_segs.append((src, width, big_dst))
            big_dst += width
        else:
            small_segs.append((src, width))
        src += width
    rb = 1024
    for s, width, _ in big_segs:
        assert s % 8 == 0
        rb = _block(width, rb)
    w_t = jnp.swapaxes(w_in, 0, 1)

    def src_row(i):
        row = i * rb
        for s, _, dst in big_segs[1:]:
            row = jnp.where(i * rb >= dst, i * rb + (s - dst), row)
        return pl.multiple_of(row, 8)

    w_big = pl.pallas_call(
        _split_w_in_body,
        grid=(big_dst // rb,),
        in_specs=[pl.BlockSpec((pl.Element(rb), pl.Element(d)), lambda i: (src_row(i), 0))],
        out_specs=pl.BlockSpec((d, rb), lambda i: (0, i)),
        out_shape=jax.ShapeDtypeStruct((d, big_dst), BF16),
        compiler_params=_params("parallel"),
        name="split_w_in",
    )(w_t)
    small = jnp.concatenate([w_t[s:s + width] for s, width in small_segs], axis=0)
    assert small.shape[0] <= GATE_LANES
    w_small = jnp.pad(small, ((0, GATE_LANES - small.shape[0]), (0, 0))).T
    return w_big, w_small


def _prep_layer_weights(l, w, dims):
    lw = {k: w[k][l] for k in w if k != "w_in"}
    for k in ("ffn1_down", "ffn2_down", "w_branch_a", "w_branch_b", "w_out"):
        lw[k] = lw[k].astype(BF16)
    lw["w_big"], lw["w_small"] = split_w_in(w["w_in"][l], dims)
    return lw


def kernel(x_prompt, x_sample, state_conv, state_gdn, state_mlstm_C, state_mlstm_n, state_mlstm_m,
           norm1, ffn1_gate, ffn1_up, ffn1_down, norm2, w_in, mlstm_b_i, mlstm_b_f, mlstm_norm,
           gdn_conv, gdn_A_log, gdn_dt_bias, gdn_norm, w_branch_a, w_branch_b, w_out,
           norm3, ffn2_gate, ffn2_up, ffn2_down, norm_f):
    depth = norm1.shape[0]
    bp, sp, d = x_prompt.shape
    bs, ss, _ = x_sample.shape
    _, _, h_a, dk_a, dv_a = state_mlstm_C.shape
    _, _, h_b, dk_b, dv_b = state_gdn.shape
    width = gdn_conv.shape[1]
    dims = (h_a, dk_a, dv_a, h_b, dk_b, dv_b, width)
    w = dict(norm1=norm1, ffn1_gate=ffn1_gate, ffn1_up=ffn1_up, ffn1_down=ffn1_down, norm2=norm2,
             w_in=w_in, mlstm_b_i=mlstm_b_i, mlstm_b_f=mlstm_b_f, mlstm_norm=mlstm_norm,
             gdn_conv=gdn_conv, gdn_A_log=gdn_A_log, gdn_dt_bias=gdn_dt_bias, gdn_norm=gdn_norm,
             w_branch_a=w_branch_a, w_branch_b=w_branch_b, w_out=w_out, norm3=norm3,
             ffn2_gate=ffn2_gate, ffn2_up=ffn2_up, ffn2_down=ffn2_down)

    x_parts = [x_prompt.reshape(bp * sp, d), x_sample.reshape(bs * ss, d)]
    zeros = lambda *shape: jnp.zeros(shape, F32)
    prompt = dict(row0=0, bsz=bp, seq=sp, chunk_a=math.gcd(sp, 256), chunk_b=math.gcd(sp, CHUNK))
    sample = dict(row0=bp * sp, bsz=bs, seq=ss, chunk_a=math.gcd(ss, 256), chunk_b=math.gcd(ss, CHUNK))

    per_layer = [[], []]
    for l in range(depth):
        lw = _prep_layer_weights(l, w, dims)
        prompt.update(conv=zeros(bp, width - 1, (2 * dk_b + dv_b) * h_b), S=zeros(bp, h_b, dk_b, dv_b),
                      C=zeros(bp, h_a, dk_a, dv_a), n=zeros(bp, h_a, dk_a), m=zeros(bp, h_a))
        sample.update(conv=state_conv[l], S=state_gdn[l], C=state_mlstm_C[l], n=state_mlstm_n[l],
                      m=state_mlstm_m[l])
        x, states = _layer(x_parts, [prompt, sample], lw, dims)
        x_parts = [x]
        per_layer[0].append(states[0])
        per_layer[1].append(states[1])

    part_rows = (bp * sp, bs * ss)
    y_prompt, y_sample = rmsnorm_rows(x, norm_f, part_rows,
                                      bm=_block(math.gcd(*part_rows), 512))
    stack = lambda sts: tuple(jnp.stack(z) for z in zip(*sts))
    return ((y_prompt.reshape(bp, sp, d), y_sample.reshape(bs, ss, d))
            + stack(per_layer[0]) + stack(per_layer[1]))
```

```python
import functools
import math

import jax
import jax.numpy as jnp
from jax import lax
from jax.experimental import pallas as pl
from jax.experimental.pallas import tpu as pltpu

F32 = jnp.float32
BF16 = jnp.bfloat16
HIGHEST = lax.Precision.HIGHEST
EPS = 1e-6
NEG = -1e30
CHUNK = 64
GATE_LANES = 128
V7X_VMEM_BYTES = 64 * 1024 * 1024
VMEM_LIMIT = V7X_VMEM_BYTES - 4 * 1024 * 1024


def _block(n, preferred):
    return math.gcd(n, preferred)


def _params(*semantics):
    return pltpu.CompilerParams(dimension_semantics=semantics, vmem_limit_bytes=VMEM_LIMIT)


def _sigmoid(x):
    return 1.0 / (1.0 + jnp.exp(-x))


def _softplus(x):
    return jnp.maximum(x, 0.0) + jnp.log1p(jnp.exp(-jnp.abs(x)))


def _rms_rows(x, gain):
    return x * lax.rsqrt(jnp.mean(x * x, axis=-1, keepdims=True) + EPS) * gain


def _mm(a, b):
    return jnp.dot(a, b, preferred_element_type=F32)


def _bmm(a, b):
    return jnp.einsum('hij,hjk->hik', a, b, preferred_element_type=F32)


def _bmm_nt(a, b):
    return jnp.einsum('hik,hjk->hij', a, b, preferred_element_type=F32)


def _bmm_tn(a, b):
    return jnp.einsum('hsk,hsv->hkv', a, b, preferred_element_type=F32)


def _split_heads(seqs, heads, width):
    return jnp.stack([a[:, i * width:(i + 1) * width] for a in seqs for i in range(heads)])


def _part_starts(parts, bm):
    starts, total = [], 0
    for p in parts:
        assert p.shape[0] % bm == 0
        starts.append(total)
        total += p.shape[0] // bm
    return tuple(starts), total


def _part_specs(parts, bm, cols, col_of, **spec_kwargs):
    starts, _ = _part_starts(parts, bm)
    specs = []
    for p, s in zip(parts, starts):
        nb = p.shape[0] // bm
        specs.append(pl.BlockSpec(
            (bm, cols), lambda i, *js, s=s, nb=nb: (jnp.clip(i - s, 0, nb - 1), col_of(*js)),
            **spec_kwargs))
    return specs


def _read_parts(refs, starts):
    val = refs[0][...]
    for r, s in zip(refs[1:], starts[1:]):
        val = jnp.where(pl.program_id(0) >= s, r[...], val)
    return val


def _on_owner_part(part_refs, starts, nrow, fn, also=True):
    if len(part_refs) == 1:
        if also is True:
            fn(part_refs[0])
        else:
            pl.when(also)(functools.partial(fn, part_refs[0]))
        return
    i = pl.program_id(0)
    ends = starts[1:] + (nrow,)
    for refs, lo, hi in zip(part_refs, starts, ends):
        pl.when((i >= lo) & (i < hi) & also)(functools.partial(fn, refs))


def _ffn_up_body(*refs, starts, nrow):
    x_refs = refs[:len(starts)]
    g_ref, wg_ref, wu_ref, act_ref, h_ref = refs[len(starts):]

    def norm_rows(x_ref):
        h_ref[...] = _rms_rows(x_ref[...], g_ref[...]).astype(BF16)

    _on_owner_part(x_refs, starts, nrow, norm_rows, also=pl.program_id(1) == 0)
    h = h_ref[...]
    gate = _mm(h, wg_ref[...].astype(BF16))
    up = _mm(h, wu_ref[...].astype(BF16))
    act_ref[...] = (gate * _sigmoid(gate) * up).astype(BF16)


def ffn_up(x_parts, gain, wg, wu, *, bm, bf):
    d, ff = wg.shape
    starts, nrow = _part_starts(x_parts, bm)
    x_mode = {}
    return pl.pallas_call(
        functools.partial(_ffn_up_body, starts=starts, nrow=nrow),
        grid=(nrow, ff // bf),
        in_specs=_part_specs(x_parts, bm, d, lambda j: 0, **x_mode) + [
            pl.BlockSpec((1, d), lambda i, j: (0, 0)),
            pl.BlockSpec((d, bf), lambda i, j: (0, j)),
            pl.BlockSpec((d, bf), lambda i, j: (0, j)),
        ],
        out_specs=pl.BlockSpec((bm, bf), lambda i, j: (i, j)),
        out_shape=jax.ShapeDtypeStruct((nrow * bm, ff), BF16),
        scratch_shapes=[pltpu.VMEM((bm, d), BF16)],
        compiler_params=_params("parallel", "arbitrary"),
        name="ffn_up",
    )(*x_parts, gain.reshape(1, d), wg, wu)


def _mm_res_body(a_ref, w_ref, *refs, scale, starts):
    x_refs, o_ref = refs[:-1], refs[-1]
    o_ref[...] = _read_parts(x_refs, starts) + scale * _mm(a_ref[...], w_ref[...])


def matmul_residual(a, w, x_parts, *, scale, bm, bn):
    m, k = a.shape
    n = w.shape[1]
    starts, nrow = _part_starts(x_parts, bm)
    assert nrow * bm == m
    return pl.pallas_call(
        functools.partial(_mm_res_body, scale=scale, starts=starts),
        grid=(nrow, n // bn),
        in_specs=[
            pl.BlockSpec((bm, k), lambda i, j: (i, 0)),
            pl.BlockSpec((k, bn), lambda i, j: (0, j)),
        ] + _part_specs(x_parts, bm, bn, lambda j: j),
        out_specs=pl.BlockSpec((bm, bn), lambda i, j: (i, j)),
        out_shape=jax.ShapeDtypeStruct((m, n), F32),
        compiler_params=_params("parallel", "arbitrary"),
        name="matmul_residual",
    )(a, w, *x_parts)


def _in_proj_body(x_ref, g_ref, w_ref, ws_ref, p_ref, ps_ref, h_ref):
    @pl.when(pl.program_id(1) == 0)
    def _():
        h = _rms_rows(x_ref[...], g_ref[...])
        h_hi = h.astype(BF16)
        h_ref[...] = h_hi
        h_lo = (h - h_hi.astype(F32)).astype(BF16)
        both = _mm(h_hi, ws_ref[...])
        ps_ref[...] = (both[:, :GATE_LANES] + both[:, GATE_LANES:]
                       + _mm(h_lo, ws_ref[:, :GATE_LANES]))

    p_ref[...] = _mm(h_ref[...], w_ref[...])


def in_proj(x, gain, w_big, w_small, *, bm, bn):
    m, d = x.shape
    n = w_big.shape[1]
    ws_hi = w_small.astype(BF16)
    ws_lo = (w_small - ws_hi.astype(F32)).astype(BF16)
    ws_split = jnp.concatenate([ws_hi, ws_lo], axis=1)
    return pl.pallas_call(
        _in_proj_body,
        grid=(m // bm, n // bn),
        in_specs=[
            pl.BlockSpec((bm, d), lambda i, j: (i, 0)),
            pl.BlockSpec((1, d), lambda i, j: (0, 0)),
            pl.BlockSpec((d, bn), lambda i, j: (0, j)),
            pl.BlockSpec((d, 2 * GATE_LANES), lambda i, j: (0, 0)),
        ],
        out_specs=[
            pl.BlockSpec((bm, bn), lambda i, j: (i, j)),
            pl.BlockSpec((bm, GATE_LANES), lambda i, j: (i, 0)),
        ],
        out_shape=[
            jax.ShapeDtypeStruct((m, n), F32),
            jax.ShapeDtypeStruct((m, GATE_LANES), F32),
        ],
        scratch_shapes=[pltpu.VMEM((bm, d), BF16)],
        compiler_params=_params("parallel", "arbitrary"),
        name="in_proj",
    )(x, gain.reshape(1, d), w_big, ws_split)


def _merge_body(*refs, starts):
    n = len(starts)
    ha_refs, ob_refs = refs[:n], refs[n:2 * n]
    wa_ref, wb_ref, ga_ref, gb_ref, o_ref = refs[2 * n:]
    ya = _mm(_read_parts(ha_refs, starts), wa_ref[...])
    yb = _mm(_read_parts(ob_refs, starts), wb_ref[...])
    o_ref[...] = (_sigmoid(ga_ref[...]) * ya + _sigmoid(gb_ref[...]) * yb).astype(BF16)


def merge(ha_parts, ob_parts, wa, wb, p_big, *, ga_off, gb_off, bm, bn):
    ka, n = wa.shape
    kb = wb.shape[0]
    assert ga_off % bn == 0 and gb_off % bn == 0
    ga_blk, gb_blk = ga_off // bn, gb_off // bn
    starts, nrow = _part_starts(ha_parts, bm)
    assert _part_starts(ob_parts, bm) == (starts, nrow) and nrow * bm == p_big.shape[0]
    return pl.pallas_call(
        functools.partial(_merge_body, starts=starts),
        grid=(nrow, n // bn),
        in_specs=_part_specs(ha_parts, bm, ka, lambda j: 0) + _part_specs(ob_parts, bm, kb, lambda j: 0) + [
            pl.BlockSpec((ka, bn), lambda i, j: (0, j)),
            pl.BlockSpec((kb, bn), lambda i, j: (0, j)),
            pl.BlockSpec((bm, bn), lambda i, j: (i, ga_blk + j)),
            pl.BlockSpec((bm, bn), lambda i, j: (i, gb_blk + j)),
        ],
        out_specs=pl.BlockSpec((bm, bn), lambda i, j: (i, j)),
        out_shape=jax.ShapeDtypeStruct((nrow * bm, n), BF16),
        compiler_params=_params("parallel", "arbitrary"),
        name="merge",
    )(*ha_parts, *ob_parts, wa, wb, p_big, p_big)


def _mm_res_norm_body(a_ref, w_ref, x_ref, g_ref, *o_refs, scale, starts, nrow):
    y = _rms_rows(x_ref[...] + scale * _mm(a_ref[...], w_ref[...]), g_ref[...])

    def store(o_ref):
        o_ref[...] = y

    _on_owner_part(o_refs, starts, nrow, store)


def matmul_residual_norm(a, w, x, gain, part_rows, *, scale, bm):
    m, k = a.shape
    n = w.shape[1]
    parts = [jax.ShapeDtypeStruct((r, n), F32) for r in part_rows]
    starts, nrow = _part_starts(parts, bm)
    assert nrow * bm == m
    return pl.pallas_call(
        functools.partial(_mm_res_norm_body, scale=scale, starts=starts, nrow=nrow),
        grid=(nrow,),
        in_specs=[
            pl.BlockSpec((bm, k), lambda i: (i, 0)),
            pl.BlockSpec((k, n), lambda i: (0, 0), pipeline_mode=pl.Buffered(1)),
            pl.BlockSpec((bm, n), lambda i: (i, 0)),
            pl.BlockSpec((1, n), lambda i: (0, 0)),
        ],
        out_specs=_part_specs(parts, bm, n, lambda: 0),
        out_shape=parts,
        compiler_params=_params("arbitrary"),
        name="matmul_residual_norm",
    )(a, w, x, gain.reshape(1, n))


def _mlstm_body(q_ref, k_ref, v_ref, ao_ref, gc_ref, gbias_ref, norm_ref, c0_ref, n0_ref, m0_ref,
                ha_ref, c_ref, n_ref, m_ref, *, heads, dk, dv, chunk, bb, single_chunk):
    L = chunk
    if single_chunk:
        c_src, n_src, m_src = c0_ref, n0_ref, m0_ref
    else:
        c_src, n_src, m_src = c_ref, n_ref, m_ref

        @pl.when(pl.program_id(1) == 0)
        def _():
            c_ref[...] = c0_ref[...]
            n_ref[...] = n0_ref[...]
            m_ref[...] = m0_ref[...]

    row = lax.broadcasted_iota(jnp.int32, (L, L), 0)
    col = lax.broadcasted_iota(jnp.int32, (L, L), 1)
    causal = (row >= col)[None]
    tri = jnp.where(row >= col, 1.0, 0.0).astype(F32)
    pairs = [(bi, h) for bi in range(bb) for h in range(heads)]

    ig_cols, ig_rows, b_cols, b_rows = [], [], [], []
    for bi in range(bb):
        gates = gc_ref[bi * L:(bi + 1) * L, :] + gbias_ref[...]
        log_f = jnp.minimum(gates, 0.0) - jnp.log1p(jnp.exp(-jnp.abs(gates)))
        b_all = jnp.dot(tri, log_f, precision=HIGHEST, preferred_element_type=F32)
        gates_t, b_all_t = gates.T, b_all.T
        ig_cols += [gates[:, h:h + 1] for h in range(heads)]
        ig_rows += [gates_t[h:h + 1, :] for h in range(heads)]
        b_cols += [b_all[:, heads + h:heads + h + 1] for h in range(heads)]
        b_rows += [b_all_t[heads + h:heads + h + 1, :] for h in range(heads)]
    ig_c, ig_r = jnp.stack(ig_cols), jnp.stack(ig_rows)
    b_c, b_r = jnp.stack(b_cols), jnp.stack(b_rows)
    m_prev = jnp.stack([m_src[bi][0:1, h:h + 1] for bi, h in pairs])

    log_d = jnp.where(causal, b_c - b_r + ig_r, NEG)
    m_inter = m_prev + b_c
    m_t = jnp.maximum(m_inter, jnp.max(log_d, axis=-1, keepdims=True))
    d_mat = jnp.exp(log_d - m_t)
    inter = jnp.exp(m_inter - m_t)

    seq_rows = lambda ref: [ref[bi * L:(bi + 1) * L, :] for bi in range(bb)]
    q = _split_heads(seq_rows(q_ref), heads, dk)
    k = _split_heads(seq_rows(k_ref), heads, dk) * (dk ** -0.5)
    v = _split_heads(seq_rows(v_ref), heads, dv)
    qb, kb, vb = q.astype(BF16), k.astype(BF16), v.astype(BF16)
    c_old = c_src[...].reshape(bb * heads, dk, dv)
    n_old = jnp.stack([n_src[bi, h:h + 1, :] for bi, h in pairs])

    s = _bmm_nt(qb, kb) * d_mat
    num = inter * _bmm(qb, c_old.astype(BF16)) + _bmm(s.astype(BF16), vb)
    den = inter * jnp.sum(q * n_old, axis=-1, keepdims=True) + jnp.sum(s, axis=-1, keepdims=True)
    h_out = num / jnp.maximum(jnp.abs(den), jnp.exp(-m_t))

    m_new = m_t[:, L - 1:L, :]
    b_last = b_c[:, L - 1:L, :]
    carry = jnp.exp(m_prev + b_last - m_new)
    kw = k * jnp.exp(b_last - b_c + ig_c - m_new)
    c_new = carry * c_old + _bmm_tn(kw.astype(BF16), vb)
    c_ref[...] = c_new.reshape(bb, heads, dk, dv)
    n_new = carry * n_old + jnp.sum(kw, axis=1, keepdims=True)

    gain = jnp.stack([norm_ref[:, h * dv:(h + 1) * dv] for _, h in pairs])
    gate_o = _sigmoid(_split_heads(seq_rows(ao_ref), heads, dv))
    ha = (_rms_rows(h_out, gain) * gate_o).astype(BF16)

    lane = lax.broadcasted_iota(jnp.int32, (8, GATE_LANES), 1)
    for bi in range(bb):
        m_next = m_src[bi]
        for h in range(heads):
            i = bi * heads + h
            n_ref[bi, h:h + 1, :] = n_new[i]
            m_next = jnp.where(lane == h, m_new[i], m_next)
            ha_ref[bi * L:(bi + 1) * L, h * dv:(h + 1) * dv] = ha[i]
        m_ref[bi] = m_next


def mlstm(p_big, p_small, gbias, norm_row, c0, n0, m0, *, row0, seq, chunk, heads, dk, dv,
          q_off, k_off, v_off, ao_off):
    bsz = c0.shape[0]
    nc = seq // chunk
    bb = math.gcd(bsz, 4) if nc == 1 else 1
    blk = bb * chunk
    qk_w, v_w = heads * dk, heads * dv
    assert q_off % qk_w == 0 and k_off % qk_w == 0 and v_off % v_w == 0 and ao_off % v_w == 0
    assert row0 % blk == 0
    rows = lambda b, c: row0 // blk + b * nc + c
    body = functools.partial(_mlstm_body, heads=heads, dk=dk, dv=dv, chunk=chunk, bb=bb,
                             single_chunk=(nc == 1))
    return pl.pallas_call(
        body,
        grid=(bsz // bb, nc),
        in_specs=[
            pl.BlockSpec((blk, qk_w), lambda b, c: (rows(b, c), q_off // qk_w)),
            pl.BlockSpec((blk, qk_w), lambda b, c: (rows(b, c), k_off // qk_w)),
            pl.BlockSpec((blk, v_w), lambda b, c: (rows(b, c), v_off // v_w)),
            pl.BlockSpec((blk, v_w), lambda b, c: (rows(b, c), ao_off // v_w)),
            pl.BlockSpec((blk, GATE_LANES), lambda b, c: (rows(b, c), 0)),
            pl.BlockSpec((1, GATE_LANES), lambda b, c: (0, 0)),
            pl.BlockSpec((1, v_w), lambda b, c: (0, 0)),
            pl.BlockSpec((bb, heads, dk, dv), lambda b, c: (b, 0, 0, 0)),
            pl.BlockSpec((bb, heads, dk), lambda b, c: (b, 0, 0)),
            pl.BlockSpec((bb, 8, GATE_LANES), lambda b, c: (b, 0, 0)),
        ],
        out_specs=[
            pl.BlockSpec((blk, v_w), lambda b, c: (b * nc + c, 0)),
            pl.BlockSpec((bb, heads, dk, dv), lambda b, c: (b, 0, 0, 0)),
            pl.BlockSpec((bb, heads, dk), lambda b, c: (b, 0, 0)),
            pl.BlockSpec((bb, 8, GATE_LANES), lambda b, c: (b, 0, 0)),
        ],
        out_shape=[
            jax.ShapeDtypeStruct((bsz * seq, v_w), BF16),
            jax.ShapeDtypeStruct(c0.shape, F32),
            jax.ShapeDtypeStruct(n0.shape, F32),
            jax.ShapeDtypeStruct(m0.shape, F32),
        ],
        compiler_params=_params("parallel", "arbitrary"),
        name="mlstm",
    )(p_big, p_big, p_big, p_big, p_small, gbias, norm_row, c0, n0, m0)


def _shift_rows(x, prev, j):
    xs = pltpu.roll(x, j, axis=0)
    row = lax.broadcasted_iota(jnp.int32, prev.shape, 0)
    head = jnp.where(row < j, pltpu.roll(prev, j, axis=0), xs[0:8])
    if x.shape[0] == 8:
        return head
    return jnp.concatenate([head, xs[8:]], axis=0)


def _conv_silu(x_ref, w_ref, tail_src, tail_ref, width, chunk, nsub):
    L = chunk
    outs = []
    for bi in range(tail_ref.shape[0]):
        for sub in range(nsub):
            r0 = (bi * nsub + sub) * L
            x = x_ref[r0:r0 + L, :]
            prev = tail_src[bi] if sub == 0 else x_ref[r0 - 8:r0, :]
            acc = x * w_ref[width - 1:width, :]
            for j in range(1, width):
                acc = acc + _shift_rows(x, prev, j) * w_ref[width - 1 - j:width - j, :]
            outs.append(acc * _sigmoid(acc))
        tail_ref[bi] = x_ref[(bi + 1) * nsub * L - 8:(bi + 1) * nsub * L, :]
    return outs


def _gdn_body(x_ref, z_ref, w_ref, t0_ref, gc_ref, gbias_ref, alog_ref, norm_ref, s0_ref,
              ob_ref, s_ref, t_ref, *, heads, dk, dv, chunk, nsub, width, gl_col, beta_col,
              single_chunk):
    L = chunk
    bb = s_ref.shape[0]
    assert bb == 1 or nsub == 1
    if single_chunk:
        s_src, t_src = s0_ref, t0_ref
    else:
        s_src, t_src = s_ref, t_ref

        @pl.when(pl.program_id(1) == 0)
        def _():
            s_ref[...] = s0_ref[...]
            t_ref[...] = t0_ref[...]

    hw = heads * dk
    qkv = _conv_silu(x_ref, w_ref, t_src, t_ref, width, L, nsub)
    q = _split_heads([c[:, :hw] for c in qkv], heads, dk)
    k = _split_heads([c[:, hw:2 * hw] for c in qkv], heads, dk)
    v = _split_heads([c[:, 2 * hw:] for c in qkv], heads, dv)
    q = q * lax.rsqrt(jnp.sum(q * q, axis=-1, keepdims=True) + EPS) * (dk ** -0.5)
    k = k * lax.rsqrt(jnp.sum(k * k, axis=-1, keepdims=True) + EPS)

    row = lax.broadcasted_iota(jnp.int32, (L, L), 0)
    col = lax.broadcasted_iota(jnp.int32, (L, L), 1)
    causal = (row >= col)[None]
    strict = (row > col)[None]
    tri = jnp.where(row >= col, 1.0, 0.0).astype(F32)

    g_cols, g_rows, beta_cols = [], [], []
    for u in range(bb * nsub):
        raw = gc_ref[u * L:(u + 1) * L, :] + gbias_ref[...]
        gl_all = -jnp.exp(alog_ref[...]) * _softplus(raw)
        g_all = jnp.dot(tri, gl_all, precision=HIGHEST, preferred_element_type=F32)
        g_all_t = g_all.T
        beta_all = _sigmoid(raw)
        g_cols += [g_all[:, gl_col + i:gl_col + i + 1] for i in range(heads)]
        g_rows += [g_all_t[gl_col + i:gl_col + i + 1, :] for i in range(heads)]
        beta_cols += [beta_all[:, beta_col + i:beta_col + i + 1] for i in range(heads)]
    g_c, g_r, beta_c = jnp.stack(g_cols), jnp.stack(g_rows), jnp.stack(beta_cols)
    eg_c = jnp.exp(g_c)
    decay = jnp.exp(jnp.where(causal, g_c - g_r, NEG))

    kb = k.astype(BF16)
    both = _bmm_nt(jnp.concatenate([k, q], axis=1).astype(BF16), kb)
    kk, qk = both[:, :L], both[:, L:]

    x_pow = jnp.where(strict, -(beta_c * kk * decay), 0.0)
    n_inv = x_pow
    for _ in range(max(int(math.ceil(math.log2(L))) - 1, 0)):
        xb = x_pow.astype(BF16)
        x_pow = _bmm(xb, xb)
        n_inv = n_inv + x_pow + _bmm(n_inv.astype(BF16), x_pow.astype(BF16))
    rhs = jnp.concatenate([v * beta_c, k * (beta_c * eg_c)], axis=-1)
    sol = rhs + _bmm(n_inv.astype(BF16), rhs.astype(BF16))
    wq = jnp.concatenate([sol[..., dv:], q], axis=1).astype(BF16)
    u_all = sol[..., :dv]
    qkd = (qk * decay).astype(BF16)
    g_last = g_c[:, L - 1:L, :]
    kd = (k * jnp.exp(g_last - g_c)).astype(BF16)

    s_cur = s_src[...].reshape(bb * heads, dk, dv)
    hs = bb * heads
    for sub in range(nsub):
        sl = slice(sub * hs, (sub + 1) * hs)
        ws_qs = _bmm(wq[sl], s_cur.astype(BF16))
        db = (u_all[sl] - ws_qs[:, :L]).astype(BF16)
        o = _rms_rows(eg_c[sl] * ws_qs[:, L:] + _bmm(qkd[sl], db), norm_ref[...])
        s_cur = jnp.exp(g_last[sl]) * s_cur + _bmm_tn(kd[sl], db)
        for bi in range(bb):
            r0 = (bi * nsub + sub) * L
            for i in range(heads):
                z = z_ref[r0:r0 + L, i * dv:(i + 1) * dv]
                ob_ref[r0:r0 + L, i * dv:(i + 1) * dv] = (
                    o[bi * heads + i] * (z * _sigmoid(z))).astype(BF16)
    s_ref[...] = s_cur.reshape(bb, heads, dk, dv)


def gdn(p_big, p_small, gbias, alog_row, norm_row, conv_w, tail0, s0, *, row0, seq, chunk,
        qkv_off, z_off, gl_col, beta_col):
    bsz, heads, dk, dv = s0.shape
    assert dk == dv
    width = conv_w.shape[0]
    nc = seq // chunk
    bb = math.gcd(bsz, 4) if nc == 1 else 1
    nsub = 2 if nc % 2 == 0 else 1
    steps = nc // nsub
    blk = bb * nsub * chunk
    hw = heads * dk
    assert qkv_off % (3 * hw) == 0 and z_off % hw == 0 and row0 % blk == 0
    rows = lambda b, c: row0 // blk + b * steps + c
    tail_spec = pl.BlockSpec((bb, 8, 3 * hw), lambda b, c: (b, 0, 0))
    s_spec = pl.BlockSpec((bb, heads, dk, dv), lambda b, c: (b, 0, 0, 0))
    vec_spec = lambda n: pl.BlockSpec((1, n), lambda b, c: (0, 0))
    body = functools.partial(_gdn_body, heads=heads, dk=dk, dv=dv, chunk=chunk, nsub=nsub, width=width,
                             gl_col=gl_col, beta_col=beta_col, single_chunk=(nc == 1))
    return pl.pallas_call(
        body,
        grid=(bsz // bb, steps),
        in_specs=[
            pl.BlockSpec((blk, 3 * hw), lambda b, c: (rows(b, c), qkv_off // (3 * hw))),
            pl.BlockSpec((blk, hw), lambda b, c: (rows(b, c), z_off // hw)),
            pl.BlockSpec((width, 3 * hw), lambda b, c: (0, 0)),
            tail_spec,
            pl.BlockSpec((blk, GATE_LANES), lambda b, c: (rows(b, c), 0)),
            vec_spec(GATE_LANES), vec_spec(GATE_LANES), vec_spec(dv), s_spec,
        ],
        out_specs=[pl.BlockSpec((blk, hw), lambda b, c: (b * steps + c, 0)), s_spec, tail_spec],
        out_shape=[
            jax.ShapeDtypeStruct((bsz * seq, heads * dv), BF16),
            jax.ShapeDtypeStruct(s0.shape, F32),
            jax.ShapeDtypeStruct(tail0.shape, F32),
        ],
        compiler_params=_params("parallel", "arbitrary"),
        name="gdn",
    )(p_big, p_big, conv_w, tail0, p_small, gbias, alog_row, norm_row, s0)


def _pad_lanes(vec, offset):
    return jnp.zeros((1, GATE_LANES), F32).at[0, offset:offset + vec.shape[0]].set(vec.astype(F32))


def _layer(x_parts, streams, lw, dims, final=None):
    h_a, dk_a, dv_a, h_b, dk_b, dv_b, width = dims
    a_qk, a_v, b_k, b_v = h_a * dk_a, h_a * dv_a, h_b * dk_b, h_b * dv_b
    d = x_parts[0].shape[1]
    ff = lw["ffn1_gate"].shape[1]
    bm = 1024
    for p in x_parts:
        bm = _block(p.shape[0], bm)
    bf, bd, bd_down, bd_out = _block(ff, 512), _block(d, 512), _block(d, 512), _block(d, 1024)

    act = ffn_up(x_parts, lw["norm1"], lw["ffn1_gate"], lw["ffn1_up"], bm=bm, bf=bf)
    x = matmul_residual(act, lw["ffn1_down"], x_parts, scale=0.5, bm=bm, bn=bd_down)

    offs = dict(q=0, k=a_qk, v=2 * a_qk, ao=2 * a_qk + a_v)
    offs["bq"] = offs["ao"] + a_v
    offs["bk"] = offs["bq"] + b_k
    offs["bv"] = offs["bk"] + b_k
    offs["bz"] = offs["bv"] + b_v
    offs["ga"] = offs["bz"] + b_v
    offs["gb"] = offs["ga"] + d
    p_big, p_small = in_proj(x, lw["norm2"], lw["w_big"], lw["w_small"], bm=bm,
                             bn=_block(lw["w_big"].shape[1], 1024))

    gl_col, beta_col = 2 * h_a, 2 * h_a + h_b
    gbias = (_pad_lanes(lw["mlstm_b_i"], 0) + _pad_lanes(lw["mlstm_b_f"], h_a)
             + _pad_lanes(lw["gdn_dt_bias"], gl_col))
    alog_row = _pad_lanes(lw["gdn_A_log"], gl_col)

    ha_parts, ob_parts, new_states = [], [], []
    for st in streams:
        bsz = st["bsz"]
        m0 = jnp.zeros((bsz, 8, GATE_LANES), F32).at[:, :, :h_a].set(
            jnp.broadcast_to(st["m"][:, None, :], (bsz, 8, h_a)))
        ha, c1, n1, m1 = mlstm(
            p_big, p_small, gbias, lw["mlstm_norm"].reshape(1, a_v), st["C"], st["n"], m0,
            row0=st["row0"], seq=st["seq"], chunk=st["chunk_a"], heads=h_a, dk=dk_a, dv=dv_a,
            q_off=offs["q"], k_off=offs["k"], v_off=offs["v"], ao_off=offs["ao"])
        tail0 = jnp.pad(st["conv"], ((0, 0), (8 - (width - 1), 0), (0, 0)))
        ob, s1, tail1 = gdn(
            p_big, p_small, gbias, alog_row, lw["gdn_norm"].reshape(1, dv_b), lw["gdn_conv"], tail0,
            st["S"], row0=st["row0"], seq=st["seq"], chunk=st["chunk_b"],
            qkv_off=offs["bq"], z_off=offs["bz"], gl_col=gl_col, beta_col=beta_col)
        conv1 = tail1[:, 8 - (width - 1):, :]
        ha_parts.append(ha)
        ob_parts.append(ob)
        new_states.append((conv1, s1, c1, n1, m1[:, 0, :h_a]))

    merged = merge(ha_parts, ob_parts, lw["w_branch_a"], lw["w_branch_b"], p_big,
                   ga_off=offs["ga"], gb_off=offs["gb"], bm=bm, bn=bd)
    x = matmul_residual(merged, lw["w_out"], [x], scale=1.0, bm=bm, bn=bd_out)

    act = ffn_up([x], lw["norm3"], lw["ffn2_gate"], lw["ffn2_up"], bm=bm, bf=bf)
    if final is None:
        x = matmul_residual(act, lw["ffn2_down"], [x], scale=0.5, bm=bm, bn=bd_down)
    else:
        norm_f, part_rows = final
        x = matmul_residual_norm(act, lw["ffn2_down"], x, norm_f, part_rows, scale=0.5,
                                 bm=_block(math.gcd(*part_rows), 256))
    return x, new_states


def _split_w_in_body(wt_ref, big_ref):
    big_ref[...] = wt_ref[...].T.astype(BF16)


def split_w_in(w_in, dims):
    h_a, dk_a, dv_a, h_b, dk_b, dv_b, _ = dims
    a_qk, a_v, b_k, b_v = h_a * dk_a, h_a * dv_a, h_b * dk_b, h_b * dv_b
    d, d_in = w_in.shape
    runs = [(2 * a_qk + a_v, True), (2 * h_a, False), (a_v + 2 * b_k + b_v, True), (2 * h_b, False),
            (b_v + 2 * d, True)]
    assert sum(r for r, _ in runs) == d_in
    big_segs, small_segs, src, big_dst = [], [], 0, 0
    for width, is_big in runs:
        if is_big:
            big_segs.append((src, width, big_dst))
            big_dst += width
        else:
            small_segs.append((src, width))
        src += width
    rb = 1024
    for s, width, _ in big_segs:
        assert s % 8 == 0
        rb = _block(width, rb)
    w_t = jnp.swapaxes(w_in, 0, 1)

    def src_row(i):
        row = i * rb
        for s, _, dst in big_segs[1:]:
            row = jnp.where(i * rb >= dst, i * rb + (s - dst), row)
        return pl.multiple_of(row, 8)

    w_big = pl.pallas_call(
        _split_w_in_body,
        grid=(big_dst // rb,),
        in_specs=[pl.BlockSpec((pl.Element(rb), pl.Element(d)), lambda i: (src_row(i), 0))],
        out_specs=pl.BlockSpec((d, rb), lambda i: (0, i)),
        out_shape=jax.ShapeDtypeStruct((d, big_dst), BF16),
        compiler_params=_params("parallel"),
        name="split_w_in",
    )(w_t)
    small = jnp.concatenate([w_t[s:s + width] for s, width in small_segs], axis=0)
    assert small.shape[0] <= GATE_LANES
    w_small = jnp.pad(small, ((0, GATE_LANES - small.shape[0]), (0, 0))).T
    return w_big, w_small


def _prep_layer_weights(l, w, dims):
    lw = {k: w[k][l] for k in w if k != "w_in"}
    for k in ("ffn1_down", "ffn2_down", "w_branch_a", "w_branch_b", "w_out"):
        lw[k] = lw[k].astype(BF16)
    lw["w_big"], lw["w_small"] = split_w_in(w["w_in"][l], dims)
    return lw


def kernel(x_prompt, x_sample, state_conv, state_gdn, state_mlstm_C, state_mlstm_n, state_mlstm_m,
           norm1, ffn1_gate, ffn1_up, ffn1_down, norm2, w_in, mlstm_b_i, mlstm_b_f, mlstm_norm,
           gdn_conv, gdn_A_log, gdn_dt_bias, gdn_norm, w_branch_a, w_branch_b, w_out,
           norm3, ffn2_gate, ffn2_up, ffn2_down, norm_f):
    depth = norm1.shape[0]
    bp, sp, d = x_prompt.shape
    bs, ss, _ = x_sample.shape
    _, _, h_a, dk_a, dv_a = state_mlstm_C.shape
    _, _, h_b, dk_b, dv_b = state_gdn.shape
    width = gdn_conv.shape[1]
    dims = (h_a, dk_a, dv_a, h_b, dk_b, dv_b, width)
    w = dict(norm1=norm1, ffn1_gate=ffn1_gate, ffn1_up=ffn1_up, ffn1_down=ffn1_down, norm2=norm2,
             w_in=w_in, mlstm_b_i=mlstm_b_i, mlstm_b_f=mlstm_b_f, mlstm_norm=mlstm_norm,
             gdn_conv=gdn_conv, gdn_A_log=gdn_A_log, gdn_dt_bias=gdn_dt_bias, gdn_norm=gdn_norm,
             w_branch_a=w_branch_a, w_branch_b=w_branch_b, w_out=w_out, norm3=norm3,
             ffn2_gate=ffn2_gate, ffn2_up=ffn2_up, ffn2_down=ffn2_down)

    x_parts = [x_prompt.reshape(bp * sp, d), x_sample.reshape(bs * ss, d)]
    zeros = lambda *shape: jnp.zeros(shape, F32)
    prompt = dict(row0=0, bsz=bp, seq=sp, chunk_a=math.gcd(sp, 256), chunk_b=math.gcd(sp, CHUNK))
    sample = dict(row0=bp * sp, bsz=bs, seq=ss, chunk_a=math.gcd(ss, 256), chunk_b=math.gcd(ss, CHUNK))

    per_layer = [[], []]
    for l in range(depth):
        lw = _prep_layer_weights(l, w, dims)
        prompt.update(conv=zeros(bp, width - 1, (2 * dk_b + dv_b) * h_b), S=zeros(bp, h_b, dk_b, dv_b),
                      C=zeros(bp, h_a, dk_a, dv_a), n=zeros(bp, h_a, dk_a), m=zeros(bp, h_a))
        sample.update(conv=state_conv[l], S=state_gdn[l], C=state_mlstm_C[l], n=state_mlstm_n[l],
                      m=state_mlstm_m[l])
        final = (norm_f, (bp * sp, bs * ss)) if l == depth - 1 else None
        x, states = _layer(x_parts, [prompt, sample], lw, dims, final=final)
        x_parts = [x]
        per_layer[0].append(states[0])
        per_layer[1].append(states[1])

    y_prompt, y_sample = x
    stack = lambda sts: tuple(jnp.stack(z) for z in zip(*sts))
    return ((y_prompt.reshape(bp, sp, d), y_sample.reshape(bs, ss, d))
            + stack(per_layer[0]) + stack(per_layer[1]))
```
